```python
import jax
import jax.numpy as jnp
from jax import lax
import numpy as np

D_MODEL = 1024
BATCH = 2
SEQ = 8192
DEPTH = 2

GRID_W = 64
CTX_LEN = 256

H_A = 16
DH_A = 64
D_A = H_A * DH_A
WIN_H = 8
WIN_W = 16
ROPE_BASE = 10000.0

D_B = D_MODEL
NB_B = 16
BS_B = D_B // NB_B
CONV_W = 4
RG_C = 8.0

H_C = 16
N_C = 64
D_C = H_C * N_C
R_DECAY = 64
R_ICLR = 64
R_VRES = 32
R_GATE = 160
LNX_EPS = 64e-5

D_FF = 4 * D_MODEL
NORM_EPS = 1e-6
NEG_INF = -1e30

IN_WIDTHS = (D_A, D_A, D_A, D_B, D_B, D_C, D_C, D_C, D_MODEL, D_MODEL, D_MODEL)
N_IN = 3 * D_A + 2 * D_B + 3 * D_C + 3 * D_MODEL

kernel_name = 'hybrid_natten_rglru_rwkv7_dit_trunk'


def _rmsnorm(x, g):
    xf = x.astype(jnp.float32)
    y = xf * lax.rsqrt(jnp.mean(xf * xf, axis=-1, keepdims=True) + NORM_EPS)
    return (y * g.astype(jnp.float32)).astype(x.dtype)


def _split_in(z):
    return jnp.split(z, [int(s) for s in np.cumsum(IN_WIDTHS)[:-1]], axis=-1)


def _neighbour_mean(x):
    xp = jnp.pad(x, ((0, 0), (1, 1), (0, 0)))
    return 0.5 * (xp[:, :-2] + xp[:, 2:])


def _dwconv(x, w, b):
    pad_l = CONV_W // 2
    y = lax.conv_general_dilated(x, w[:, None, :].astype(x.dtype), window_strides=(1,),
                                 padding=[(pad_l, CONV_W - 1 - pad_l)],
                                 dimension_numbers=('NWC', 'WIO', 'NWC'),
                                 feature_group_count=x.shape[-1])
    return y + b


def _axial_rope(x, row, col):
    half = x.shape[-1] // 2
    nf = half // 2
    inv_freq = ROPE_BASE ** (-jnp.arange(nf, dtype=jnp.float32) / nf)

    def rotate(xp, pos):
        ang = pos.astype(jnp.float32)[:, None] * inv_freq
        cos = jnp.cos(ang)[None, :, None, :]
        sin = jnp.sin(ang)[None, :, None, :]
        x1 = xp[..., :nf].astype(jnp.float32)
        x2 = xp[..., nf:].astype(jnp.float32)
        return jnp.concatenate([x1 * cos - x2 * sin, x1 * sin + x2 * cos], axis=-1)

    return jnp.concatenate([rotate(x[..., :half], row), rotate(x[..., half:], col)], axis=-1).astype(x.dtype)


def _natten_mix(q, k, v, qc, kc, vc, rpb, ctx_out):
    B, L, H, dh = q.shape
    rows = L // GRID_W
    kh = min(WIN_H, rows)
    scale = dh ** -0.5
    f32 = jnp.float32
    t = jnp.arange(L)
    q = _axial_rope(q, t // GRID_W, t % GRID_W)
    k = _axial_rope(k, t // GRID_W, t % GRID_W)
    qg = q.reshape(B, rows, GRID_W, H, dh)
    kg = k.reshape(B, rows, GRID_W, H, dh)
    vg = v.reshape(B, rows, GRID_W, H, dh)
    col = jnp.arange(GRID_W)
    c0 = jnp.clip(col - WIN_W // 2, 0, GRID_W - WIN_W)
    in_win = (col[None, :] >= c0[:, None]) & (col[None, :] < c0[:, None] + WIN_W)
    dc_idx = jnp.clip(col[None, :] - col[:, None] + WIN_W - 1, 0, 2 * WIN_W - 2)

    def row_block(r):
        r0 = jnp.clip(r - kh // 2, 0, rows - kh)
        q_r = lax.dynamic_index_in_dim(qg, r, axis=1, keepdims=False)
        k_b = lax.dynamic_slice_in_dim(kg, r0, kh, axis=1)
        v_b = lax.dynamic_slice_in_dim(vg, r0, kh, axis=1)
        s_loc = jnp.einsum('bqhd,bikhd->bhqik', q_r, k_b, preferred_element_type=f32) * scale
        dr_idx = r0 + jnp.arange(kh) - r + WIN_H - 1
        bias = rpb[:, dr_idx][:, :, dc_idx].astype(f32)
        s_loc = jnp.where(in_win[:, None, :], s_loc + jnp.transpose(bias, (0, 2, 1, 3)), NEG_INF)
        s_ctx = jnp.einsum('bqhd,bchd->bhqc', q_r, kc, preferred_element_type=f32) * scale
        s_all = jnp.concatenate([s_loc.reshape(B, H, GRID_W, kh * GRID_W), s_ctx], axis=-1)
        p = jax.nn.softmax(s_all, axis=-1).astype(v.dtype)
        p_loc = p[..., :kh * GRID_W].reshape(B, H, GRID_W, kh, GRID_W)
        return (jnp.einsum('bhqik,bikhd->bqhd', p_loc, v_b)
                + jnp.einsum('bhqc,bchd->bqhd', p[..., kh * GRID_W:], vc))

    o = lax.map(row_block, jnp.arange(rows))
    y = jnp.transpose(o, (1, 0, 2, 3, 4)).reshape(B, L, H * dh)
    if not ctx_out:
        return y, None
    s = jnp.einsum('bqhd,bchd->bhqc', qc, kc, preferred_element_type=f32) * scale
    p = jax.nn.softmax(s, axis=-1).astype(vc.dtype)
    y_c = jnp.einsum('bhqc,bchd->bqhd', p, vc).reshape(B, qc.shape[1], H * dh)
    return y, y_c


def _rglru_gates(u, wa, ba, wx, bx, lam):
    B, L, D = u.shape
    ub = u.reshape(B, L, NB_B, BS_B)
    gr = jnp.einsum('blnd,nde->blne', ub, wa).reshape(B, L, D) + ba
    gi = jnp.einsum('blnd,nde->blne', ub, wx).reshape(B, L, D) + bx
    r = jax.nn.sigmoid(gr.astype(jnp.float32))
    i = jax.nn.sigmoid(gi.astype(jnp.float32))
    log_a = -RG_C * r * jax.nn.softplus(-lam.astype(jnp.float32))
    a = jnp.exp(log_a)
    b = jnp.sqrt(-jnp.expm1(2.0 * log_a)) * (i * u.astype(jnp.float32))
    return a, b


def _linear_scan(a, b, h0, reverse):
    def comb(lhs, rhs):
        return lhs[0] * rhs[0], rhs[0] * lhs[1] + rhs[1]
    a_cum, b_cum = lax.associative_scan(comb, (a, b), axis=1, reverse=reverse)
    return a_cum * h0[:, None, :] + b_cum


def _rglru_mix(u, ug, uc, ugc, lp, ctx_out):
    u = _dwconv(u, lp['conv_w'], lp['conv_b'])
    uc = _dwconv(uc, lp['conv_w'], lp['conv_b'])
    h_lat = 0.0
    h_ctx = 0.0
    for d, rev in enumerate((False, True)):
        gates = (lp['rg_wa'][d], lp['rg_ba'][d], lp['rg_wx'][d], lp['rg_bx'][d], lp['rg_lam'][d])
        a_c, b_c = _rglru_gates(uc, *gates)
        hc = _linear_scan(a_c, b_c, jnp.zeros_like(b_c[:, 0]), rev)
        h_last = hc[:, 0] if rev else hc[:, -1]
        a_l, b_l = _rglru_gates(u, *gates)
        h_lat = h_lat + _linear_scan(a_l, b_l, h_last, rev)
        if ctx_out:
            h_ctx = h_ctx + hc
    y = h_lat.astype(u.dtype) * jax.nn.gelu(ug)
    y_c = h_ctx.astype(uc.dtype) * jax.nn.gelu(ugc) if ctx_out else None
    return y, y_c


def _rwkv7_scan(S0, r, w, k, v, kk, a, reverse):
    def step(S, inp):
        r_t, w_t, k_t, v_t, kk_t, a_t = inp
        sa = jnp.einsum('bhvk,bhk->bhv', S, -kk_t)
        S = (S * w_t[:, :, None, :] + sa[..., None] * (kk_t * a_t)[:, :, None, :]
             + v_t[..., None] * k_t[:, :, None, :])
        return S, jnp.einsum('bhvk,bhk->bhv', S, r_t)
    xs = tuple(jnp.moveaxis(t, 1, 0) for t in (r, w, k, v, kk, a))
    S, y = lax.scan(step, S0, xs, reverse=reverse)
    return S, jnp.moveaxis(y, 0, 1)


def _rwkv7_seq(hh, r, k, v, v_first, lp, vres, states, need_out):
    B, L, _ = hh.shape
    f32 = jnp.float32
    xx = _neighbour_mean(hh) - hh
    mu = lp['rw_mu_h']
    xw = hh + xx * mu[0]
    xa = hh + xx * mu[1]
    xg = hh + xx * mu[2]
    if vres is None:
        v_first = v
    else:
        v0, v1, v2, mu_v = vres
        v = v + (v_first - v) * jax.nn.sigmoid(v0 + ((hh + xx * mu_v) @ v1) @ v2)

    def heads(t):
        return t.reshape(B, L, H_C, N_C)

    kk = heads((k * lp['rw_k_k']).astype(f32))
    kk = kk / jnp.maximum(jnp.sqrt(jnp.sum(kk * kk, axis=-1, keepdims=True)), 1e-12)
    rh = heads(r.astype(f32))
    vh = heads(v.astype(f32))
    y = jnp.zeros((B, L, H_C, N_C), f32)
    bonus = jnp.zeros((B, L, H_C, N_C), f32)
    new_states = []
    for d, rev in enumerate((False, True)):
        z = (lp['rw_w0'][d] + jnp.tanh(xw @ lp['rw_w1'][d]) @ lp['rw_w2'][d]).astype(f32)
        decay = jnp.exp(-jnp.exp(-jax.nn.softplus(-z) - 0.5))
        a = jax.nn.sigmoid((lp['rw_a0'][d] + (xa @ lp['rw_a1'][d]) @ lp['rw_a2'][d]).astype(f32))
        kd = heads(k.astype(f32) * (1.0 + (a - 1.0) * lp['rw_k_a']))
        S, yd = _rwkv7_scan(states[d], rh, heads(decay), kd, vh, kk, heads(a), rev)
        new_states.append(S)
        if need_out:
            y = y + yd
            bonus = bonus + jnp.sum(rh * kd * lp['rw_r_k'], axis=-1, keepdims=True) * vh
    if not need_out:
        return None, (new_states[0], new_states[1]), v_first
    mean = jnp.mean(y, axis=-1, keepdims=True)
    var = jnp.mean(jnp.square(y - mean), axis=-1, keepdims=True)
    yn = ((y - mean) * lax.rsqrt(var + LNX_EPS)).reshape(B, L, D_C) * lp['rw_ln_w'] + lp['rw_ln_b']
    g = jax.nn.sigmoid(xg @ lp['rw_g1']) @ lp['rw_g2']
    out = (yn + bonus.reshape(B, L, D_C)) * g
    return out.astype(hh.dtype), (new_states[0], new_states[1]), v_first


def _hybrid_mixer(h, hc, lp, vres, v_first, v_first_c, ctx_out):
    B = h.shape[0]
    qa, ka, va, ub, ug, rr, kr, vr, m_a, m_b, m_c = _split_in(h @ lp['w_in'])
    qac, kac, vac, ubc, ugc, rrc, krc, vrc, m_ac, m_bc, m_cc = _split_in(hc @ lp['w_in'])

    def heads_a(t):
        return t.reshape(t.shape[0], t.shape[1], H_A, DH_A)

    y_nat, y_nat_c = _natten_mix(heads_a(qa), heads_a(ka), heads_a(va),
                                 heads_a(qac), heads_a(kac), heads_a(vac), lp['rpb'], ctx_out)
    y_lru, y_lru_c = _rglru_mix(ub, ug, ubc, ugc, lp, ctx_out)

    mu = lp['rw_mu_rkv']

    def shift3(r_, k_, v_):
        return tuple(t + mu[j] * (_neighbour_mean(t) - t) for j, t in enumerate((r_, k_, v_)))

    zero = jnp.zeros((B, H_C, N_C, N_C), jnp.float32)
    r_c, k_c, v_c = shift3(rrc, krc, vrc)
    y_rw_c, states_c, v_first_c = _rwkv7_seq(hc, r_c, k_c, v_c, v_first_c, lp, vres, (zero, zero), ctx_out)
    r_l, k_l, v_l = shift3(rr, kr, vr)
    y_rw, _, v_first = _rwkv7_seq(h, r_l, k_l, v_l, v_first, lp, vres, states_c, True)

    def merge(ya, yb, yc, ga, gb, gc):
        mixed = (jax.nn.sigmoid(ga) * (ya @ lp['w_br_a'])
                 + jax.nn.sigmoid(gb) * (yb @ lp['w_br_b'])
                 + jax.nn.sigmoid(gc) * (yc @ lp['w_br_c']))
        return mixed @ lp['w_out']

    y = merge(y_nat, y_lru, y_rw, m_a, m_b, m_c)
    y_c = merge(y_nat_c, y_lru_c, y_rw_c, m_ac, m_bc, m_cc) if ctx_out else None
    return y, y_c, v_first, v_first_c


def _sqrelu_mlp(h, w1, w2):
    return jnp.square(jax.nn.relu(h @ w1)) @ w2


def setup_inputs(seed: int = 0) -> dict:
    key = jax.random.key(seed)
    ks = iter(jax.random.split(key, 64))
    f32 = jnp.float32
    D = D_MODEL

    def nrm(shape, scale):
        return scale * jax.random.normal(next(ks), shape, f32)

    def uni(shape, lo, hi):
        return jax.random.uniform(next(ks), shape, f32, lo, hi)

    inp = {}
    inp['x'] = nrm((BATCH, SEQ, D), 1.0)
    inp['c'] = nrm((BATCH, D), 1.0)
    inp['ctx'] = nrm((BATCH, CTX_LEN, D), 1.0)
    inp['c_ctx'] = nrm((D,), 1.0)
    inp['w_ada'] = nrm((DEPTH, D, 6 * D), 0.5 * D ** -0.5)
    inp['b_ada'] = nrm((DEPTH, 6 * D), 0.02)
    inp['norm1_g'] = 1.0 + nrm((DEPTH, D), 0.02)
    inp['norm2_g'] = 1.0 + nrm((DEPTH, D), 0.02)
    inp['w_in'] = nrm((DEPTH, D, N_IN), D ** -0.5)
    inp['rpb'] = nrm((DEPTH, H_A, 2 * WIN_H - 1, 2 * WIN_W - 1), 0.2)
    inp['conv_w'] = nrm((DEPTH, CONV_W, D_B), CONV_W ** -0.5)
    inp['conv_b'] = nrm((DEPTH, D_B), 0.02)
    inp['rg_wa'] = nrm((DEPTH, 2, NB_B, BS_B, BS_B), BS_B ** -0.5)
    inp['rg_ba'] = nrm((DEPTH, 2, D_B), 0.02)
    inp['rg_wx'] = nrm((DEPTH, 2, NB_B, BS_B, BS_B), BS_B ** -0.5)
    inp['rg_bx'] = nrm((DEPTH, 2, D_B), 0.02)
    a_base = uni((DEPTH, 2, D_B), 0.9 ** (1.0 / RG_C), 0.999 ** (1.0 / RG_C))
    inp['rg_lam'] = jnp.log(a_base) - jnp.log1p(-a_base)
    inp['rw_mu_rkv'] = uni((DEPTH, 3, D_C), 0.0, 1.0)
    inp['rw_mu_h'] = uni((DEPTH, 3, D), 0.0, 1.0)
    inp['rw_w0'] = uni((DEPTH, 2, D_C), -6.0, -1.0)
    inp['rw_w1'] = nrm((DEPTH, 2, D, R_DECAY), D ** -0.5)
    inp['rw_w2'] = nrm((DEPTH, 2, R_DECAY, D_C), 0.5 * R_DECAY ** -0.5)
    inp['rw_a0'] = nrm((DEPTH, 2, D_C), 0.1)
    inp['rw_a1'] = nrm((DEPTH, 2, D, R_ICLR), D ** -0.5)
    inp['rw_a2'] = nrm((DEPTH, 2, R_ICLR, D_C), 0.5 * R_ICLR ** -0.5)
    inp['rw_g1'] = nrm((DEPTH, D, R_GATE), D ** -0.5)
    inp['rw_g2'] = nrm((DEPTH, R_GATE, D_C), R_GATE ** -0.5)
    inp['rw_k_k'] = 0.85 + nrm((DEPTH, D_C), 0.02)
    inp['rw_k_a'] = 1.0 + nrm((DEPTH, D_C), 0.02)
    inp['rw_r_k'] = nrm((DEPTH, H_C, N_C), 0.1)
    inp['rw_ln_w'] = 1.0 + nrm((DEPTH, D_C), 0.02)
    inp['rw_ln_b'] = nrm((DEPTH, D_C), 0.02)
    inp['vres_v0'] = 1.0 + nrm((DEPTH - 1, D_C), 0.1)
    inp['vres_v1'] = nrm((DEPTH - 1, D, R_VRES), D ** -0.5)
    inp['vres_v2'] = nrm((DEPTH - 1, R_VRES, D_C), 0.5 * R_VRES ** -0.5)
    inp['vres_mu'] = uni((DEPTH - 1, D), 0.0, 1.0)
    inp['w_br_a'] = nrm((DEPTH, D_A, D), D_A ** -0.5)
    inp['w_br_b'] = nrm((DEPTH, D_B, D), D_B ** -0.5)
    inp['w_br_c'] = nrm((DEPTH, D_C, D), D_C ** -0.5)
    inp['w_out'] = nrm((DEPTH, D, D), D ** -0.5)
    inp['w_ff1'] = nrm((DEPTH, D, D_FF), D ** -0.5)
    inp['w_ff2'] = nrm((DEPTH, D_FF, D), D_FF ** -0.5)
    inp['final_g'] = 1.0 + nrm((D,), 0.02)
    return inp


def reference(x, c, ctx, c_ctx, w_ada, b_ada, norm1_g, norm2_g, w_in, rpb, conv_w, conv_b,
              rg_wa, rg_ba, rg_wx, rg_bx, rg_lam, rw_mu_rkv, rw_mu_h, rw_w0, rw_w1, rw_w2,
              rw_a0, rw_a1, rw_a2, rw_g1, rw_g2, rw_k_k, rw_k_a, rw_r_k, rw_ln_w, rw_ln_b,
              vres_v0, vres_v1, vres_v2, vres_mu, w_br_a, w_br_b, w_br_c, w_out,
              w_ff1, w_ff2, final_g):
    v_first = None
    v_first_c = None
    for i in range(DEPTH):
        last = i == DEPTH - 1
        lp = {
            'w_in': w_in[i], 'rpb': rpb[i], 'conv_w': conv_w[i], 'conv_b': conv_b[i],
            'rg_wa': rg_wa[i], 'rg_ba': rg_ba[i], 'rg_wx': rg_wx[i], 'rg_bx': rg_bx[i], 'rg_lam': rg_lam[i],
            'rw_mu_rkv': rw_mu_rkv[i], 'rw_mu_h': rw_mu_h[i], 'rw_w0': rw_w0[i], 'rw_w1': rw_w1[i],
            'rw_w2': rw_w2[i], 'rw_a0': rw_a0[i], 'rw_a1': rw_a1[i], 'rw_a2': rw_a2[i],
            'rw_g1': rw_g1[i], 'rw_g2': rw_g2[i], 'rw_k_k': rw_k_k[i], 'rw_k_a': rw_k_a[i],
            'rw_r_k': rw_r_k[i], 'rw_ln_w': rw_ln_w[i], 'rw_ln_b': rw_ln_b[i],
            'w_br_a': w_br_a[i], 'w_br_b': w_br_b[i], 'w_br_c': w_br_c[i], 'w_out': w_out[i],
        }
        vres = None if i == 0 else (vres_v0[i - 1], vres_v1[i - 1], vres_v2[i - 1], vres_mu[i - 1])
        sh1, sc1, gt1, sh2, sc2, gt2 = jnp.split((jax.nn.silu(c) @ w_ada[i] + b_ada[i])[:, None, :], 6, axis=-1)
        csh1, csc1, cgt1, csh2, csc2, cgt2 = jnp.split(jax.nn.silu(c_ctx) @ w_ada[i] + b_ada[i], 6, axis=-1)
        h = _rmsnorm(x, norm1_g[i]) * (1.0 + sc1) + sh1
        hc = _rmsnorm(ctx, norm1_g[i]) * (1.0 + csc1) + csh1
        y, y_c, v_first, v_first_c = _hybrid_mixer(h, hc, lp, vres, v_first, v_first_c, not last)
        x = x + gt1 * y
        x = x + gt2 * _sqrelu_mlp(_rmsnorm(x, norm2_g[i]) * (1.0 + sc2) + sh2, w_ff1[i], w_ff2[i])
        if not last:
            ctx = ctx + cgt1 * y_c
            ctx = ctx + cgt2 * _sqrelu_mlp(_rmsnorm(ctx, norm2_g[i]) * (1.0 + csc2) + csh2, w_ff1[i], w_ff2[i])
    return _rmsnorm(x, final_g)
```

```python
import functools
import math

import jax
import jax.numpy as jnp
import numpy as np
from jax import lax
from jax.experimental import pallas as pl
from jax.experimental.pallas import tpu as pltpu

F32 = jnp.float32
BF16 = jnp.bfloat16

D_MODEL = 1024
GRID_W = 64
N_HEADS = 16
HEAD_DIM = 64
WIN_H = 8
WIN_W = 16
ROPE_BASE = 10000.0
NB_B = 16
BS_B = 64
RG_C = 8.0
LNX_EPS = 64e-5
NORM_EPS = 1e-6
NEG_INF = -1e30
MASKED = -1e30
N_GROUPS = 12
LR_W = 512
CHUNK = 64
VMEM_LIMIT = 56 * 1024 * 1024
PAIR = 2 * HEAD_DIM


def _cparams(sem):
    return pltpu.CompilerParams(dimension_semantics=sem, vmem_limit_bytes=VMEM_LIMIT)


def _pick(n, cands):
    for c in cands:
        if n % c == 0:
            return c
    raise ValueError(f"no tile for {n}")


def _dot(a, b):
    return jnp.dot(a, b, preferred_element_type=F32)


def _dot_nt(a, b):
    return lax.dot_general(a, b, (((1,), (1,)), ((), ())), preferred_element_type=F32)


def _dot_tn(a, b):
    return lax.dot_general(a, b, (((0,), (0,)), ((), ())), preferred_element_type=F32)


def _split(x):
    hi = x.astype(BF16)
    return hi, (x - hi.astype(F32)).astype(BF16)


def _lanes_l(p):
    return jnp.concatenate([p[0], p[1], p[0]], axis=1)


def _lanes_r(p):
    return jnp.concatenate([p[0], p[0], p[1]], axis=1)


def _rows_l(p):
    return jnp.concatenate([p[0], p[1], p[0]], axis=0)


def _rows_r(p):
    return jnp.concatenate([p[0], p[0], p[1]], axis=0)


def _sigmoid(x):
    return 1.0 / (1.0 + jnp.exp(-x))


def _row_select(mod_ref, idx, rows, n_ctx):
    return jnp.where(rows < n_ctx, mod_ref[0, 0, idx:idx + 1, :], mod_ref[0, 1, idx:idx + 1, :])


def _seg_sum64(x):
    n = x.shape[-1]
    lane = lax.broadcasted_iota(jnp.int32, (1, n), 1)
    for dd in (32, 16, 8, 4, 2, 1):
        lower = (lane & dd) == 0
        x = x + jnp.where(lower, pltpu.roll(x, n - dd, 1), pltpu.roll(x, dd, 1))
    return x


def _seq_block(d, i, nt, nctx):
    rev = jnp.where(i < nctx, nctx - 1 - i, nt - 1 - (i - nctx))
    return jnp.where(d == 0, i, rev)


def _ada_kernel(a_ref, w_ref, b_ref, o_ref):
    a = a_ref[...]
    s = a * _sigmoid(a)
    o_ref[...] = _dot(s.astype(BF16), w_ref[...].astype(BF16)) + b_ref[...]


def _ada(cvec, w, b):
    m, k = cvec.shape
    n = w.shape[1]
    tn = _pick(n, (1536, 1024, 512, 128))
    return pl.pallas_call(
        _ada_kernel,
        out_shape=jax.ShapeDtypeStruct((m, n), F32),
        grid=(n // tn,),
        in_specs=[pl.BlockSpec((m, k), lambda j: (0, 0)),
                  pl.BlockSpec((k, tn), lambda j: (0, j)),
                  pl.BlockSpec((1, tn), lambda j: (0, j))],
        out_specs=pl.BlockSpec((m, tn), lambda j: (0, j)),
        compiler_params=_cparams(("arbitrary",)),
        name="ada",
    )(cvec, w, b.reshape(1, n))


def _inproj_kernel(x_ref, g_ref, mod_ref, w_ref, o_ref, hb_ref, *, tm, n_ctx):
    i = pl.program_id(1)
    j = pl.program_id(2)

    @pl.when(j == 0)
    def _():
        x = x_ref[0]
        ms = jnp.mean(x * x, axis=-1, keepdims=True)
        y = x * lax.rsqrt(ms + NORM_EPS) * g_ref[...]
        rows = i * tm + lax.broadcasted_iota(jnp.int32, (tm, 1), 0)
        sh = _row_select(mod_ref, 0, rows, n_ctx)
        sc = _row_select(mod_ref, 1, rows, n_ctx)
        hb_ref[...] = (y * (1.0 + sc) + sh).astype(BF16)

    o_ref[0] = _dot(hb_ref[...], w_ref[...])


def _inproj(x, g, mods, w, n_ctx):
    bsz, t, d = x.shape
    n = w.shape[1]
    tm = _pick(t, (768, 256))
    tn = 1024
    return pl.pallas_call(
        functools.partial(_inproj_kernel, tm=tm, n_ctx=n_ctx),
        out_shape=jax.ShapeDtypeStruct((bsz, t, n), F32),
        grid=(bsz, t // tm, n // tn),
        in_specs=[pl.BlockSpec((1, tm, d), lambda b, i, j: (b, i, 0)),
                  pl.BlockSpec((1, d), lambda b, i, j: (0, 0)),
                  pl.BlockSpec((1, 2, 6, d), lambda b, i, j: (b, 0, 0, 0)),
                  pl.BlockSpec((d, tn), lambda b, i, j: (0, j))],
        out_specs=pl.BlockSpec((1, tm, tn), lambda b, i, j: (b, i, j)),
        scratch_shapes=[pltpu.VMEM((tm, d), BF16)],
        compiler_params=_cparams(("parallel", "parallel", "arbitrary")),
        name="inproj",
    )(x, g.reshape(1, d), mods, w)


def _rope_kernel(q_ref, k_ref, v_ref, c_ref, s1_ref, s2_ref, qo_ref, ko_ref, vo_ref):
    d = q_ref.shape[-1]
    reps = d // c_ref.shape[-1]
    cos = jnp.concatenate([c_ref[...]] * reps, axis=1)
    s1 = jnp.concatenate([s1_ref[...]] * reps, axis=1)
    s2 = jnp.concatenate([s2_ref[...]] * reps, axis=1)
    quarter = HEAD_DIM // 4

    def rope(x):
        return x * cos + pltpu.roll(x, d - quarter, 1) * s1 + pltpu.roll(x, quarter, 1) * s2

    qo_ref[0] = (rope(q_ref[0]) * (HEAD_DIM ** -0.5)).astype(BF16)
    ko_ref[0] = rope(k_ref[0]).astype(BF16)
    vo_ref[0] = v_ref[0].astype(BF16)


def _rope_tables(t, n_ctx):
    nf = HEAD_DIM // 4
    inv_freq = ROPE_BASE ** (-jnp.arange(nf, dtype=F32) / nf)
    tok = jnp.arange(t) - n_ctx
    row = (tok // GRID_W).astype(F32)
    col = (tok % GRID_W).astype(F32)
    lane = np.arange(PAIR) % HEAD_DIM
    use_col = jnp.asarray(lane >= HEAD_DIM // 2)
    first = jnp.asarray((lane % (HEAD_DIM // 2)) < nf)
    pos = jnp.where(use_col[None, :], col[:, None], row[:, None])
    ang = pos * inv_freq[lane % nf][None, :]
    is_lat = (tok >= 0)[:, None]
    cos = jnp.where(is_lat, jnp.cos(ang), 1.0)
    sin = jnp.where(is_lat, jnp.sin(ang), 0.0)
    return cos, jnp.where(first[None, :], -sin, 0.0), jnp.where(first[None, :], 0.0, sin)


def _rope(z, tables):
    bsz, t, _ = z.shape
    d = D_MODEL
    tm = _pick(t, (768, 256))
    zspec = lambda g: pl.BlockSpec((1, tm, d), lambda b, i: (b, i, g))
    tspec = pl.BlockSpec((tm, PAIR), lambda b, i: (i, 0))
    ospec = pl.BlockSpec((1, tm, d), lambda b, i: (b, i, 0))
    oshape = jax.ShapeDtypeStruct((bsz, t, d), BF16)
    return pl.pallas_call(
        _rope_kernel,
        out_shape=(oshape, oshape, oshape),
        grid=(bsz, t // tm),
        in_specs=[zspec(0), zspec(1), zspec(2), tspec, tspec, tspec],
        out_specs=(ospec, ospec, ospec),
        compiler_params=_cparams(("parallel", "parallel")),
        name="rope",
    )(z, z, z, *tables)


def _natten_kernel(q_ref, *refs):
    k_refs = refs[0:WIN_H]
    v_refs = refs[WIN_H:2 * WIN_H]
    kc_ref, vc_ref, tb_ref, o_ref = refs[2 * WIN_H:]
    nq = q_ref.shape[1]
    lane = lax.broadcasted_iota(jnp.int32, (nq, PAIR), 1)
    first = lane < HEAD_DIM
    zero = jnp.zeros((nq, PAIR), BF16)
    for p in range(N_HEADS // 2):
        sl = slice(p * PAIR, (p + 1) * PAIR)
        q2 = q_ref[0, :, sl]
        k2 = jnp.concatenate([r[0, :, sl] for r in k_refs] + [kc_ref[0, :, sl]], axis=0)
        v2 = jnp.concatenate([r[0, :, sl] for r in v_refs] + [vc_ref[0, :, sl]], axis=0)
        outs = []
        for hh in range(2):
            qm = jnp.where(first, q2, zero) if hh == 0 else jnp.where(first, zero, q2)
            s = _dot_nt(qm, k2)
            tb = tb_ref[0, 2 * p + hh]
            s = jnp.where(tb > 0.5 * MASKED, s + tb, NEG_INF)
            m = jnp.max(s, axis=-1, keepdims=True)
            e = jnp.exp(s - m)
            pr = (e / jnp.sum(e, axis=-1, keepdims=True)).astype(BF16)
            outs.append(_dot(pr, v2))
        o_ref[0, :, sl] = jnp.where(first, outs[0], outs[1]).astype(BF16)


def _natten_table(rpb, n_ctx):
    col = np.arange(GRID_W)
    c0 = np.clip(col - WIN_W // 2, 0, GRID_W - WIN_W)
    in_win = (col[None, :] >= c0[:, None]) & (col[None, :] < c0[:, None] + WIN_W)
    dc_idx = np.clip(col[None, :] - col[:, None] + WIN_W - 1, 0, 2 * WIN_W - 2)
    dr = np.arange(WIN_H)[None, :] - np.arange(WIN_H)[:, None] + WIN_H - 1
    b = rpb.astype(F32)[:, dr][:, :, :, dc_idx]
    b = jnp.where(jnp.asarray(in_win)[None, None, None], b, MASKED)
    b = jnp.transpose(b, (1, 0, 3, 2, 4)).reshape(WIN_H, N_HEADS, GRID_W, WIN_H * GRID_W)
    b = jnp.concatenate([b, jnp.full((1,) + b.shape[1:], MASKED, F32)], axis=0)
    return jnp.concatenate([b, jnp.zeros(b.shape[:3] + (n_ctx,), F32)], axis=-1)


def _natten(qr, kr, vb, table, n_ctx):
    bsz, t, d = qr.shape
    nqc = n_ctx // GRID_W
    rows = (t - n_ctx) // GRID_W
    assert rows >= WIN_H
    nkeys = table.shape[-1]

    def r0(s):
        return jnp.clip(s - nqc - WIN_H // 2, 0, rows - WIN_H)

    def variant(s):
        return jnp.where(s < nqc, WIN_H, s - nqc - r0(s))

    blk = (1, GRID_W, d)
    kv_specs = [pl.BlockSpec(blk, functools.partial(lambda b, s, i: (b, nqc + r0(s) + i, 0), i=i))
                for i in range(WIN_H)]
    ctx_spec = pl.BlockSpec((1, n_ctx, d), lambda b, s: (b, 0, 0))
    return pl.pallas_call(
        _natten_kernel,
        out_shape=jax.ShapeDtypeStruct((bsz, t, d), BF16),
        grid=(bsz, t // GRID_W),
        in_specs=[pl.BlockSpec(blk, lambda b, s: (b, s, 0))] + kv_specs + kv_specs + [
            ctx_spec, ctx_spec,
            pl.BlockSpec((1, N_HEADS, GRID_W, nkeys), lambda b, s: (variant(s), 0, 0, 0))],
        out_specs=pl.BlockSpec(blk, lambda b, s: (b, s, 0)),
        compiler_params=_cparams(("parallel", "arbitrary")),
        name="natten",
    )(qr, *([kr] * WIN_H), *([vb] * WIN_H), kr, vb, table)


def _rglru_kernel(u_ref, up_ref, un_ref, cw_ref, cb_ref, wa_ref, ba_ref, wx_ref, bx_ref, lam_ref,
                  o_ref, a_s, b_s, h_s, *, tb, nt, nctx):
    d = pl.program_id(1)
    i = pl.program_id(2)
    blk = _seq_block(d, i, nt, nctx)
    seq_first = (blk == 0) | (blk == nctx)
    seq_last = (blk == nctx - 1) | (blk == nt - 1)
    x = u_ref[0]
    prev = jnp.where(seq_first, 0.0, up_ref[0])
    nxt = jnp.where(seq_last, 0.0, un_ref[0])
    row = lax.broadcasted_iota(jnp.int32, (tb, 1), 0)
    xm1 = jnp.where(row == 0, prev[7:8], pltpu.roll(x, 1, 0))
    xm2 = jnp.where(row == 0, prev[6:7], jnp.where(row == 1, prev[7:8], pltpu.roll(x, 2, 0)))
    xp1 = jnp.where(row == tb - 1, nxt[0:1], pltpu.roll(x, tb - 1, 0))
    cw = cw_ref[...]
    u = cw[0:1] * xm2 + cw[1:2] * xm1 + cw[2:3] * x + cw[3:4] * xp1 + cb_ref[...]
    ub = u.astype(BF16)
    gw = wa_ref.shape[-1]
    ngrp = u.shape[-1] // gw
    gr = jnp.concatenate([_dot(ub[:, g * gw:(g + 1) * gw], wa_ref[0, g]) for g in range(ngrp)], axis=1)
    gi = jnp.concatenate([_dot(ub[:, g * gw:(g + 1) * gw], wx_ref[0, g]) for g in range(ngrp)], axis=1)
    r = _sigmoid(gr + ba_ref[0])
    ig = _sigmoid(gi + bx_ref[0])
    nl = -lam_ref[0]
    softplus = jnp.maximum(nl, 0.0) + jnp.log(1.0 + jnp.exp(-jnp.abs(nl)))
    log_a = -RG_C * r * softplus
    a_s[...] = jnp.exp(log_a)
    b_s[...] = jnp.sqrt(1.0 - jnp.exp(2.0 * log_a)) * (ig * u)

    @pl.when(i == 0)
    def _():
        h_s[...] = jnp.zeros_like(h_s)

    def step(t, h):
        tt = jnp.where(d == 0, t, tb - 1 - t)
        h = a_s[pl.ds(tt, 1), :] * h + b_s[pl.ds(tt, 1), :]
        o_ref[0, 0, pl.ds(tt, 1), :] = h
        return h

    h_s[...] = lax.fori_loop(0, tb, step, h_s[...], unroll=8)


def _rglru(z, conv_w, conv_b, wa, ba, wx, bx, lam, n_ctx):
    bsz, t, _ = z.shape
    d = D_MODEL
    tb = 256
    nt = t // tb
    nctx = n_ctx // tb
    hb = tb // 8
    gw = wa.shape[-1]

    def blk_of(dd, i):
        return _seq_block(dd, i, nt, nctx)

    vec = lambda: pl.BlockSpec((1, 1, d), lambda b, dd, i: (dd, 0, 0))
    wspec = lambda: pl.BlockSpec((1, d // gw, gw, gw), lambda b, dd, i: (dd, 0, 0, 0))
    return pl.pallas_call(
        functools.partial(_rglru_kernel, tb=tb, nt=nt, nctx=nctx),
        out_shape=jax.ShapeDtypeStruct((2, bsz, t, d), F32),
        grid=(bsz, 2, nt),
        in_specs=[pl.BlockSpec((1, tb, d), lambda b, dd, i: (b, blk_of(dd, i), 3)),
                  pl.BlockSpec((1, 8, d), lambda b, dd, i: (b, jnp.maximum(blk_of(dd, i) * hb - 1, 0), 3)),
                  pl.BlockSpec((1, 8, d), lambda b, dd, i: (b, jnp.minimum((blk_of(dd, i) + 1) * hb, t // 8 - 1), 3)),
                  pl.BlockSpec((4, d), lambda b, dd, i: (0, 0)),
                  pl.BlockSpec((1, d), lambda b, dd, i: (0, 0)),
                  wspec(), vec(), wspec(), vec(), vec()],
        out_specs=pl.BlockSpec((1, 1, tb, d), lambda b, dd, i: (dd, b, blk_of(dd, i), 0)),
        scratch_shapes=[pltpu.VMEM((tb, d), F32), pltpu.VMEM((tb, d), F32), pltpu.VMEM((1, d), F32)],
        compiler_params=_cparams(("parallel", "arbitrary", "arbitrary")),
        name="rglru",
    )(z, z, z, conv_w, conv_b.reshape(1, d), wa, ba.reshape(2, 1, d), wx, bx.reshape(2, 1, d),
      lam.reshape(2, 1, d))


def _block_diag_groups(w, per):
    two, nb, bs, _ = w.shape
    w = w.reshape(two, nb // per, per, bs, bs)
    eye = jnp.eye(per, dtype=w.dtype)
    return jnp.einsum('dgaij,ab->dgaibj', w, eye).reshape(two, nb // per, per * bs, per * bs)


def _rwkv_prep_kernel(*refs, tb, nt, nctx, has_vres):
    (zr, zr_p, zr_n, zk, zk_p, zk_n, zv, zv_p, zv_n, zl, zl_p, zl_n) = refs[:12]
    refs = refs[12:]
    if has_vres:
        vf_ref, refs = refs[0], refs[1:]
    (mu_ref, kk_ref, ka_ref, rk_ref, w0_ref, a0_ref, v0_ref, w2_ref,
     r_o, v_o, kk_o, g_o, bonus_o, lw_o, kd_o, bb_o) = refs
    blk = pl.program_id(1)
    seq_first = (blk == 0) | (blk == nctx)
    seq_last = (blk == nctx - 1) | (blk == nt - 1)
    row = lax.broadcasted_iota(jnp.int32, (tb, 1), 0)

    def nbr_mean(main, prev_row, next_row):
        prev_row = jnp.where(seq_first, 0.0, prev_row)
        next_row = jnp.where(seq_last, 0.0, next_row)
        up = jnp.where(row == 0, prev_row, pltpu.roll(main, 1, 0))
        dn = jnp.where(row == tb - 1, next_row, pltpu.roll(main, tb - 1, 0))
        return 0.5 * (up + dn)

    def shifted(m_ref, p_ref, n_ref, mu):
        main = m_ref[0]
        return main + mu * (nbr_mean(main, p_ref[0, 7:8, :], n_ref[0, 0:1, :]) - main)

    mu = mu_ref[...]
    r = shifted(zr, zr_p, zr_n, mu[0:1])
    k = shifted(zk, zk_p, zk_n, mu[1:2])
    v = shifted(zv, zv_p, zv_n, mu[2:3])

    lr = zl[0, :, 0:LR_W] + nbr_mean(zl[0, :, LR_W:2 * LR_W], zl_p[0, 7:8, LR_W:2 * LR_W],
                                     zl_n[0, 0:1, LR_W:2 * LR_W])
    col = lax.broadcasted_iota(jnp.int32, (1, LR_W), 1)
    act = jnp.where(col < 128, jnp.tanh(lr), jnp.where((col >= 256) & (col < 416), _sigmoid(lr), lr))
    actb = act.astype(BF16)
    s0 = actb[:, 0:256]
    s1 = actb[:, 256:512]
    g_o[0] = _dot(s1, w2_ref[4])
    if has_vres:
        v = v + (vf_ref[0] - v) * _sigmoid(v0_ref[...] + _dot(s1, w2_ref[5]))
    r_o[0] = r
    v_o[0] = v
    kkr = k * kk_ref[...]
    kk = kkr / jnp.maximum(jnp.sqrt(_seg_sum64(kkr * kkr)), 1e-12)
    kk_o[0] = kk
    ksum = jnp.zeros_like(k)
    for dd in range(2):
        zw = w0_ref[dd:dd + 1, :] + _dot(s0, w2_ref[dd])
        lw_o[dd, 0] = -math.exp(-0.5) * _sigmoid(zw)
        a = _sigmoid(a0_ref[dd:dd + 1, :] + _dot(s0, w2_ref[2 + dd]))
        kd = k * (1.0 + (a - 1.0) * ka_ref[...])
        kd_o[dd, 0] = kd
        bb_o[dd, 0] = kk * a
        ksum = ksum + kd
    bonus_o[0] = _seg_sum64(r * ksum * rk_ref[...]) * v


def _rwkv_prep(z, vfirst, mu_rkv, k_k, k_a, r_k, w0, a0, v0, w2cat, n_ctx):
    bsz, t, _ = z.shape
    d = D_MODEL
    tb = 256
    nt = t // tb
    nctx = n_ctx // tb
    hb = tb // 8
    has_vres = vfirst is not None

    def triple(g):
        return [pl.BlockSpec((1, tb, d), lambda b, i: (b, i, g)),
                pl.BlockSpec((1, 8, d), lambda b, i: (b, jnp.maximum(i * hb - 1, 0), g)),
                pl.BlockSpec((1, 8, d), lambda b, i: (b, jnp.minimum((i + 1) * hb, t // 8 - 1), g))]

    tok = pl.BlockSpec((1, tb, d), lambda b, i: (b, i, 0))
    tok2 = pl.BlockSpec((2, 1, tb, d), lambda b, i: (0, b, i, 0))
    const = lambda shape: pl.BlockSpec(shape, lambda b, i: (0,) * len(shape))
    in_specs = triple(5) + triple(6) + triple(7) + triple(11)
    args = [z] * 12
    if has_vres:
        in_specs.append(tok)
        args.append(vfirst)
    in_specs += [const((3, d)), const((1, d)), const((1, d)), const((1, d)), const((2, d)), const((2, d)),
                 const((1, d)), const(w2cat.shape)]
    args += [mu_rkv, k_k.reshape(1, d), k_a.reshape(1, d), r_k.reshape(1, d), w0, a0, v0.reshape(1, d), w2cat]
    one = jax.ShapeDtypeStruct((bsz, t, d), F32)
    two = jax.ShapeDtypeStruct((2, bsz, t, d), F32)
    return pl.pallas_call(
        functools.partial(_rwkv_prep_kernel, tb=tb, nt=nt, nctx=nctx, has_vres=has_vres),
        out_shape=(one, one, one, one, one, two, two, two),
        grid=(bsz, nt),
        in_specs=in_specs,
        out_specs=(tok, tok, tok, tok, tok, tok2, tok2, tok2),
        compiler_params=_cparams(("parallel", "parallel")),
        name="rwkv_prep",
    )(*args)


def _rwkv_scan_kernel(r_ref, v_ref, kk_ref, lw_ref, kd_ref, bb_ref, y_ref, s_ref):
    d = pl.program_id(1)
    i = pl.program_id(2)
    n = r_ref.shape[1]
    rev = d == 1

    @pl.when(i == 0)
    def _():
        s_ref[...] = jnp.zeros_like(s_ref)

    ti = lax.broadcasted_iota(jnp.int32, (n, n), 0)
    tj = lax.broadcasted_iota(jnp.int32, (n, n), 1)
    order = jnp.where(rev, tj - ti, ti - tj)
    tri = jnp.where(order >= 0, 1.0, 0.0).astype(BF16)
    lw = lw_ref[0, 0]
    h1 = lw.astype(BF16)
    r1 = lw - h1.astype(F32)
    h2 = r1.astype(BF16)
    h3 = (r1 - h2.astype(F32)).astype(BF16)
    c = _dot(tri, h1) + _dot(tri, h2) + _dot(tri, h3)
    c_last = jnp.where(rev, c[0:1], c[n - 1:n])
    r = r_ref[0]
    v = v_ref[0]
    kk = kk_ref[0]
    kd = kd_ref[0, 0]
    bb = bb_ref[0, 0]
    enc = jnp.exp(-c)
    el = jnp.exp(c_last - c)
    rh = r * jnp.exp(c)
    ah = -kk * jnp.exp(c - lw)
    bh = bb * enc
    kh = kd * enc
    bl = bb * el
    kl = kd * el
    gl = jnp.exp(c_last)

    lane = lax.broadcasted_iota(jnp.int32, (n, PAIR), 1)
    first = lane < HEAD_DIM
    si = lax.broadcasted_iota(jnp.int32, (2 * n, 2 * n), 0) & (n - 1)
    sj = lax.broadcasted_iota(jnp.int32, (2 * n, 2 * n), 1) & (n - 1)
    row2 = lax.broadcasted_iota(jnp.int32, (2 * n, 2 * n), 0)
    col2 = lax.broadcasted_iota(jnp.int32, (2 * n, 2 * n), 1)
    qi = jnp.where(rev, n - 1 - si, si)
    qj = jnp.where(rev, n - 1 - sj, sj)
    strict = qi > qj
    incl = qi >= qj
    eye = jnp.where(row2 == col2, 1.0, 0.0)
    levels = []
    blk = 1
    while blk < n:
        bi = qi // blk
        bj = qj // blk
        levels.append((bi // 2 == bj // 2) & (bi % 2 == 1) & (bj % 2 == 0))
        blk *= 2

    def stack(x):
        return jnp.concatenate([jnp.where(first, x, 0.0), jnp.where(first, 0.0, x)], axis=0)

    for p in range(N_HEADS // 2):
        sl = slice(p * PAIR, (p + 1) * PAIR)
        am, rm, bm, km, vm, blm, klm = (_split(stack(t[:, sl])) for t in (ah, rh, bh, kh, v, bl, kl))
        s2 = s_ref[p]
        s2p = _split(s2)
        aa = _dot_nt(jnp.concatenate([_lanes_l(am), _lanes_l(rm)], axis=0),
                     jnp.concatenate([_lanes_r(bm), _lanes_r(km)], axis=0))
        a_ab = jnp.where(strict, aa[:2 * n, :2 * n], 0.0)
        a_ak = jnp.where(strict, aa[:2 * n, 2 * n:], 0.0)
        a_rb = jnp.where(incl, aa[2 * n:, :2 * n], 0.0)
        a_rk = jnp.where(incl, aa[2 * n:, 2 * n:], 0.0)
        x = _dot_nt(_lanes_l(am), _lanes_r(s2p)) + _dot(_lanes_l(_split(a_ak)), _rows_r(vm))
        tinv = eye + jnp.where(levels[0], a_ab, 0.0)
        for m in levels[1:]:
            tp = _split(tinv)
            half = _dot(_lanes_l(tp), _rows_r(_split(jnp.where(m, a_ab, 0.0))))
            tinv = tinv + _dot(_lanes_l(_split(half)), _rows_r(tp))
        u = _dot(_lanes_l(_split(tinv)), _rows_r(_split(x)))
        up = _split(u)
        o = (_dot_nt(_lanes_l(rm), _lanes_r(s2p))
             + _dot(jnp.concatenate([_lanes_l(_split(a_rb)), _lanes_l(_split(a_rk))], axis=1),
                    jnp.concatenate([_rows_r(up), _rows_r(vm)], axis=0)))
        y_ref[0, 0, :, sl] = o[:n] + o[n:]
        s_ref[p] = s2 * gl[:, sl] + _dot_tn(jnp.concatenate([_rows_l(up), _rows_l(vm)], axis=0),
                                            jnp.concatenate([_rows_r(blm), _rows_r(klm)], axis=0))


def _rwkv_scan(r, v, kk, lw, kd, bb, n_ctx):
    bsz, t, d = r.shape
    nc = t // CHUNK
    nctx = n_ctx // CHUNK

    def blk_of(dd, i):
        return _seq_block(dd, i, nc, nctx)

    tok = pl.BlockSpec((1, CHUNK, d), lambda b, dd, i: (b, blk_of(dd, i), 0))
    tok2 = pl.BlockSpec((1, 1, CHUNK, d), lambda b, dd, i: (dd, b, blk_of(dd, i), 0))
    return pl.pallas_call(
        _rwkv_scan_kernel,
        out_shape=jax.ShapeDtypeStruct((2, bsz, t, d), F32),
        grid=(bsz, 2, nc),
        in_specs=[tok, tok, tok, tok2, tok2, tok2],
        out_specs=tok2,
        scratch_shapes=[pltpu.VMEM((N_HEADS // 2, PAIR, PAIR), F32)],
        compiler_params=_cparams(("parallel", "arbitrary", "arbitrary")),
        name="rwkv_scan",
    )(r, v, kk, lw, kd, bb)


def _gelu_tanh(x):
    return 0.5 * x * (1.0 + jnp.tanh(math.sqrt(2.0 / math.pi) * (x + 0.044715 * (x * x * x))))


def _merge_kernel(x_ref, ya_ref, hl_ref, ug_ref, yw_ref, bonus_ref, g_ref, ma_ref, mb_ref, mc_ref,
                  mod_ref, lnw_ref, lnb_ref, wa_ref, wb_ref, wc_ref, wo_ref, o_ref, *, tm, n_ctx):
    i = pl.program_id(1)
    yb = ((hl_ref[0, 0] + hl_ref[1, 0]) * _gelu_tanh(ug_ref[0])).astype(BF16)
    y = yw_ref[0, 0] + yw_ref[1, 0]
    inv = 1.0 / HEAD_DIM
    mean = _seg_sum64(y) * inv
    yc = y - mean
    var = _seg_sum64(yc * yc) * inv
    yn = yc * lax.rsqrt(var + LNX_EPS) * lnw_ref[...] + lnb_ref[...]
    oc = ((yn + bonus_ref[0]) * g_ref[0]).astype(BF16)
    mixed = (_sigmoid(ma_ref[0]) * _dot(ya_ref[0], wa_ref[...])
             + _sigmoid(mb_ref[0]) * _dot(yb, wb_ref[...])
             + _sigmoid(mc_ref[0]) * _dot(oc, wc_ref[...]))
    yo = _dot(mixed.astype(BF16), wo_ref[...])
    rows = i * tm + lax.broadcasted_iota(jnp.int32, (tm, 1), 0)
    o_ref[0] = x_ref[0] + _row_select(mod_ref, 2, rows, n_ctx) * yo


def _merge(x, ynat, hlru, z, yrw, bonus, g, mods, ln_w, ln_b, wa, wb, wc, wo, n_ctx):
    bsz, t, d = x.shape
    tm = 256
    tok = pl.BlockSpec((1, tm, d), lambda b, i: (b, i, 0))
    tok2 = pl.BlockSpec((2, 1, tm, d), lambda b, i: (0, b, i, 0))
    zspec = lambda gidx: pl.BlockSpec((1, tm, d), lambda b, i: (b, i, gidx))
    vec = pl.BlockSpec((1, d), lambda b, i: (0, 0))
    wspec = pl.BlockSpec((d, d), lambda b, i: (0, 0))
    return pl.pallas_call(
        functools.partial(_merge_kernel, tm=tm, n_ctx=n_ctx),
        out_shape=jax.ShapeDtypeStruct((bsz, t, d), F32),
        grid=(bsz, t // tm),
        in_specs=[tok, tok, tok2, zspec(4), tok2, tok, tok, zspec(8), zspec(9), zspec(10),
                  pl.BlockSpec((1, 2, 6, d), lambda b, i: (b, 0, 0, 0)), vec, vec,
                  wspec, wspec, wspec, wspec],
        out_specs=tok,
        compiler_params=_cparams(("parallel", "parallel")),
        name="merge",
    )(x, ynat, hlru, z, yrw, bonus, g, z, z, z, mods, ln_w.reshape(1, d), ln_b.reshape(1, d), wa, wb, wc, wo)


def _mlp_kernel(x_ref, g_ref, mod_ref, w1_ref, w2_ref, fg_ref, o_ref, hb_ref, acc_ref, *, tm, n_ctx, final):
    i = pl.program_id(1)
    k = pl.program_id(2)
    rows = i * tm + lax.broadcasted_iota(jnp.int32, (tm, 1), 0)

    @pl.when(k == 0)
    def _():
        x = x_ref[0]
        ms = jnp.mean(x * x, axis=-1, keepdims=True)
        y = x * lax.rsqrt(ms + NORM_EPS) * g_ref[...]
        sh = _row_select(mod_ref, 3, rows, n_ctx)
        sc = _row_select(mod_ref, 4, rows, n_ctx)
        hb_ref[...] = (y * (1.0 + sc) + sh).astype(BF16)
        acc_ref[...] = jnp.zeros_like(acc_ref)

    a = jnp.maximum(_dot(hb_ref[...], w1_ref[...]), 0.0)
    acc_ref[...] += _dot((a * a).astype(BF16), w2_ref[...])

    @pl.when(k == pl.num_programs(2) - 1)
    def _():
        xn = x_ref[0] + _row_select(mod_ref, 5, rows, n_ctx) * acc_ref[...]
        if final:
            ms = jnp.mean(xn * xn, axis=-1, keepdims=True)
            xn = xn * lax.rsqrt(ms + NORM_EPS) * fg_ref[...]
        o_ref[0] = xn


def _mlp(x, g, mods, w1, w2, final_g, n_ctx, final):
    bsz, t, d = x.shape
    dff = w1.shape[1]
    tm = _pick(t, (768, 256))
    tk = 1024
    vec = pl.BlockSpec((1, d), lambda b, i, k: (0, 0))
    tok = pl.BlockSpec((1, tm, d), lambda b, i, k: (b, i, 0))
    return pl.pallas_call(
        functools.partial(_mlp_kernel, tm=tm, n_ctx=n_ctx, final=final),
        out_shape=jax.ShapeDtypeStruct((bsz, t, d), F32),
        grid=(bsz, t // tm, dff // tk),
        in_specs=[tok, vec, pl.BlockSpec((1, 2, 6, d), lambda b, i, k: (b, 0, 0, 0)),
                  pl.BlockSpec((d, tk), lambda b, i, k: (0, k)),
                  pl.BlockSpec((tk, d), lambda b, i, k: (k, 0)), vec],
        out_specs=tok,
        scratch_shapes=[pltpu.VMEM((tm, d), BF16), pltpu.VMEM((tm, d), F32)],
        compiler_params=_cparams(("parallel", "parallel", "arbitrary")),
        name="mlp",
    )(x, g.reshape(1, d), mods, w1, w2, final_g.reshape(1, d))


def _lowrank_weights(mu_h, w1, a1, g1, v1, mu_v):
    d = mu_h.shape[-1]
    blocks = [(w1[0], mu_h[0]), (w1[1], mu_h[0]), (a1[0], mu_h[1]), (a1[1], mu_h[1]), (g1, mu_h[2])]
    if v1 is not None:
        blocks.append((v1, mu_v))
    p = jnp.concatenate([w * (1.0 - m)[:, None] for w, m in blocks], axis=1)
    q = jnp.concatenate([w * m[:, None] for w, m in blocks], axis=1)
    pad = lambda a: jnp.pad(a, ((0, 0), (0, LR_W - a.shape[1])))
    return jnp.concatenate([pad(p), pad(q)], axis=1)


def _second_stage_weights(w2, a2, g2, v2):
    d = w2.shape[-1]
    def place(w, off):
        return jnp.zeros((256, d), F32).at[off:off + w.shape[0]].set(w)
    v2p = place(v2, 160) if v2 is not None else jnp.zeros((256, d), F32)
    return jnp.stack([place(w2[0], 0), place(w2[1], 64), place(a2[0], 128), place(a2[1], 192),
                      place(g2, 0), v2p]).astype(BF16)


def kernel(x, c, ctx, c_ctx, w_ada, b_ada, norm1_g, norm2_g, w_in, rpb, conv_w, conv_b, rg_wa, rg_ba, rg_wx, rg_bx, rg_lam, rw_mu_rkv, rw_mu_h, rw_w0, rw_w1, rw_w2, rw_a0, rw_a1, rw_a2, rw_g1, rw_g2, rw_k_k, rw_k_a, rw_r_k, rw_ln_w, rw_ln_b, vres_v0, vres_v1, vres_v2, vres_mu, w_br_a, w_br_b, w_br_c, w_out, w_ff1, w_ff2, final_g):
    bsz, seq, d = x.shape
    n_ctx = ctx.shape[1]
    depth = w_in.shape[0]
    t = n_ctx + seq
    xs = jnp.concatenate([ctx, x], axis=1)
    cvec = jnp.zeros((8, d), F32).at[:bsz].set(c).at[bsz].set(c_ctx)
    tables = _rope_tables(t, n_ctx)
    vfirst = None
    for i in range(depth):
        last = i == depth - 1
        mod = _ada(cvec, w_ada[i], b_ada[i])
        mods = jnp.stack([jnp.broadcast_to(mod[bsz].reshape(1, 6, d), (bsz, 6, d)),
                          mod[:bsz].reshape(bsz, 6, d)], axis=1)
        if i == 0:
            lr_w = _lowrank_weights(rw_mu_h[i], rw_w1[i], rw_a1[i], rw_g1[i], None, None)
            w2cat = _second_stage_weights(rw_w2[i], rw_a2[i], rw_g2[i], None)
            v0 = jnp.zeros((d,), F32)
        else:
            lr_w = _lowrank_weights(rw_mu_h[i], rw_w1[i], rw_a1[i], rw_g1[i], vres_v1[i - 1], vres_mu[i - 1])
            w2cat = _second_stage_weights(rw_w2[i], rw_a2[i], rw_g2[i], vres_v2[i - 1])
            v0 = vres_v0[i - 1]
        w_all = jnp.concatenate([w_in[i], lr_w], axis=1).astype(BF16)
        z = _inproj(xs, norm1_g[i], mods, w_all, n_ctx)
        qr, kr, vb = _rope(z, tables)
        ynat = _natten(qr, kr, vb, _natten_table(rpb[i], n_ctx), n_ctx)
        hlru = _rglru(z, conv_w[i], conv_b[i],
                      _block_diag_groups(rg_wa[i], 4).astype(BF16), rg_ba[i],
                      _block_diag_groups(rg_wx[i], 4).astype(BF16), rg_bx[i], rg_lam[i], n_ctx)
        r, v, kk, g, bonus, lw, kd, bb = _rwkv_prep(
            z, vfirst, rw_mu_rkv[i], rw_k_k[i], rw_k_a[i], rw_r_k[i].reshape(-1), rw_w0[i], rw_a0[i], v0,
            w2cat, n_ctx)
        if i == 0:
            vfirst = v
        yrw = _rwkv_scan(r, v, kk, lw, kd, bb, n_ctx)
        xs = _merge(xs, ynat, hlru, z, yrw, bonus, g, mods, rw_ln_w[i], rw_ln_b[i],
                    w_br_a[i].astype(BF16), w_br_b[i].astype(BF16), w_br_c[i].astype(BF16),
                    w_out[i].astype(BF16), n_ctx)
        xs = _mlp(xs, norm2_g[i], mods, w_ff1[i].astype(BF16), w_ff2[i].astype(BF16), final_g, n_ctx, last)
    return xs[:, n_ctx:]
```

```python
import functools
import math

import jax
import jax.numpy as jnp
import numpy as np
from jax import lax
from jax.experimental import pallas as pl
from jax.experimental.pallas import tpu as pltpu

F32 = jnp.float32
BF16 = jnp.bfloat16

D_MODEL = 1024
GRID_W = 64
N_HEADS = 16
HEAD_DIM = 64
WIN_H = 8
WIN_W = 16
ROPE_BASE = 10000.0
NB_B = 16
BS_B = 64
RG_C = 8.0
LNX_EPS = 64e-5
NORM_EPS = 1e-6
NEG_INF = -1e30
MASKED = -1e30
N_GROUPS = 12
LR_W = 512
CHUNK = 64
VMEM_LIMIT = 56 * 1024 * 1024
PAIR = 2 * HEAD_DIM
SCAN_GROUP = 8


def _cparams(sem):
    return pltpu.CompilerParams(dimension_semantics=sem, vmem_limit_bytes=VMEM_LIMIT)


def _pick(n, cands):
    for c in cands:
        if n % c == 0:
            return c
    raise ValueError(f"no tile for {n}")


def _dot(a, b):
    return jnp.dot(a, b, preferred_element_type=F32)


def _dot_nt(a, b):
    return lax.dot_general(a, b, (((1,), (1,)), ((), ())), preferred_element_type=F32)


def _dot_tn(a, b):
    return lax.dot_general(a, b, (((0,), (0,)), ((), ())), preferred_element_type=F32)


def _sigmoid(x):
    return 1.0 / (1.0 + jnp.exp(-x))


def _row_select(mod_ref, idx, rows, n_ctx):
    return jnp.where(rows < n_ctx, mod_ref[0, 0, idx:idx + 1, :], mod_ref[0, 1, idx:idx + 1, :])


def _seg_sum64(x):
    n = x.shape[-1]
    lane = lax.broadcasted_iota(jnp.int32, (1, n), 1)
    for dd in (32, 16, 8, 4, 2, 1):
        lower = (lane & dd) == 0
        x = x + jnp.where(lower, pltpu.roll(x, n - dd, 1), pltpu.roll(x, dd, 1))
    return x


def _seq_block(d, i, nt, nctx):
    rev = jnp.where(i < nctx, nctx - 1 - i, nt - 1 - (i - nctx))
    return jnp.where(d == 0, i, rev)


def _ada_kernel(a_ref, w_ref, b_ref, o_ref):
    a = a_ref[...]
    s = a * _sigmoid(a)
    o_ref[...] = _dot(s.astype(BF16), w_ref[...].astype(BF16)) + b_ref[...]


def _ada(cvec, w, b):
    m, k = cvec.shape
    n = w.shape[1]
    tn = _pick(n, (1536, 1024, 512, 128))
    return pl.pallas_call(
        _ada_kernel,
        out_shape=jax.ShapeDtypeStruct((m, n), F32),
        grid=(n // tn,),
        in_specs=[pl.BlockSpec((m, k), lambda j: (0, 0)),
                  pl.BlockSpec((k, tn), lambda j: (0, j)),
                  pl.BlockSpec((1, tn), lambda j: (0, j))],
        out_specs=pl.BlockSpec((m, tn), lambda j: (0, j)),
        compiler_params=_cparams(("arbitrary",)),
        name="ada",
    )(cvec, w, b.reshape(1, n))


def _inproj_kernel(x_ref, g_ref, mod_ref, w_ref, o_ref, hb_ref, *, tm, n_ctx):
    i = pl.program_id(1)
    j = pl.program_id(2)

    @pl.when(j == 0)
    def _():
        x = x_ref[0]
        ms = jnp.mean(x * x, axis=-1, keepdims=True)
        y = x * lax.rsqrt(ms + NORM_EPS) * g_ref[...]
        rows = i * tm + lax.broadcasted_iota(jnp.int32, (tm, 1), 0)
        sh = _row_select(mod_ref, 0, rows, n_ctx)
        sc = _row_select(mod_ref, 1, rows, n_ctx)
        hb_ref[...] = (y * (1.0 + sc) + sh).astype(BF16)

    o_ref[0] = _dot(hb_ref[...], w_ref[...])


def _inproj(x, g, mods, w, n_ctx):
    bsz, t, d = x.shape
    n = w.shape[1]
    tm = _pick(t, (768, 256))
    tn = 1024
    return pl.pallas_call(
        functools.partial(_inproj_kernel, tm=tm, n_ctx=n_ctx),
        out_shape=jax.ShapeDtypeStruct((bsz, t, n), F32),
        grid=(bsz, t // tm, n // tn),
        in_specs=[pl.BlockSpec((1, tm, d), lambda b, i, j: (b, i, 0)),
                  pl.BlockSpec((1, d), lambda b, i, j: (0, 0)),
                  pl.BlockSpec((1, 2, 6, d), lambda b, i, j: (b, 0, 0, 0)),
                  pl.BlockSpec((d, tn), lambda b, i, j: (0, j))],
        out_specs=pl.BlockSpec((1, tm, tn), lambda b, i, j: (b, i, j)),
        scratch_shapes=[pltpu.VMEM((tm, d), BF16)],
        compiler_params=_cparams(("parallel", "parallel", "arbitrary")),
        name="inproj",
    )(x, g.reshape(1, d), mods, w)


def _rope_kernel(q_ref, k_ref, v_ref, c_ref, s1_ref, s2_ref, qo_ref, ko_ref, vo_ref):
    d = q_ref.shape[-1]
    reps = d // c_ref.shape[-1]
    cos = jnp.concatenate([c_ref[...]] * reps, axis=1)
    s1 = jnp.concatenate([s1_ref[...]] * reps, axis=1)
    s2 = jnp.concatenate([s2_ref[...]] * reps, axis=1)
    quarter = HEAD_DIM // 4

    def rope(x):
        return x * cos + pltpu.roll(x, d - quarter, 1) * s1 + pltpu.roll(x, quarter, 1) * s2

    qo_ref[0] = (rope(q_ref[0]) * (HEAD_DIM ** -0.5)).astype(BF16)
    ko_ref[0] = rope(k_ref[0]).astype(BF16)
    vo_ref[0] = v_ref[0].astype(BF16)


def _rope_tables(t, n_ctx):
    nf = HEAD_DIM // 4
    inv_freq = ROPE_BASE ** (-jnp.arange(nf, dtype=F32) / nf)
    tok = jnp.arange(t) - n_ctx
    row = (tok // GRID_W).astype(F32)
    col = (tok % GRID_W).astype(F32)
    lane = np.arange(PAIR) % HEAD_DIM
    use_col = jnp.asarray(lane >= HEAD_DIM // 2)
    first = jnp.asarray((lane % (HEAD_DIM // 2)) < nf)
    pos = jnp.where(use_col[None, :], col[:, None], row[:, None])
    ang = pos * inv_freq[lane % nf][None, :]
    is_lat = (tok >= 0)[:, None]
    cos = jnp.where(is_lat, jnp.cos(ang), 1.0)
    sin = jnp.where(is_lat, jnp.sin(ang), 0.0)
    return cos, jnp.where(first[None, :], -sin, 0.0), jnp.where(first[None, :], 0.0, sin)


def _rope(z, tables):
    bsz, t, _ = z.shape
    d = D_MODEL
    tm = _pick(t, (768, 256))
    zspec = lambda g: pl.BlockSpec((1, tm, d), lambda b, i: (b, i, g))
    tspec = pl.BlockSpec((tm, PAIR), lambda b, i: (i, 0))
    ospec = pl.BlockSpec((1, tm, d), lambda b, i: (b, i, 0))
    oshape = jax.ShapeDtypeStruct((bsz, t, d), BF16)
    return pl.pallas_call(
        _rope_kernel,
        out_shape=(oshape, oshape, oshape),
        grid=(bsz, t // tm),
        in_specs=[zspec(0), zspec(1), zspec(2), tspec, tspec, tspec],
        out_specs=(ospec, ospec, ospec),
        compiler_params=_cparams(("parallel", "parallel")),
        name="rope",
    )(z, z, z, *tables)


def _natten_kernel(q_ref, *refs):
    k_refs = refs[0:WIN_H]
    v_refs = refs[WIN_H:2 * WIN_H]
    kc_ref, vc_ref, tb_ref, o_ref = refs[2 * WIN_H:]
    nq = q_ref.shape[1]
    lane = lax.broadcasted_iota(jnp.int32, (nq, PAIR), 1)
    first = lane < HEAD_DIM
    zero = jnp.zeros((nq, PAIR), BF16)
    sls = [slice(p * PAIR, (p + 1) * PAIR) for p in range(N_HEADS // 2)]
    qs = [jnp.concatenate([jnp.where(first, q_ref[0, :, sl], zero), jnp.where(first, zero, q_ref[0, :, sl])], axis=0)
          for sl in sls]
    ks = [jnp.concatenate([r[0, :, sl] for r in k_refs] + [kc_ref[0, :, sl]], axis=0) for sl in sls]
    scores = [_dot_nt(q, k) for q, k in zip(qs, ks)]
    probs = []
    for p, s in enumerate(scores):
        tb = tb_ref[0, p]
        s = jnp.where(tb > 0.5 * MASKED, s + tb, NEG_INF)
        e = jnp.exp(s - jnp.max(s, axis=-1, keepdims=True))
        probs.append((e / jnp.sum(e, axis=-1, keepdims=True)).astype(BF16))
    vs = [jnp.concatenate([r[0, :, sl] for r in v_refs] + [vc_ref[0, :, sl]], axis=0) for sl in sls]
    outs = [_dot(pr, v) for pr, v in zip(probs, vs)]
    for sl, o in zip(sls, outs):
        o_ref[0, :, sl] = jnp.where(first, o[:nq], o[nq:]).astype(BF16)


def _natten_table(rpb, n_ctx):
    col = np.arange(GRID_W)
    c0 = np.clip(col - WIN_W // 2, 0, GRID_W - WIN_W)
    in_win = (col[None, :] >= c0[:, None]) & (col[None, :] < c0[:, None] + WIN_W)
    dc_idx = np.clip(col[None, :] - col[:, None] + WIN_W - 1, 0, 2 * WIN_W - 2)
    dr = np.arange(WIN_H)[None, :] - np.arange(WIN_H)[:, None] + WIN_H - 1
    b = rpb.astype(F32)[:, dr][:, :, :, dc_idx]
    b = jnp.where(jnp.asarray(in_win)[None, None, None], b, MASKED)
    b = jnp.transpose(b, (1, 0, 3, 2, 4)).reshape(WIN_H, N_HEADS, GRID_W, WIN_H * GRID_W)
    b = jnp.concatenate([b, jnp.full((1,) + b.shape[1:], MASKED, F32)], axis=0)
    b = jnp.concatenate([b, jnp.zeros(b.shape[:3] + (n_ctx,), F32)], axis=-1)
    return b.reshape(WIN_H + 1, N_HEADS // 2, 2 * GRID_W, -1)


def _natten(qr, kr, vb, table, n_ctx):
    bsz, t, d = qr.shape
    nqc = n_ctx // GRID_W
    rows = (t - n_ctx) // GRID_W
    assert rows >= WIN_H
    nkeys = table.shape[-1]

    def r0(s):
        return jnp.clip(s - nqc - WIN_H // 2, 0, rows - WIN_H)

    def variant(s):
        return jnp.where(s < nqc, WIN_H, s - nqc - r0(s))

    blk = (1, GRID_W, d)
    kv_specs = [pl.BlockSpec(blk, functools.partial(lambda b, s, i: (b, nqc + r0(s) + i, 0), i=i))
                for i in range(WIN_H)]
    ctx_spec = pl.BlockSpec((1, n_ctx, d), lambda b, s: (b, 0, 0))
    return pl.pallas_call(
        _natten_kernel,
        out_shape=jax.ShapeDtypeStruct((bsz, t, d), BF16),
        grid=(bsz, t // GRID_W),
        in_specs=[pl.BlockSpec(blk, lambda b, s: (b, s, 0))] + kv_specs + kv_specs + [
            ctx_spec, ctx_spec,
            pl.BlockSpec((1, N_HEADS // 2, 2 * GRID_W, nkeys), lambda b, s: (variant(s), 0, 0, 0))],
        out_specs=pl.BlockSpec(blk, lambda b, s: (b, s, 0)),
        compiler_params=_cparams(("parallel", "arbitrary")),
        name="natten",
    )(qr, *([kr] * WIN_H), *([vb] * WIN_H), kr, vb, table)


def _rglru_kernel(u_ref, up_ref, un_ref, cw_ref, cb_ref, wa_ref, ba_ref, wx_ref, bx_ref, lam_ref,
                  o_ref, a_s, b_s, h_s, *, tb, nt, nctx):
    d = pl.program_id(1)
    i = pl.program_id(2)
    blk = _seq_block(d, i, nt, nctx)
    seq_first = (blk == 0) | (blk == nctx)
    seq_last = (blk == nctx - 1) | (blk == nt - 1)
    x = u_ref[0]
    prev = jnp.where(seq_first, 0.0, up_ref[0])
    nxt = jnp.where(seq_last, 0.0, un_ref[0])
    row = lax.broadcasted_iota(jnp.int32, (tb, 1), 0)
    xm1 = jnp.where(row == 0, prev[7:8], pltpu.roll(x, 1, 0))
    xm2 = jnp.where(row == 0, prev[6:7], jnp.where(row == 1, prev[7:8], pltpu.roll(x, 2, 0)))
    xp1 = jnp.where(row == tb - 1, nxt[0:1], pltpu.roll(x, tb - 1, 0))
    cw = cw_ref[...]
    u = cw[0:1] * xm2 + cw[1:2] * xm1 + cw[2:3] * x + cw[3:4] * xp1 + cb_ref[...]
    ub = u.astype(BF16)
    gw = wa_ref.shape[-1]
    ngrp = u.shape[-1] // gw
    gr = jnp.concatenate([_dot(ub[:, g * gw:(g + 1) * gw], wa_ref[0, g]) for g in range(ngrp)], axis=1)
    gi = jnp.concatenate([_dot(ub[:, g * gw:(g + 1) * gw], wx_ref[0, g]) for g in range(ngrp)], axis=1)
    r = _sigmoid(gr + ba_ref[0])
    ig = _sigmoid(gi + bx_ref[0])
    nl = -lam_ref[0]
    softplus = jnp.maximum(nl, 0.0) + jnp.log(1.0 + jnp.exp(-jnp.abs(nl)))
    log_a = -RG_C * r * softplus
    a_s[...] = jnp.exp(log_a)
    b_s[...] = jnp.sqrt(1.0 - jnp.exp(2.0 * log_a)) * (ig * u)

    @pl.when(i == 0)
    def _():
        h_s[...] = jnp.zeros_like(h_s)

    def step(t, h):
        tt = jnp.where(d == 0, t, tb - 1 - t)
        h = a_s[pl.ds(tt, 1), :] * h + b_s[pl.ds(tt, 1), :]
        o_ref[0, 0, pl.ds(tt, 1), :] = h
        return h

    h_s[...] = lax.fori_loop(0, tb, step, h_s[...], unroll=8)


def _rglru(z, conv_w, conv_b, wa, ba, wx, bx, lam, n_ctx):
    bsz, t, _ = z.shape
    d = D_MODEL
    tb = 256
    nt = t // tb
    nctx = n_ctx // tb
    hb = tb // 8
    gw = wa.shape[-1]

    def blk_of(dd, i):
        return _seq_block(dd, i, nt, nctx)

    vec = lambda: pl.BlockSpec((1, 1, d), lambda b, dd, i: (dd, 0, 0))
    wspec = lambda: pl.BlockSpec((1, d // gw, gw, gw), lambda b, dd, i: (dd, 0, 0, 0))
    return pl.pallas_call(
        functools.partial(_rglru_kernel, tb=tb, nt=nt, nctx=nctx),
        out_shape=jax.ShapeDtypeStruct((2, bsz, t, d), F32),
        grid=(bsz, 2, nt),
        in_specs=[pl.BlockSpec((1, tb, d), lambda b, dd, i: (b, blk_of(dd, i), 3)),
                  pl.BlockSpec((1, 8, d), lambda b, dd, i: (b, jnp.maximum(blk_of(dd, i) * hb - 1, 0), 3)),
                  pl.BlockSpec((1, 8, d), lambda b, dd, i: (b, jnp.minimum((blk_of(dd, i) + 1) * hb, t // 8 - 1), 3)),
                  pl.BlockSpec((4, d), lambda b, dd, i: (0, 0)),
                  pl.BlockSpec((1, d), lambda b, dd, i: (0, 0)),
                  wspec(), vec(), wspec(), vec(), vec()],
        out_specs=pl.BlockSpec((1, 1, tb, d), lambda b, dd, i: (dd, b, blk_of(dd, i), 0)),
        scratch_shapes=[pltpu.VMEM((tb, d), F32), pltpu.VMEM((tb, d), F32), pltpu.VMEM((1, d), F32)],
        compiler_params=_cparams(("parallel", "arbitrary", "arbitrary")),
        name="rglru",
    )(z, z, z, conv_w, conv_b.reshape(1, d), wa, ba.reshape(2, 1, d), wx, bx.reshape(2, 1, d),
      lam.reshape(2, 1, d))


def _block_diag_groups(w, per):
    two, nb, bs, _ = w.shape
    w = w.reshape(two, nb // per, per, bs, bs)
    eye = jnp.eye(per, dtype=w.dtype)
    return jnp.einsum('dgaij,ab->dgaibj', w, eye).reshape(two, nb // per, per * bs, per * bs)


def _rwkv_prep_kernel(*refs, tb, nt, nctx, has_vres):
    (zr, zr_p, zr_n, zk, zk_p, zk_n, zv, zv_p, zv_n, zl, zl_p, zl_n) = refs[:12]
    refs = refs[12:]
    if has_vres:
        vf_ref, refs = refs[0], refs[1:]
    (mu_ref, kk_ref, ka_ref, rk_ref, w0_ref, a0_ref, v0_ref, w2_ref,
     r_o, v_o, kk_o, g_o, bonus_o, lw_o, kd_o, bb_o) = refs
    blk = pl.program_id(1)
    seq_first = (blk == 0) | (blk == nctx)
    seq_last = (blk == nctx - 1) | (blk == nt - 1)
    row = lax.broadcasted_iota(jnp.int32, (tb, 1), 0)

    def nbr_mean(main, prev_row, next_row):
        prev_row = jnp.where(seq_first, 0.0, prev_row)
        next_row = jnp.where(seq_last, 0.0, next_row)
        up = jnp.where(row == 0, prev_row, pltpu.roll(main, 1, 0))
        dn = jnp.where(row == tb - 1, next_row, pltpu.roll(main, tb - 1, 0))
        return 0.5 * (up + dn)

    def shifted(m_ref, p_ref, n_ref, mu):
        main = m_ref[0]
        return main + mu * (nbr_mean(main, p_ref[0, 7:8, :], n_ref[0, 0:1, :]) - main)

    mu = mu_ref[...]
    r = shifted(zr, zr_p, zr_n, mu[0:1])
    k = shifted(zk, zk_p, zk_n, mu[1:2])
    v = shifted(zv, zv_p, zv_n, mu[2:3])

    lr = zl[0, :, 0:LR_W] + nbr_mean(zl[0, :, LR_W:2 * LR_W], zl_p[0, 7:8, LR_W:2 * LR_W],
                                     zl_n[0, 0:1, LR_W:2 * LR_W])
    col = lax.broadcasted_iota(jnp.int32, (1, LR_W), 1)
    act = jnp.where(col < 128, jnp.tanh(lr), jnp.where((col >= 256) & (col < 416), _sigmoid(lr), lr))
    actb = act.astype(BF16)
    s0 = actb[:, 0:256]
    s1 = actb[:, 256:512]
    g_o[0] = _dot(s1, w2_ref[4])
    if has_vres:
        v = v + (vf_ref[0] - v) * _sigmoid(v0_ref[...] + _dot(s1, w2_ref[5]))
    r_o[0] = r
    v_o[0] = v
    kkr = k * kk_ref[...]
    kk = kkr / jnp.maximum(jnp.sqrt(_seg_sum64(kkr * kkr)), 1e-12)
    kk_o[0] = kk
    ksum = jnp.zeros_like(k)
    for dd in range(2):
        zw = w0_ref[dd:dd + 1, :] + _dot(s0, w2_ref[dd])
        lw_o[dd, 0] = -math.exp(-0.5) * _sigmoid(zw)
        a = _sigmoid(a0_ref[dd:dd + 1, :] + _dot(s0, w2_ref[2 + dd]))
        kd = k * (1.0 + (a - 1.0) * ka_ref[...])
        kd_o[dd, 0] = kd
        bb_o[dd, 0] = kk * a
        ksum = ksum + kd
    bonus_o[0] = _seg_sum64(r * ksum * rk_ref[...]) * v


def _rwkv_prep(z, vfirst, mu_rkv, k_k, k_a, r_k, w0, a0, v0, w2cat, n_ctx):
    bsz, t, _ = z.shape
    d = D_MODEL
    tb = 256
    nt = t // tb
    nctx = n_ctx // tb
    hb = tb // 8
    has_vres = vfirst is not None

    def triple(g):
        return [pl.BlockSpec((1, tb, d), lambda b, i: (b, i, g)),
                pl.BlockSpec((1, 8, d), lambda b, i: (b, jnp.maximum(i * hb - 1, 0), g)),
                pl.BlockSpec((1, 8, d), lambda b, i: (b, jnp.minimum((i + 1) * hb, t // 8 - 1), g))]

    tok = pl.BlockSpec((1, tb, d), lambda b, i: (b, i, 0))
    tok2 = pl.BlockSpec((2, 1, tb, d), lambda b, i: (0, b, i, 0))
    const = lambda shape: pl.BlockSpec(shape, lambda b, i: (0,) * len(shape))
    in_specs = triple(5) + triple(6) + triple(7) + triple(11)
    args = [z] * 12
    if has_vres:
        in_specs.append(tok)
        args.append(vfirst)
    in_specs += [const((3, d)), const((1, d)), const((1, d)), const((1, d)), const((2, d)), const((2, d)),
                 const((1, d)), const(w2cat.shape)]
    args += [mu_rkv, k_k.reshape(1, d), k_a.reshape(1, d), r_k.reshape(1, d), w0, a0, v0.reshape(1, d), w2cat]
    one = jax.ShapeDtypeStruct((bsz, t, d), F32)
    two = jax.ShapeDtypeStruct((2, bsz, t, d), F32)
    return pl.pallas_call(
        functools.partial(_rwkv_prep_kernel, tb=tb, nt=nt, nctx=nctx, has_vres=has_vres),
        out_shape=(one, one, one, one, one, two, two, two),
        grid=(bsz, nt),
        in_specs=in_specs,
        out_specs=(tok, tok, tok, tok, tok, tok2, tok2, tok2),
        compiler_params=_cparams(("parallel", "parallel")),
        name="rwkv_prep",
    )(*args)


def _rwkv_scan_kernel(r_ref, v_ref, kk_ref, lw_ref, kd_ref, bb_ref, y_ref, s_ref):
    d = pl.program_id(1)
    i = pl.program_id(2)
    n = r_ref.shape[1]
    rev = d == 1

    @pl.when(i == 0)
    def _():
        s_ref[...] = jnp.zeros_like(s_ref)

    ti = lax.broadcasted_iota(jnp.int32, (n, n), 0)
    tj = lax.broadcasted_iota(jnp.int32, (n, n), 1)
    order = jnp.where(rev, tj - ti, ti - tj)
    tri = jnp.where(order >= 0, 1.0, 0.0).astype(BF16)
    lw = lw_ref[0, 0]
    h1 = lw.astype(BF16)
    r1 = lw - h1.astype(F32)
    h2 = r1.astype(BF16)
    h3 = (r1 - h2.astype(F32)).astype(BF16)
    c = _dot(tri, h1) + _dot(tri, h2) + _dot(tri, h3)
    c_last = jnp.where(rev, c[0:1], c[n - 1:n])
    r = r_ref[0]
    v = v_ref[0]
    kk = kk_ref[0]
    kd = kd_ref[0, 0]
    bb = bb_ref[0, 0]
    enc = jnp.exp(-c)
    el = jnp.exp(c_last - c)
    rh = r * jnp.exp(c)
    ah = -kk * jnp.exp(c - lw)
    bh = bb * enc
    kh = kd * enc
    bl = bb * el
    kl = kd * el
    gl = jnp.exp(c_last)

    lane = lax.broadcasted_iota(jnp.int32, (n, PAIR), 1)
    first = lane < HEAD_DIM
    si = lax.broadcasted_iota(jnp.int32, (2 * n, 2 * n), 0) & (n - 1)
    sj = lax.broadcasted_iota(jnp.int32, (2 * n, 2 * n), 1) & (n - 1)
    row2 = lax.broadcasted_iota(jnp.int32, (2 * n, 2 * n), 0)
    col2 = lax.broadcasted_iota(jnp.int32, (2 * n, 2 * n), 1)
    qi = jnp.where(rev, n - 1 - si, si)
    qj = jnp.where(rev, n - 1 - sj, sj)
    strict = qi > qj
    incl = qi >= qj
    eye = jnp.where(row2 == col2, 1.0, 0.0)
    levels = []
    blk = 1
    while blk < n:
        bi = qi // blk
        bj = qj // blk
        levels.append((bi // 2 == bj // 2) & (bi % 2 == 1) & (bj % 2 == 0))
        blk *= 2

    zero = jnp.zeros((n, PAIR), BF16)

    def stack(x, sl):
        return jnp.concatenate([jnp.where(first, x[:, sl], zero), jnp.where(first, zero, x[:, sl])], axis=0)

    wide = [t.astype(BF16) for t in (ah, rh, bh, kh, v, bl, kl)]
    zero2 = jnp.zeros((2 * n, 2 * n), BF16)
    for g0 in range(0, N_HEADS // 2, SCAN_GROUP):
        pairs = list(range(g0, g0 + SCAN_GROUP))
        sls = [slice(p * PAIR, (p + 1) * PAIR) for p in pairs]
        opnd = [[stack(t, sl) for t in wide] for sl in sls]
        s2 = [s_ref[p] for p in pairs]
        s2b = [s.astype(BF16) for s in s2]
        aa = [_dot_nt(jnp.concatenate([am, rm], axis=0), jnp.concatenate([bm, km], axis=0))
              for am, rm, bm, km, _, _, _ in opnd]
        a_ab = [jnp.where(strict, a[:2 * n, :2 * n], 0.0) for a in aa]
        a_abb = [a.astype(BF16) for a in a_ab]
        a_ak = [jnp.where(strict, a[:2 * n, 2 * n:], 0.0).astype(BF16) for a in aa]
        a_r = [jnp.concatenate([jnp.where(incl, a[2 * n:, :2 * n], 0.0), jnp.where(incl, a[2 * n:, 2 * n:], 0.0)],
                               axis=1).astype(BF16) for a in aa]
        x = [_dot_nt(o[0], sb) + _dot(ak, o[4]) for o, sb, ak in zip(opnd, s2b, a_ak)]
        tinv = [eye + jnp.where(levels[0], a, 0.0) for a in a_ab]
        for m in levels[1:]:
            tb = [t.astype(BF16) for t in tinv]
            half = [_dot(t, jnp.where(m, a, zero2)) for t, a in zip(tb, a_abb)]
            tinv = [t + _dot(h.astype(BF16), q) for t, h, q in zip(tinv, half, tb)]
        uv = [jnp.concatenate([_dot(t.astype(BF16), xx.astype(BF16)).astype(BF16), o[4]], axis=0)
              for t, xx, o in zip(tinv, x, opnd)]
        for j, p in enumerate(pairs):
            rm, blm, klm = opnd[j][1], opnd[j][5], opnd[j][6]
            o = _dot_nt(rm, s2b[j]) + _dot(a_r[j], uv[j])
            y_ref[0, 0, :, sls[j]] = o[:n] + o[n:]
            s_ref[p] = s2[j] * gl[:, sls[j]] + _dot_tn(uv[j], jnp.concatenate([blm, klm], axis=0))


def _rwkv_scan(r, v, kk, lw, kd, bb, n_ctx):
    bsz, t, d = r.shape
    nc = t // CHUNK
    nctx = n_ctx // CHUNK

    def blk_of(dd, i):
        return _seq_block(dd, i, nc, nctx)

    tok = pl.BlockSpec((1, CHUNK, d), lambda b, dd, i: (b, blk_of(dd, i), 0))
    tok2 = pl.BlockSpec((1, 1, CHUNK, d), lambda b, dd, i: (dd, b, blk_of(dd, i), 0))
    return pl.pallas_call(
        _rwkv_scan_kernel,
        out_shape=jax.ShapeDtypeStruct((2, bsz, t, d), F32),
        grid=(bsz, 2, nc),
        in_specs=[tok, tok, tok, tok2, tok2, tok2],
        out_specs=tok2,
        scratch_shapes=[pltpu.VMEM((N_HEADS // 2, PAIR, PAIR), F32)],
        compiler_params=_cparams(("parallel", "arbitrary", "arbitrary")),
        name="rwkv_scan",
    )(r, v, kk, lw, kd, bb)


def _gelu_tanh(x):
    return 0.5 * x * (1.0 + jnp.tanh(math.sqrt(2.0 / math.pi) * (x + 0.044715 * (x * x * x))))


def _merge_kernel(x_ref, ya_ref, hl_ref, ug_ref, yw_ref, bonus_ref, g_ref, ma_ref, mb_ref, mc_ref,
                  mod_ref, lnw_ref, lnb_ref, wa_ref, wb_ref, wc_ref, wo_ref, o_ref, *, tm, n_ctx):
    i = pl.program_id(1)
    yb = ((hl_ref[0, 0] + hl_ref[1, 0]) * _gelu_tanh(ug_ref[0])).astype(BF16)
    y = yw_ref[0, 0] + yw_ref[1, 0]
    inv = 1.0 / HEAD_DIM
    mean = _seg_sum64(y) * inv
    yc = y - mean
    var = _seg_sum64(yc * yc) * inv
    yn = yc * lax.rsqrt(var + LNX_EPS) * lnw_ref[...] + lnb_ref[...]
    oc = ((yn + bonus_ref[0]) * g_ref[0]).astype(BF16)
    mixed = (_sigmoid(ma_ref[0]) * _dot(ya_ref[0], wa_ref[...])
             + _sigmoid(mb_ref[0]) * _dot(yb, wb_ref[...])
             + _sigmoid(mc_ref[0]) * _dot(oc, wc_ref[...]))
    yo = _dot(mixed.astype(BF16), wo_ref[...])
    rows = i * tm + lax.broadcasted_iota(jnp.int32, (tm, 1), 0)
    o_ref[0] = x_ref[0] + _row_select(mod_ref, 2, rows, n_ctx) * yo


def _merge(x, ynat, hlru, z, yrw, bonus, g, mods, ln_w, ln_b, wa, wb, wc, wo, n_ctx):
    bsz, t, d = x.shape
    tm = 256
    tok = pl.BlockSpec((1, tm, d), lambda b, i: (b, i, 0))
    tok2 = pl.BlockSpec((2, 1, tm, d), lambda b, i: (0, b, i, 0))
    zspec = lambda gidx: pl.BlockSpec((1, tm, d), lambda b, i: (b, i, gidx))
    vec = pl.BlockSpec((1, d), lambda b, i: (0, 0))
    wspec = pl.BlockSpec((d, d), lambda b, i: (0, 0))
    return pl.pallas_call(
        functools.partial(_merge_kernel, tm=tm, n_ctx=n_ctx),
        out_shape=jax.ShapeDtypeStruct((bsz, t, d), F32),
        grid=(bsz, t // tm),
        in_specs=[tok, tok, tok2, zspec(4), tok2, tok, tok, zspec(8), zspec(9), zspec(10),
                  pl.BlockSpec((1, 2, 6, d), lambda b, i: (b, 0, 0, 0)), vec, vec,
                  wspec, wspec, wspec, wspec],
        out_specs=tok,
        compiler_params=_cparams(("parallel", "parallel")),
        name="merge",
    )(x, ynat, hlru, z, yrw, bonus, g, z, z, z, mods, ln_w.reshape(1, d), ln_b.reshape(1, d), wa, wb, wc, wo)


def _mlp_kernel(x_ref, g_ref, mod_ref, w1_ref, w2_ref, fg_ref, o_ref, hb_ref, acc_ref, *, tm, n_ctx, final):
    i = pl.program_id(1)
    k = pl.program_id(2)
    rows = i * tm + lax.broadcasted_iota(jnp.int32, (tm, 1), 0)

    @pl.when(k == 0)
    def _():
        x = x_ref[0]
        ms = jnp.mean(x * x, axis=-1, keepdims=True)
        y = x * lax.rsqrt(ms + NORM_EPS) * g_ref[...]
        sh = _row_select(mod_ref, 3, rows, n_ctx)
        sc = _row_select(mod_ref, 4, rows, n_ctx)
        hb_ref[...] = (y * (1.0 + sc) + sh).astype(BF16)
        acc_ref[...] = jnp.zeros_like(acc_ref)

    a = jnp.maximum(_dot(hb_ref[...], w1_ref[...]), 0.0)
    acc_ref[...] += _dot((a * a).astype(BF16), w2_ref[...])

    @pl.when(k == pl.num_programs(2) - 1)
    def _():
        xn = x_ref[0] + _row_select(mod_ref, 5, rows, n_ctx) * acc_ref[...]
        if final:
            ms = jnp.mean(xn * xn, axis=-1, keepdims=True)
            xn = xn * lax.rsqrt(ms + NORM_EPS) * fg_ref[...]
        o_ref[0] = xn


def _mlp(x, g, mods, w1, w2, final_g, n_ctx, final):
    bsz, t, d = x.shape
    dff = w1.shape[1]
    tm = _pick(t, (768, 256))
    tk = 1024
    vec = pl.BlockSpec((1, d), lambda b, i, k: (0, 0))
    tok = pl.BlockSpec((1, tm, d), lambda b, i, k: (b, i, 0))
    return pl.pallas_call(
        functools.partial(_mlp_kernel, tm=tm, n_ctx=n_ctx, final=final),
        out_shape=jax.ShapeDtypeStruct((bsz, t, d), F32),
        grid=(bsz, t // tm, dff // tk),
        in_specs=[tok, vec, pl.BlockSpec((1, 2, 6, d), lambda b, i, k: (b, 0, 0, 0)),
                  pl.BlockSpec((d, tk), lambda b, i, k: (0, k)),
                  pl.BlockSpec((tk, d), lambda b, i, k: (k, 0)), vec],
        out_specs=tok,
        scratch_shapes=[pltpu.VMEM((tm, d), BF16), pltpu.VMEM((tm, d), F32)],
        compiler_params=_cparams(("parallel", "parallel", "arbitrary")),
        name="mlp",
    )(x, g.reshape(1, d), mods, w1, w2, final_g.reshape(1, d))


def _lowrank_weights(mu_h, w1, a1, g1, v1, mu_v):
    d = mu_h.shape[-1]
    blocks = [(w1[0], mu_h[0]), (w1[1], mu_h[0]), (a1[0], mu_h[1]), (a1[1], mu_h[1]), (g1, mu_h[2])]
    if v1 is not None:
        blocks.append((v1, mu_v))
    p = jnp.concatenate([w * (1.0 - m)[:, None] for w, m in blocks], axis=1)
    q = jnp.concatenate([w * m[:, None] for w, m in blocks], axis=1)
    pad = lambda a: jnp.pad(a, ((0, 0), (0, LR_W - a.shape[1])))
    return jnp.concatenate([pad(p), pad(q)], axis=1)


def _second_stage_weights(w2, a2, g2, v2):
    d = w2.shape[-1]
    def place(w, off):
        return jnp.zeros((256, d), F32).at[off:off + w.shape[0]].set(w)
    v2p = place(v2, 160) if v2 is not None else jnp.zeros((256, d), F32)
    return jnp.stack([place(w2[0], 0), place(w2[1], 64), place(a2[0], 128), place(a2[1], 192),
                      place(g2, 0), v2p]).astype(BF16)


def kernel(x, c, ctx, c_ctx, w_ada, b_ada, norm1_g, norm2_g, w_in, rpb, conv_w, conv_b, rg_wa, rg_ba, rg_wx, rg_bx, rg_lam, rw_mu_rkv, rw_mu_h, rw_w0, rw_w1, rw_w2, rw_a0, rw_a1, rw_a2, rw_g1, rw_g2, rw_k_k, rw_k_a, rw_r_k, rw_ln_w, rw_ln_b, vres_v0, vres_v1, vres_v2, vres_mu, w_br_a, w_br_b, w_br_c, w_out, w_ff1, w_ff2, final_g):
    bsz, seq, d = x.shape
    n_ctx = ctx.shape[1]
    depth = w_in.shape[0]
    t = n_ctx + seq
    xs = jnp.concatenate([ctx, x], axis=1)
    cvec = jnp.zeros((8, d), F32).at[:bsz].set(c).at[bsz].set(c_ctx)
    tables = _rope_tables(t, n_ctx)
    vfirst = None
    for i in range(depth):
        last = i == depth - 1
        mod = _ada(cvec, w_ada[i], b_ada[i])
        mods = jnp.stack([jnp.broadcast_to(mod[bsz].reshape(1, 6, d), (bsz, 6, d)),
                          mod[:bsz].reshape(bsz, 6, d)], axis=1)
        if i == 0:
            lr_w = _lowrank_weights(rw_mu_h[i], rw_w1[i], rw_a1[i], rw_g1[i], None, None)
            w2cat = _second_stage_weights(rw_w2[i], rw_a2[i], rw_g2[i], None)
            v0 = jnp.zeros((d,), F32)
        else:
            lr_w = _lowrank_weights(rw_mu_h[i], rw_w1[i], rw_a1[i], rw_g1[i], vres_v1[i - 1], vres_mu[i - 1])
            w2cat = _second_stage_weights(rw_w2[i], rw_a2[i], rw_g2[i], vres_v2[i - 1])
            v0 = vres_v0[i - 1]
        w_all = jnp.concatenate([w_in[i], lr_w], axis=1).astype(BF16)
        z = _inproj(xs, norm1_g[i], mods, w_all, n_ctx)
        qr, kr, vb = _rope(z, tables)
        ynat = _natten(qr, kr, vb, _natten_table(rpb[i], n_ctx), n_ctx)
        hlru = _rglru(z, conv_w[i], conv_b[i],
                      _block_diag_groups(rg_wa[i], 4).astype(BF16), rg_ba[i],
                      _block_diag_groups(rg_wx[i], 4).astype(BF16), rg_bx[i], rg_lam[i], n_ctx)
        r, v, kk, g, bonus, lw, kd, bb = _rwkv_prep(
            z, vfirst, rw_mu_rkv[i], rw_k_k[i], rw_k_a[i], rw_r_k[i].reshape(-1), rw_w0[i], rw_a0[i], v0,
            w2cat, n_ctx)
        if i == 0:
            vfirst = v
        yrw = _rwkv_scan(r, v, kk, lw, kd, bb, n_ctx)
        xs = _merge(xs, ynat, hlru, z, yrw, bonus, g, mods, rw_ln_w[i], rw_ln_b[i],
                    w_br_a[i].astype(BF16), w_br_b[i].astype(BF16), w_br_c[i].astype(BF16),
                    w_out[i].astype(BF16), n_ctx)
        xs = _mlp(xs, norm2_g[i], mods, w_ff1[i].astype(BF16), w_ff2[i].astype(BF16), final_g, n_ctx, last)
    return xs[:, n_ctx:]
```

```python
import functools
import math

import jax
import jax.numpy as jnp
import numpy as np
from jax import lax
from jax.experimental import pallas as pl
from jax.experimental.pallas import tpu as pltpu

F32 = jnp.float32
BF16 = jnp.bfloat16

D_MODEL = 1024
GRID_W = 64
N_HEADS = 16
HEAD_DIM = 64
WIN_H = 8
WIN_W = 16
ROPE_BASE = 10000.0
NB_B = 16
BS_B = 64
RG_C = 8.0
LNX_EPS = 64e-5
NORM_EPS = 1e-6
NEG_INF = -1e30
MASKED = -1e30
N_ATT = 3
G_XB, G_GB, G_R, G_K, G_V, G_MA, G_MB, G_MC, G_LR = range(9)
LR_W = 512
CHUNK = 64
VMEM_LIMIT = 56 * 1024 * 1024
PAIR = 2 * HEAD_DIM
SCAN_GROUP = 16


def _cparams(sem):
    return pltpu.CompilerParams(dimension_semantics=sem, vmem_limit_bytes=VMEM_LIMIT)


def _pick(n, cands):
    for c in cands:
        if n % c == 0:
            return c
    raise ValueError(f"no tile for {n}")


def _dot(a, b):
    return jnp.dot(a, b, preferred_element_type=F32)


def _dot_nt(a, b):
    return lax.dot_general(a, b, (((1,), (1,)), ((), ())), preferred_element_type=F32)


def _dot_tn(a, b):
    return lax.dot_general(a, b, (((0,), (0,)), ((), ())), preferred_element_type=F32)


def _sigmoid(x):
    return 1.0 / (1.0 + jnp.exp(-x))


def _row_select(mod_ref, idx, rows, n_ctx):
    return jnp.where(rows < n_ctx, mod_ref[0, 0, idx:idx + 1, :], mod_ref[0, 1, idx:idx + 1, :])


def _seg_sum64(x):
    h1 = x.astype(BF16)
    r1 = x - h1.astype(F32)
    h2 = r1.astype(BF16)
    h3 = (r1 - h2.astype(F32)).astype(BF16)
    gi = lax.broadcasted_iota(jnp.int32, (3 * PAIR, PAIR), 0) % PAIR
    gj = lax.broadcasted_iota(jnp.int32, (3 * PAIR, PAIR), 1)
    ones = jnp.where(gi // HEAD_DIM == gj // HEAD_DIM, 1.0, 0.0).astype(BF16)
    out = []
    for p in range(x.shape[-1] // PAIR):
        sl = slice(p * PAIR, (p + 1) * PAIR)
        out.append(_dot(jnp.concatenate([h1[:, sl], h2[:, sl], h3[:, sl]], axis=1), ones))
    return jnp.concatenate(out, axis=1)


def _seq_block(d, i, nt, nctx):
    rev = jnp.where(i < nctx, nctx - 1 - i, nt - 1 - (i - nctx))
    return jnp.where(d == 0, i, rev)


def _ada_kernel(a_ref, w_ref, b_ref, o_ref):
    a = a_ref[...]
    s = a * _sigmoid(a)
    o_ref[...] = _dot(s.astype(BF16), w_ref[...].astype(BF16)) + b_ref[...]


def _ada(cvec, w, b):
    m, k = cvec.shape
    n = w.shape[1]
    tn = _pick(n, (1536, 1024, 512, 128))
    return pl.pallas_call(
        _ada_kernel,
        out_shape=jax.ShapeDtypeStruct((m, n), F32),
        grid=(n // tn,),
        in_specs=[pl.BlockSpec((m, k), lambda j: (0, 0)),
                  pl.BlockSpec((k, tn), lambda j: (0, j)),
                  pl.BlockSpec((1, tn), lambda j: (0, j))],
        out_specs=pl.BlockSpec((m, tn), lambda j: (0, j)),
        compiler_params=_cparams(("arbitrary",)),
        name="ada",
    )(cvec, w, b.reshape(1, n))


def _inproj_kernel(x_ref, g_ref, mod_ref, w_ref, c_ref, s1_ref, s2_ref, z_ref, q_ref, k_ref, v_ref, hb_ref,
                   *, tm, n_ctx):
    i = pl.program_id(1)
    j = pl.program_id(2)

    @pl.when(j == 0)
    def _():
        x = x_ref[0]
        ms = jnp.mean(x * x, axis=-1, keepdims=True)
        y = x * lax.rsqrt(ms + NORM_EPS) * g_ref[...]
        rows = i * tm + lax.broadcasted_iota(jnp.int32, (tm, 1), 0)
        sh = _row_select(mod_ref, 0, rows, n_ctx)
        sc = _row_select(mod_ref, 1, rows, n_ctx)
        hb_ref[...] = (y * (1.0 + sc) + sh).astype(BF16)

    acc = _dot(hb_ref[...], w_ref[...])
    d = acc.shape[-1]
    quarter = HEAD_DIM // 4

    def rope(x):
        reps = d // c_ref.shape[-1]
        cos = jnp.concatenate([c_ref[...]] * reps, axis=1)
        s1 = jnp.concatenate([s1_ref[...]] * reps, axis=1)
        s2 = jnp.concatenate([s2_ref[...]] * reps, axis=1)
        return x * cos + pltpu.roll(x, d - quarter, 1) * s1 + pltpu.roll(x, quarter, 1) * s2

    @pl.when(j == 0)
    def _():
        q_ref[0] = (rope(acc) * (HEAD_DIM ** -0.5)).astype(BF16)

    @pl.when(j == 1)
    def _():
        k_ref[0] = rope(acc).astype(BF16)

    @pl.when(j == 2)
    def _():
        v_ref[0] = acc.astype(BF16)

    @pl.when(j >= N_ATT)
    def _():
        z_ref[0] = acc


def _inproj(x, g, mods, w, tables, n_ctx):
    bsz, t, d = x.shape
    n = w.shape[1]
    tm = _pick(t, (768, 256))
    tn = d
    tspec = pl.BlockSpec((tm, PAIR), lambda b, i, j: (i, 0))
    aspec = pl.BlockSpec((1, tm, d), lambda b, i, j: (b, i, 0))
    ashape = jax.ShapeDtypeStruct((bsz, t, d), BF16)
    return pl.pallas_call(
        functools.partial(_inproj_kernel, tm=tm, n_ctx=n_ctx),
        out_shape=(jax.ShapeDtypeStruct((bsz, t, n - N_ATT * d), F32), ashape, ashape, ashape),
        grid=(bsz, t // tm, n // tn),
        in_specs=[pl.BlockSpec((1, tm, d), lambda b, i, j: (b, i, 0)),
                  pl.BlockSpec((1, d), lambda b, i, j: (0, 0)),
                  pl.BlockSpec((1, 2, 6, d), lambda b, i, j: (b, 0, 0, 0)),
                  pl.BlockSpec((d, tn), lambda b, i, j: (0, j)),
                  tspec, tspec, tspec],
        out_specs=(pl.BlockSpec((1, tm, tn), lambda b, i, j: (b, i, jnp.maximum(j - N_ATT, 0))),
                   aspec, aspec, aspec),
        scratch_shapes=[pltpu.VMEM((tm, d), BF16)],
        compiler_params=_cparams(("parallel", "parallel", "arbitrary")),
        name="inproj",
    )(x, g.reshape(1, d), mods, w, *tables)


def _rope_tables(t, n_ctx):
    nf = HEAD_DIM // 4
    inv_freq = ROPE_BASE ** (-jnp.arange(nf, dtype=F32) / nf)
    tok = jnp.arange(t) - n_ctx
    row = (tok // GRID_W).astype(F32)
    col = (tok % GRID_W).astype(F32)
    lane = np.arange(PAIR) % HEAD_DIM
    use_col = jnp.asarray(lane >= HEAD_DIM // 2)
    first = jnp.asarray((lane % (HEAD_DIM // 2)) < nf)
    pos = jnp.where(use_col[None, :], col[:, None], row[:, None])
    ang = pos * inv_freq[lane % nf][None, :]
    is_lat = (tok >= 0)[:, None]
    cos = jnp.where(is_lat, jnp.cos(ang), 1.0)
    sin = jnp.where(is_lat, jnp.sin(ang), 0.0)
    return cos, jnp.where(first[None, :], -sin, 0.0), jnp.where(first[None, :], 0.0, sin)


def _natten_kernel(q_ref, *refs):
    k_refs = refs[0:WIN_H]
    v_refs = refs[WIN_H:2 * WIN_H]
    kc_ref, vc_ref, tb_ref, o_ref = refs[2 * WIN_H:]
    nq = q_ref.shape[1]
    lane = lax.broadcasted_iota(jnp.int32, (nq, PAIR), 1)
    first = lane < HEAD_DIM
    zero = jnp.zeros((nq, PAIR), BF16)
    sls = [slice(p * PAIR, (p + 1) * PAIR) for p in range(N_HEADS // 2)]
    qs = [jnp.concatenate([jnp.where(first, q_ref[0, :, sl], zero), jnp.where(first, zero, q_ref[0, :, sl])], axis=0)
          for sl in sls]
    ks = [jnp.concatenate([r[0, :, sl] for r in k_refs] + [kc_ref[0, :, sl]], axis=0) for sl in sls]
    scores = [_dot_nt(q, k) for q, k in zip(qs, ks)]
    probs = []
    for p, s in enumerate(scores):
        tb = tb_ref[0, p]
        s = jnp.where(tb > 0.5 * MASKED, s + tb, NEG_INF)
        e = jnp.exp(s - jnp.max(s, axis=-1, keepdims=True))
        probs.append((e / jnp.sum(e, axis=-1, keepdims=True)).astype(BF16))
    vs = [jnp.concatenate([r[0, :, sl] for r in v_refs] + [vc_ref[0, :, sl]], axis=0) for sl in sls]
    outs = [_dot(pr, v) for pr, v in zip(probs, vs)]
    for sl, o in zip(sls, outs):
        o_ref[0, :, sl] = jnp.where(first, o[:nq], o[nq:]).astype(BF16)


def _natten_table(rpb, n_ctx):
    col = np.arange(GRID_W)
    c0 = np.clip(col - WIN_W // 2, 0, GRID_W - WIN_W)
    in_win = (col[None, :] >= c0[:, None]) & (col[None, :] < c0[:, None] + WIN_W)
    dc_idx = np.clip(col[None, :] - col[:, None] + WIN_W - 1, 0, 2 * WIN_W - 2)
    dr = np.arange(WIN_H)[None, :] - np.arange(WIN_H)[:, None] + WIN_H - 1
    b = rpb.astype(F32)[:, dr][:, :, :, dc_idx]
    b = jnp.where(jnp.asarray(in_win)[None, None, None], b, MASKED)
    b = jnp.transpose(b, (1, 0, 3, 2, 4)).reshape(WIN_H, N_HEADS, GRID_W, WIN_H * GRID_W)
    b = jnp.concatenate([b, jnp.full((1,) + b.shape[1:], MASKED, F32)], axis=0)
    b = jnp.concatenate([b, jnp.zeros(b.shape[:3] + (n_ctx,), F32)], axis=-1)
    return b.reshape(WIN_H + 1, N_HEADS // 2, 2 * GRID_W, -1)


def _natten(qr, kr, vb, table, n_ctx):
    bsz, t, d = qr.shape
    nqc = n_ctx // GRID_W
    rows = (t - n_ctx) // GRID_W
    assert rows >= WIN_H
    nkeys = table.shape[-1]

    def r0(s):
        return jnp.clip(s - nqc - WIN_H // 2, 0, rows - WIN_H)

    def variant(s):
        return jnp.where(s < nqc, WIN_H, s - nqc - r0(s))

    blk = (1, GRID_W, d)
    kv_specs = [pl.BlockSpec(blk, functools.partial(lambda b, s, i: (b, nqc + r0(s) + i, 0), i=i))
                for i in range(WIN_H)]
    ctx_spec = pl.BlockSpec((1, n_ctx, d), lambda b, s: (b, 0, 0))
    return pl.pallas_call(
        _natten_kernel,
        out_shape=jax.ShapeDtypeStruct((bsz, t, d), BF16),
        grid=(bsz, t // GRID_W),
        in_specs=[pl.BlockSpec(blk, lambda b, s: (b, s, 0))] + kv_specs + kv_specs + [
            ctx_spec, ctx_spec,
            pl.BlockSpec((1, N_HEADS // 2, 2 * GRID_W, nkeys), lambda b, s: (variant(s), 0, 0, 0))],
        out_specs=pl.BlockSpec(blk, lambda b, s: (b, s, 0)),
        compiler_params=_cparams(("parallel", "arbitrary")),
        name="natten",
    )(qr, *([kr] * WIN_H), *([vb] * WIN_H), kr, vb, table)


def _rglru_kernel(u_ref, up_ref, un_ref, cw_ref, cb_ref, wa_ref, ba_ref, wx_ref, bx_ref, lam_ref,
                  o_ref, a_s, b_s, h_s, *, tb, nt, nctx):
    d = pl.program_id(1)
    i = pl.program_id(2)
    blk = _seq_block(d, i, nt, nctx)
    seq_first = (blk == 0) | (blk == nctx)
    seq_last = (blk == nctx - 1) | (blk == nt - 1)
    x = u_ref[0]
    prev = jnp.where(seq_first, 0.0, up_ref[0])
    nxt = jnp.where(seq_last, 0.0, un_ref[0])
    row = lax.broadcasted_iota(jnp.int32, (tb, 1), 0)
    xm1 = jnp.where(row == 0, prev[7:8], pltpu.roll(x, 1, 0))
    xm2 = jnp.where(row == 0, prev[6:7], jnp.where(row == 1, prev[7:8], pltpu.roll(x, 2, 0)))
    xp1 = jnp.where(row == tb - 1, nxt[0:1], pltpu.roll(x, tb - 1, 0))
    cw = cw_ref[...]
    u = cw[0:1] * xm2 + cw[1:2] * xm1 + cw[2:3] * x + cw[3:4] * xp1 + cb_ref[...]
    ub = u.astype(BF16)
    gw = wa_ref.shape[-1]
    ngrp = u.shape[-1] // gw
    gr = jnp.concatenate([_dot(ub[:, g * gw:(g + 1) * gw], wa_ref[0, g]) for g in range(ngrp)], axis=1)
    gi = jnp.concatenate([_dot(ub[:, g * gw:(g + 1) * gw], wx_ref[0, g]) for g in range(ngrp)], axis=1)
    r = _sigmoid(gr + ba_ref[0])
    ig = _sigmoid(gi + bx_ref[0])
    nl = -lam_ref[0]
    softplus = jnp.maximum(nl, 0.0) + jnp.log(1.0 + jnp.exp(-jnp.abs(nl)))
    log_a = -RG_C * r * softplus
    a_s[...] = jnp.exp(log_a)
    b_s[...] = jnp.sqrt(1.0 - jnp.exp(2.0 * log_a)) * (ig * u)

    @pl.when(i == 0)
    def _():
        h_s[...] = jnp.zeros_like(h_s)

    def step(t, h):
        tt = jnp.where(d == 0, t, tb - 1 - t)
        h = a_s[pl.ds(tt, 1), :] * h + b_s[pl.ds(tt, 1), :]
        o_ref[0, 0, pl.ds(tt, 1), :] = h
        return h

    h_s[...] = lax.fori_loop(0, tb, step, h_s[...], unroll=8)


def _rglru(z, conv_w, conv_b, wa, ba, wx, bx, lam, n_ctx):
    bsz, t, _ = z.shape
    d = D_MODEL
    tb = 256
    nt = t // tb
    nctx = n_ctx // tb
    hb = tb // 8
    gw = wa.shape[-1]

    def blk_of(dd, i):
        return _seq_block(dd, i, nt, nctx)

    vec = lambda: pl.BlockSpec((1, 1, d), lambda b, dd, i: (dd, 0, 0))
    wspec = lambda: pl.BlockSpec((1, d // gw, gw, gw), lambda b, dd, i: (dd, 0, 0, 0))
    return pl.pallas_call(
        functools.partial(_rglru_kernel, tb=tb, nt=nt, nctx=nctx),
        out_shape=jax.ShapeDtypeStruct((2, bsz, t, d), F32),
        grid=(bsz, 2, nt),
        in_specs=[pl.BlockSpec((1, tb, d), lambda b, dd, i: (b, blk_of(dd, i), G_XB)),
                  pl.BlockSpec((1, 8, d), lambda b, dd, i: (b, jnp.maximum(blk_of(dd, i) * hb - 1, 0), G_XB)),
                  pl.BlockSpec((1, 8, d), lambda b, dd, i: (b, jnp.minimum((blk_of(dd, i) + 1) * hb, t // 8 - 1), G_XB)),
                  pl.BlockSpec((4, d), lambda b, dd, i: (0, 0)),
                  pl.BlockSpec((1, d), lambda b, dd, i: (0, 0)),
                  wspec(), vec(), wspec(), vec(), vec()],
        out_specs=pl.BlockSpec((1, 1, tb, d), lambda b, dd, i: (dd, b, blk_of(dd, i), 0)),
        scratch_shapes=[pltpu.VMEM((tb, d), F32), pltpu.VMEM((tb, d), F32), pltpu.VMEM((1, d), F32)],
        compiler_params=_cparams(("parallel", "arbitrary", "arbitrary")),
        name="rglru",
    )(z, z, z, conv_w, conv_b.reshape(1, d), wa, ba.reshape(2, 1, d), wx, bx.reshape(2, 1, d),
      lam.reshape(2, 1, d))


def _block_diag_groups(w, per):
    two, nb, bs, _ = w.shape
    w = w.reshape(two, nb // per, per, bs, bs)
    eye = jnp.eye(per, dtype=w.dtype)
    return jnp.einsum('dgaij,ab->dgaibj', w, eye).reshape(two, nb // per, per * bs, per * bs)


def _rwkv_prep_kernel(*refs, tb, nt, nctx, has_vres):
    (zr, zr_p, zr_n, zk, zk_p, zk_n, zv, zv_p, zv_n, zl, zl_p, zl_n) = refs[:12]
    refs = refs[12:]
    if has_vres:
        vf_ref, refs = refs[0], refs[1:]
    (mu_ref, kk_ref, ka_ref, rk_ref, w0_ref, a0_ref, v0_ref, w2_ref,
     r_o, v_o, kk_o, g_o, bonus_o, lw_o, kd_o, bb_o) = refs
    blk = pl.program_id(1)
    seq_first = (blk == 0) | (blk == nctx)
    seq_last = (blk == nctx - 1) | (blk == nt - 1)
    row = lax.broadcasted_iota(jnp.int32, (tb, 1), 0)

    def nbr_mean(main, prev_row, next_row):
        prev_row = jnp.where(seq_first, 0.0, prev_row)
        next_row = jnp.where(seq_last, 0.0, next_row)
        up = jnp.where(row == 0, prev_row, pltpu.roll(main, 1, 0))
        dn = jnp.where(row == tb - 1, next_row, pltpu.roll(main, tb - 1, 0))
        return 0.5 * (up + dn)

    def shifted(m_ref, p_ref, n_ref, mu):
        main = m_ref[0]
        return main + mu * (nbr_mean(main, p_ref[0, 7:8, :], n_ref[0, 0:1, :]) - main)

    mu = mu_ref[...]
    r = shifted(zr, zr_p, zr_n, mu[0:1])
    k = shifted(zk, zk_p, zk_n, mu[1:2])
    v = shifted(zv, zv_p, zv_n, mu[2:3])

    lr = zl[0, :, 0:LR_W] + nbr_mean(zl[0, :, LR_W:2 * LR_W], zl_p[0, 7:8, LR_W:2 * LR_W],
                                     zl_n[0, 0:1, LR_W:2 * LR_W])
    col = lax.broadcasted_iota(jnp.int32, (1, LR_W), 1)
    act = jnp.where(col < 128, jnp.tanh(lr), jnp.where((col >= 256) & (col < 416), _sigmoid(lr), lr))
    actb = act.astype(BF16)
    s0 = actb[:, 0:256]
    s1 = actb[:, 256:512]
    g_o[0] = _dot(s1, w2_ref[4])
    if has_vres:
        v = v + (vf_ref[0] - v) * _sigmoid(v0_ref[...] + _dot(s1, w2_ref[5]))
    r_o[0] = r
    v_o[0] = v
    kkr = k * kk_ref[...]
    kk = kkr / jnp.maximum(jnp.sqrt(_seg_sum64(kkr * kkr)), 1e-12)
    kk_o[0] = kk
    ksum = jnp.zeros_like(k)
    for dd in range(2):
        zw = w0_ref[dd:dd + 1, :] + _dot(s0, w2_ref[dd])
        lw_o[dd, 0] = -math.exp(-0.5) * _sigmoid(zw)
        a = _sigmoid(a0_ref[dd:dd + 1, :] + _dot(s0, w2_ref[2 + dd]))
        kd = k * (1.0 + (a - 1.0) * ka_ref[...])
        kd_o[dd, 0] = kd
        bb_o[dd, 0] = kk * a
        ksum = ksum + kd
    bonus_o[0] = _seg_sum64(r * ksum * rk_ref[...]) * v


def _rwkv_prep(z, vfirst, mu_rkv, k_k, k_a, r_k, w0, a0, v0, w2cat, n_ctx):
    bsz, t, _ = z.shape
    d = D_MODEL
    tb = 256
    nt = t // tb
    nctx = n_ctx // tb
    hb = tb // 8
    has_vres = vfirst is not None

    def triple(g):
        return [pl.BlockSpec((1, tb, d), lambda b, i: (b, i, g)),
                pl.BlockSpec((1, 8, d), lambda b, i: (b, jnp.maximum(i * hb - 1, 0), g)),
                pl.BlockSpec((1, 8, d), lambda b, i: (b, jnp.minimum((i + 1) * hb, t // 8 - 1), g))]

    tok = pl.BlockSpec((1, tb, d), lambda b, i: (b, i, 0))
    tok2 = pl.BlockSpec((2, 1, tb, d), lambda b, i: (0, b, i, 0))
    const = lambda shape: pl.BlockSpec(shape, lambda b, i: (0,) * len(shape))
    in_specs = triple(G_R) + triple(G_K) + triple(G_V) + triple(G_LR)
    args = [z] * 12
    if has_vres:
        in_specs.append(tok)
        args.append(vfirst)
    in_specs += [const((3, d)), const((1, d)), const((1, d)), const((1, d)), const((2, d)), const((2, d)),
                 const((1, d)), const(w2cat.shape)]
    args += [mu_rkv, k_k.reshape(1, d), k_a.reshape(1, d), r_k.reshape(1, d), w0, a0, v0.reshape(1, d), w2cat]
    one = jax.ShapeDtypeStruct((bsz, t, d), F32)
    two = jax.ShapeDtypeStruct((2, bsz, t, d), F32)
    return pl.pallas_call(
        functools.partial(_rwkv_prep_kernel, tb=tb, nt=nt, nctx=nctx, has_vres=has_vres),
        out_shape=(one, one, one, one, one, two, two, two),
        grid=(bsz, nt),
        in_specs=in_specs,
        out_specs=(tok, tok, tok, tok, tok, tok2, tok2, tok2),
        compiler_params=_cparams(("parallel", "parallel")),
        name="rwkv_prep",
    )(*args)


def _rwkv_scan_kernel(r_ref, v_ref, kk_ref, lw_ref, kd_ref, bb_ref, y_ref, s_ref):
    d = pl.program_id(0)
    i = pl.program_id(1)
    bsz, n = r_ref.shape[0], r_ref.shape[1]
    rev = d == 1

    @pl.when(i == 0)
    def _():
        s_ref[...] = jnp.zeros_like(s_ref)

    ti = lax.broadcasted_iota(jnp.int32, (n, n), 0)
    tj = lax.broadcasted_iota(jnp.int32, (n, n), 1)
    order = jnp.where(rev, tj - ti, ti - tj)
    tri = jnp.where(order >= 0, 1.0, 0.0).astype(BF16)
    wide = []
    gls = []
    for b in range(bsz):
        lw = lw_ref[0, b]
        h1 = lw.astype(BF16)
        r1 = lw - h1.astype(F32)
        h2 = r1.astype(BF16)
        h3 = (r1 - h2.astype(F32)).astype(BF16)
        c = _dot(tri, h1) + _dot(tri, h2) + _dot(tri, h3)
        c_last = jnp.where(rev, c[0:1], c[n - 1:n])
        kd = kd_ref[0, b]
        bb = bb_ref[0, b]
        enc = jnp.exp(-c)
        el = jnp.exp(c_last - c)
        wide.append([t.astype(BF16) for t in (-kk_ref[b] * jnp.exp(c - lw), r_ref[b] * jnp.exp(c), bb * enc, kd * enc,
                                              v_ref[b], bb * el, kd * el)])
        gls.append(jnp.exp(c_last))

    lane = lax.broadcasted_iota(jnp.int32, (n, PAIR), 1)
    first = lane < HEAD_DIM
    si = lax.broadcasted_iota(jnp.int32, (2 * n, 2 * n), 0) & (n - 1)
    sj = lax.broadcasted_iota(jnp.int32, (2 * n, 2 * n), 1) & (n - 1)
    row2 = lax.broadcasted_iota(jnp.int32, (2 * n, 2 * n), 0)
    col2 = lax.broadcasted_iota(jnp.int32, (2 * n, 2 * n), 1)
    qi = jnp.where(rev, n - 1 - si, si)
    qj = jnp.where(rev, n - 1 - sj, sj)
    strict = qi > qj
    incl = qi >= qj
    eye = jnp.where(row2 == col2, 1.0, 0.0)
    levels = []
    blk = 1
    while blk < n:
        bi = qi // blk
        bj = qj // blk
        levels.append((bi // 2 == bj // 2) & (bi % 2 == 1) & (bj % 2 == 0))
        blk *= 2

    zero = jnp.zeros((n, PAIR), BF16)

    def stack(x, sl):
        return jnp.concatenate([jnp.where(first, x[:, sl], zero), jnp.where(first, zero, x[:, sl])], axis=0)

    zero2 = jnp.zeros((2 * n, 2 * n), BF16)
    chains = [(b, p) for b in range(bsz) for p in range(N_HEADS // 2)]
    for g0 in range(0, len(chains), SCAN_GROUP):
        grp = chains[g0:g0 + SCAN_GROUP]
        sls = [slice(p * PAIR, (p + 1) * PAIR) for _, p in grp]
        opnd = [[stack(t, sl) for t in wide[b]] for (b, _), sl in zip(grp, sls)]
        s2 = [s_ref[b, p] for b, p in grp]
        s2b = [s.astype(BF16) for s in s2]
        aa = [_dot_nt(jnp.concatenate([am, rm], axis=0), jnp.concatenate([bm, km], axis=0))
              for am, rm, bm, km, _, _, _ in opnd]
        a_ab = [jnp.where(strict, a[:2 * n, :2 * n], 0.0) for a in aa]
        a_abb = [a.astype(BF16) for a in a_ab]
        a_ak = [jnp.where(strict, a[:2 * n, 2 * n:], 0.0).astype(BF16) for a in aa]
        a_r = [jnp.concatenate([jnp.where(incl, a[2 * n:, :2 * n], 0.0), jnp.where(incl, a[2 * n:, 2 * n:], 0.0)],
                               axis=1).astype(BF16) for a in aa]
        x = [_dot_nt(o[0], sb) + _dot(ak, o[4]) for o, sb, ak in zip(opnd, s2b, a_ak)]
        tinv = [eye + jnp.where(levels[0], a, 0.0) for a in a_ab]
        for m in levels[1:]:
            tb = [t.astype(BF16) for t in tinv]
            half = [_dot(t, jnp.where(m, a, zero2)) for t, a in zip(tb, a_abb)]
            tinv = [t + _dot(h.astype(BF16), q) for t, h, q in zip(tinv, half, tb)]
        uv = [jnp.concatenate([_dot(t.astype(BF16), xx.astype(BF16)).astype(BF16), o[4]], axis=0)
              for t, xx, o in zip(tinv, x, opnd)]
        for j, (b, p) in enumerate(grp):
            rm, blm, klm = opnd[j][1], opnd[j][5], opnd[j][6]
            o = _dot_nt(rm, s2b[j]) + _dot(a_r[j], uv[j])
            y_ref[0, b, :, sls[j]] = o[:n] + o[n:]
            s_ref[b, p] = s2[j] * gls[b][:, sls[j]] + _dot_tn(uv[j], jnp.concatenate([blm, klm], axis=0))


def _rwkv_scan(r, v, kk, lw, kd, bb, n_ctx):
    bsz, t, d = r.shape
    nc = t // CHUNK
    nctx = n_ctx // CHUNK

    def blk_of(dd, i):
        return _seq_block(dd, i, nc, nctx)

    tok = pl.BlockSpec((bsz, CHUNK, d), lambda dd, i: (0, blk_of(dd, i), 0))
    tok2 = pl.BlockSpec((1, bsz, CHUNK, d), lambda dd, i: (dd, 0, blk_of(dd, i), 0))
    return pl.pallas_call(
        _rwkv_scan_kernel,
        out_shape=jax.ShapeDtypeStruct((2, bsz, t, d), F32),
        grid=(2, nc),
        in_specs=[tok, tok, tok, tok2, tok2, tok2],
        out_specs=tok2,
        scratch_shapes=[pltpu.VMEM((bsz, N_HEADS // 2, PAIR, PAIR), F32)],
        compiler_params=_cparams(("arbitrary", "arbitrary")),
        name="rwkv_scan",
    )(r, v, kk, lw, kd, bb)


def _gelu_tanh(x):
    return 0.5 * x * (1.0 + jnp.tanh(math.sqrt(2.0 / math.pi) * (x + 0.044715 * (x * x * x))))


def _merge_kernel(x_ref, ya_ref, hl_ref, ug_ref, yw_ref, bonus_ref, g_ref, ma_ref, mb_ref, mc_ref,
                  mod_ref, lnw_ref, lnb_ref, wa_ref, wb_ref, wc_ref, wo_ref, o_ref, *, tm, n_ctx):
    i = pl.program_id(1)
    yb = ((hl_ref[0, 0] + hl_ref[1, 0]) * _gelu_tanh(ug_ref[0])).astype(BF16)
    y = yw_ref[0, 0] + yw_ref[1, 0]
    inv = 1.0 / HEAD_DIM
    mean = _seg_sum64(y) * inv
    yc = y - mean
    var = _seg_sum64(yc * yc) * inv
    yn = yc * lax.rsqrt(var + LNX_EPS) * lnw_ref[...] + lnb_ref[...]
    oc = ((yn + bonus_ref[0]) * g_ref[0]).astype(BF16)
    mixed = (_sigmoid(ma_ref[0]) * _dot(ya_ref[0], wa_ref[...])
             + _sigmoid(mb_ref[0]) * _dot(yb, wb_ref[...])
             + _sigmoid(mc_ref[0]) * _dot(oc, wc_ref[...]))
    yo = _dot(mixed.astype(BF16), wo_ref[...])
    rows = i * tm + lax.broadcasted_iota(jnp.int32, (tm, 1), 0)
    o_ref[0] = x_ref[0] + _row_select(mod_ref, 2, rows, n_ctx) * yo


def _merge(x, ynat, hlru, z, yrw, bonus, g, mods, ln_w, ln_b, wa, wb, wc, wo, n_ctx):
    bsz, t, d = x.shape
    tm = 256
    tok = pl.BlockSpec((1, tm, d), lambda b, i: (b, i, 0))
    tok2 = pl.BlockSpec((2, 1, tm, d), lambda b, i: (0, b, i, 0))
    zspec = lambda gidx: pl.BlockSpec((1, tm, d), lambda b, i: (b, i, gidx))
    vec = pl.BlockSpec((1, d), lambda b, i: (0, 0))
    wspec = pl.BlockSpec((d, d), lambda b, i: (0, 0))
    return pl.pallas_call(
        functools.partial(_merge_kernel, tm=tm, n_ctx=n_ctx),
        out_shape=jax.ShapeDtypeStruct((bsz, t, d), F32),
        grid=(bsz, t // tm),
        in_specs=[tok, tok, tok2, zspec(G_GB), tok2, tok, tok, zspec(G_MA), zspec(G_MB), zspec(G_MC),
                  pl.BlockSpec((1, 2, 6, d), lambda b, i: (b, 0, 0, 0)), vec, vec,
                  wspec, wspec, wspec, wspec],
        out_specs=tok,
        compiler_params=_cparams(("parallel", "parallel")),
        name="merge",
    )(x, ynat, hlru, z, yrw, bonus, g, z, z, z, mods, ln_w.reshape(1, d), ln_b.reshape(1, d), wa, wb, wc, wo)


def _mlp_kernel(x_ref, g_ref, mod_ref, w1_ref, w2_ref, fg_ref, o_ref, hb_ref, acc_ref, *, tm, n_ctx, final):
    i = pl.program_id(1)
    k = pl.program_id(2)
    rows = i * tm + lax.broadcasted_iota(jnp.int32, (tm, 1), 0)

    @pl.when(k == 0)
    def _():
        x = x_ref[0]
        ms = jnp.mean(x * x, axis=-1, keepdims=True)
        y = x * lax.rsqrt(ms + NORM_EPS) * g_ref[...]
        sh = _row_select(mod_ref, 3, rows, n_ctx)
        sc = _row_select(mod_ref, 4, rows, n_ctx)
        hb_ref[...] = (y * (1.0 + sc) + sh).astype(BF16)
        acc_ref[...] = jnp.zeros_like(acc_ref)

    a = jnp.maximum(_dot(hb_ref[...], w1_ref[...]), 0.0)
    acc_ref[...] += _dot((a * a).astype(BF16), w2_ref[...])

    @pl.when(k == pl.num_programs(2) - 1)
    def _():
        xn = x_ref[0] + _row_select(mod_ref, 5, rows, n_ctx) * acc_ref[...]
        if final:
            ms = jnp.mean(xn * xn, axis=-1, keepdims=True)
            xn = xn * lax.rsqrt(ms + NORM_EPS) * fg_ref[...]
        o_ref[0] = xn


def _mlp(x, g, mods, w1, w2, final_g, n_ctx, final):
    bsz, t, d = x.shape
    dff = w1.shape[1]
    tm = _pick(t, (768, 256))
    tk = 1024
    vec = pl.BlockSpec((1, d), lambda b, i, k: (0, 0))
    tok = pl.BlockSpec((1, tm, d), lambda b, i, k: (b, i, 0))
    return pl.pallas_call(
        functools.partial(_mlp_kernel, tm=tm, n_ctx=n_ctx, final=final),
        out_shape=jax.ShapeDtypeStruct((bsz, t, d), F32),
        grid=(bsz, t // tm, dff // tk),
        in_specs=[tok, vec, pl.BlockSpec((1, 2, 6, d), lambda b, i, k: (b, 0, 0, 0)),
                  pl.BlockSpec((d, tk), lambda b, i, k: (0, k)),
                  pl.BlockSpec((tk, d), lambda b, i, k: (k, 0)), vec],
        out_specs=tok,
        scratch_shapes=[pltpu.VMEM((tm, d), BF16), pltpu.VMEM((tm, d), F32)],
        compiler_params=_cparams(("parallel", "parallel", "arbitrary")),
        name="mlp",
    )(x, g.reshape(1, d), mods, w1, w2, final_g.reshape(1, d))


def _lowrank_weights(mu_h, w1, a1, g1, v1, mu_v):
    d = mu_h.shape[-1]
    blocks = [(w1[0], mu_h[0]), (w1[1], mu_h[0]), (a1[0], mu_h[1]), (a1[1], mu_h[1]), (g1, mu_h[2])]
    if v1 is not None:
        blocks.append((v1, mu_v))
    p = jnp.concatenate([w * (1.0 - m)[:, None] for w, m in blocks], axis=1)
    q = jnp.concatenate([w * m[:, None] for w, m in blocks], axis=1)
    pad = lambda a: jnp.pad(a, ((0, 0), (0, LR_W - a.shape[1])))
    return jnp.concatenate([pad(p), pad(q)], axis=1)


def _second_stage_weights(w2, a2, g2, v2):
    d = w2.shape[-1]
    def place(w, off):
        return jnp.zeros((256, d), F32).at[off:off + w.shape[0]].set(w)
    v2p = place(v2, 160) if v2 is not None else jnp.zeros((256, d), F32)
    return jnp.stack([place(w2[0], 0), place(w2[1], 64), place(a2[0], 128), place(a2[1], 192),
                      place(g2, 0), v2p]).astype(BF16)


def kernel(x, c, ctx, c_ctx, w_ada, b_ada, norm1_g, norm2_g, w_in, rpb, conv_w, conv_b, rg_wa, rg_ba, rg_wx, rg_bx, rg_lam, rw_mu_rkv, rw_mu_h, rw_w0, rw_w1, rw_w2, rw_a0, rw_a1, rw_a2, rw_g1, rw_g2, rw_k_k, rw_k_a, rw_r_k, rw_ln_w, rw_ln_b, vres_v0, vres_v1, vres_v2, vres_mu, w_br_a, w_br_b, w_br_c, w_out, w_ff1, w_ff2, final_g):
    bsz, seq, d = x.shape
    n_ctx = ctx.shape[1]
    depth = w_in.shape[0]
    t = n_ctx + seq
    xs = jnp.concatenate([ctx, x], axis=1)
    cvec = jnp.zeros((8, d), F32).at[:bsz].set(c).at[bsz].set(c_ctx)
    tables = _rope_tables(t, n_ctx)
    vfirst = None
    for i in range(depth):
        last = i == depth - 1
        mod = _ada(cvec, w_ada[i], b_ada[i])
        mods = jnp.stack([jnp.broadcast_to(mod[bsz].reshape(1, 6, d), (bsz, 6, d)),
                          mod[:bsz].reshape(bsz, 6, d)], axis=1)
        if i == 0:
            lr_w = _lowrank_weights(rw_mu_h[i], rw_w1[i], rw_a1[i], rw_g1[i], None, None)
            w2cat = _second_stage_weights(rw_w2[i], rw_a2[i], rw_g2[i], None)
            v0 = jnp.zeros((d,), F32)
        else:
            lr_w = _lowrank_weights(rw_mu_h[i], rw_w1[i], rw_a1[i], rw_g1[i], vres_v1[i - 1], vres_mu[i - 1])
            w2cat = _second_stage_weights(rw_w2[i], rw_a2[i], rw_g2[i], vres_v2[i - 1])
            v0 = vres_v0[i - 1]
        w_all = jnp.concatenate([w_in[i], lr_w], axis=1).astype(BF16)
        z, qr, kr, vb = _inproj(xs, norm1_g[i], mods, w_all, tables, n_ctx)
        ynat = _natten(qr, kr, vb, _natten_table(rpb[i], n_ctx), n_ctx)
        hlru = _rglru(z, conv_w[i], conv_b[i],
                      _block_diag_groups(rg_wa[i], 4).astype(BF16), rg_ba[i],
                      _block_diag_groups(rg_wx[i], 4).astype(BF16), rg_bx[i], rg_lam[i], n_ctx)
        r, v, kk, g, bonus, lw, kd, bb = _rwkv_prep(
            z, vfirst, rw_mu_rkv[i], rw_k_k[i], rw_k_a[i], rw_r_k[i].reshape(-1), rw_w0[i], rw_a0[i], v0,
            w2cat, n_ctx)
        if i == 0:
            vfirst = v
        yrw = _rwkv_scan(r, v, kk, lw, kd, bb, n_ctx)
        xs = _merge(xs, ynat, hlru, z, yrw, bonus, g, mods, rw_ln_w[i], rw_ln_b[i],
                    w_br_a[i].astype(BF16), w_br_b[i].astype(BF16), w_br_c[i].astype(BF16),
                    w_out[i].astype(BF16), n_ctx)
        xs = _mlp(xs, norm2_g[i], mods, w_ff1[i].astype(BF16), w_ff2[i].astype(BF16), final_g, n_ctx, last)
    return xs[:, n_ctx:]
```

```python
import functools
import math

import jax
import jax.numpy as jnp
import numpy as np
from jax import lax
from jax.experimental import pallas as pl
from jax.experimental.pallas import tpu as pltpu

F32 = jnp.float32
BF16 = jnp.bfloat16

D_MODEL = 1024
GRID_W = 64
N_HEADS = 16
HEAD_DIM = 64
WIN_H = 8
WIN_W = 16
ROPE_BASE = 10000.0
NB_B = 16
BS_B = 64
RG_C = 8.0
LNX_EPS = 64e-5
NORM_EPS = 1e-6
NEG_INF = -1e30
MASKED = -1e30
N_ATT = 3
G_XB, G_GB, G_R, G_K, G_V, G_MA, G_MB, G_MC, G_LR = range(9)
LR_W = 512
CHUNK = 64
VMEM_LIMIT = 56 * 1024 * 1024
PAIR = 2 * HEAD_DIM
SCAN_GROUP = 16


def _cparams(sem):
    return pltpu.CompilerParams(dimension_semantics=sem, vmem_limit_bytes=VMEM_LIMIT)


def _pick(n, cands):
    for c in cands:
        if n % c == 0:
            return c
    raise ValueError(f"no tile for {n}")


def _dot(a, b):
    return jnp.dot(a, b, preferred_element_type=F32)


def _dot_nt(a, b):
    return lax.dot_general(a, b, (((1,), (1,)), ((), ())), preferred_element_type=F32)


def _dot_tn(a, b):
    return lax.dot_general(a, b, (((0,), (0,)), ((), ())), preferred_element_type=F32)


def _sigmoid(x):
    return 1.0 / (1.0 + jnp.exp(-x))


def _row_select(mod_ref, idx, rows, n_ctx):
    return jnp.where(rows < n_ctx, mod_ref[0, 0, idx:idx + 1, :], mod_ref[0, 1, idx:idx + 1, :])


def _seg_sum64(x):
    h1 = x.astype(BF16)
    r1 = x - h1.astype(F32)
    h2 = r1.astype(BF16)
    h3 = (r1 - h2.astype(F32)).astype(BF16)
    gi = lax.broadcasted_iota(jnp.int32, (3 * PAIR, PAIR), 0) % PAIR
    gj = lax.broadcasted_iota(jnp.int32, (3 * PAIR, PAIR), 1)
    ones = jnp.where(gi // HEAD_DIM == gj // HEAD_DIM, 1.0, 0.0).astype(BF16)
    out = []
    for p in range(x.shape[-1] // PAIR):
        sl = slice(p * PAIR, (p + 1) * PAIR)
        out.append(_dot(jnp.concatenate([h1[:, sl], h2[:, sl], h3[:, sl]], axis=1), ones))
    return jnp.concatenate(out, axis=1)


def _seq_block(d, i, nt, nctx):
    rev = jnp.where(i < nctx, nctx - 1 - i, nt - 1 - (i - nctx))
    return jnp.where(d == 0, i, rev)


def _ada_kernel(a_ref, w_ref, b_ref, o_ref):
    a = a_ref[...]
    s = a * _sigmoid(a)
    o_ref[...] = _dot(s.astype(BF16), w_ref[...].astype(BF16)) + b_ref[...]


def _ada(cvec, w, b):
    m, k = cvec.shape
    n = w.shape[1]
    tn = _pick(n, (1536, 1024, 512, 128))
    return pl.pallas_call(
        _ada_kernel,
        out_shape=jax.ShapeDtypeStruct((m, n), F32),
        grid=(n // tn,),
        in_specs=[pl.BlockSpec((m, k), lambda j: (0, 0)),
                  pl.BlockSpec((k, tn), lambda j: (0, j)),
                  pl.BlockSpec((1, tn), lambda j: (0, j))],
        out_specs=pl.BlockSpec((m, tn), lambda j: (0, j)),
        compiler_params=_cparams(("arbitrary",)),
        name="ada",
    )(cvec, w, b.reshape(1, n))


def _inproj_kernel(x_ref, g_ref, mod_ref, w_ref, c_ref, s1_ref, s2_ref, z_ref, q_ref, k_ref, v_ref, hb_ref,
                   *, tm, n_ctx):
    i = pl.program_id(1)
    j = pl.program_id(2)

    @pl.when(j == 0)
    def _():
        x = x_ref[0]
        ms = jnp.mean(x * x, axis=-1, keepdims=True)
        y = x * lax.rsqrt(ms + NORM_EPS) * g_ref[...]
        rows = i * tm + lax.broadcasted_iota(jnp.int32, (tm, 1), 0)
        sh = _row_select(mod_ref, 0, rows, n_ctx)
        sc = _row_select(mod_ref, 1, rows, n_ctx)
        hb_ref[...] = (y * (1.0 + sc) + sh).astype(BF16)

    acc = _dot(hb_ref[...], w_ref[...])
    d = acc.shape[-1]
    quarter = HEAD_DIM // 4

    def rope(x):
        reps = d // c_ref.shape[-1]
        cos = jnp.concatenate([c_ref[...]] * reps, axis=1)
        s1 = jnp.concatenate([s1_ref[...]] * reps, axis=1)
        s2 = jnp.concatenate([s2_ref[...]] * reps, axis=1)
        return x * cos + pltpu.roll(x, d - quarter, 1) * s1 + pltpu.roll(x, quarter, 1) * s2

    @pl.when(j == 0)
    def _():
        q_ref[0] = (rope(acc) * (HEAD_DIM ** -0.5)).astype(BF16)

    @pl.when(j == 1)
    def _():
        k_ref[0] = rope(acc).astype(BF16)

    @pl.when(j == 2)
    def _():
        v_ref[0] = acc.astype(BF16)

    @pl.when(j >= N_ATT)
    def _():
        z_ref[0] = acc


def _inproj(x, g, mods, w, tables, n_ctx):
    bsz, t, d = x.shape
    n = w.shape[1]
    tm = _pick(t, (768, 256))
    tn = d
    tspec = pl.BlockSpec((tm, PAIR), lambda b, i, j: (i, 0))
    aspec = pl.BlockSpec((1, tm, d), lambda b, i, j: (b, i, 0))
    ashape = jax.ShapeDtypeStruct((bsz, t, d), BF16)
    return pl.pallas_call(
        functools.partial(_inproj_kernel, tm=tm, n_ctx=n_ctx),
        out_shape=(jax.ShapeDtypeStruct((bsz, t, n - N_ATT * d), F32), ashape, ashape, ashape),
        grid=(bsz, t // tm, n // tn),
        in_specs=[pl.BlockSpec((1, tm, d), lambda b, i, j: (b, i, 0)),
                  pl.BlockSpec((1, d), lambda b, i, j: (0, 0)),
                  pl.BlockSpec((1, 2, 6, d), lambda b, i, j: (b, 0, 0, 0)),
                  pl.BlockSpec((d, tn), lambda b, i, j: (0, j)),
                  tspec, tspec, tspec],
        out_specs=(pl.BlockSpec((1, tm, tn), lambda b, i, j: (b, i, jnp.maximum(j - N_ATT, 0))),
                   aspec, aspec, aspec),
        scratch_shapes=[pltpu.VMEM((tm, d), BF16)],
        compiler_params=_cparams(("parallel", "parallel", "arbitrary")),
        name="inproj",
    )(x, g.reshape(1, d), mods, w, *tables)


def _rope_tables(t, n_ctx):
    nf = HEAD_DIM // 4
    inv_freq = ROPE_BASE ** (-jnp.arange(nf, dtype=F32) / nf)
    tok = jnp.arange(t) - n_ctx
    row = (tok // GRID_W).astype(F32)
    col = (tok % GRID_W).astype(F32)
    lane = np.arange(PAIR) % HEAD_DIM
    use_col = jnp.asarray(lane >= HEAD_DIM // 2)
    first = jnp.asarray((lane % (HEAD_DIM // 2)) < nf)
    pos = jnp.where(use_col[None, :], col[:, None], row[:, None])
    ang = pos * inv_freq[lane % nf][None, :]
    is_lat = (tok >= 0)[:, None]
    cos = jnp.where(is_lat, jnp.cos(ang), 1.0)
    sin = jnp.where(is_lat, jnp.sin(ang), 0.0)
    return cos, jnp.where(first[None, :], -sin, 0.0), jnp.where(first[None, :], 0.0, sin)


def _natten_kernel(q_ref, *refs):
    k_refs = refs[0:WIN_H]
    v_refs = refs[WIN_H:2 * WIN_H]
    kc_ref, vc_ref = refs[2 * WIN_H:2 * WIN_H + 2]
    tb_refs = refs[2 * WIN_H + 2:-1]
    o_ref = refs[-1]
    nq = q_ref.shape[1]
    nloc = WIN_H * GRID_W
    lane = lax.broadcasted_iota(jnp.int32, (nq, PAIR), 1)
    first = lane < HEAD_DIM
    zero = jnp.zeros((nq, PAIR), BF16)
    sls = [slice(p * PAIR, (p + 1) * PAIR) for p in range(N_HEADS // 2)]
    qs = [jnp.concatenate([jnp.where(first, q_ref[0, :, sl], zero), jnp.where(first, zero, q_ref[0, :, sl])], axis=0)
          for sl in sls]
    ks = [jnp.concatenate([r[0, :, sl] for r in k_refs] + [kc_ref[0, :, sl]], axis=0) for sl in sls]
    scores = [_dot_nt(q, k) for q, k in zip(qs, ks)]
    probs = []
    for p, s in enumerate(scores):
        tb = jnp.concatenate([t[0, p] for t in tb_refs], axis=1)
        sl_ = jnp.where(tb > 0.5 * MASKED, s[:, :nloc] + tb, NEG_INF)
        sc_ = s[:, nloc:]
        m = jnp.maximum(jnp.max(sl_, axis=-1, keepdims=True), jnp.max(sc_, axis=-1, keepdims=True))
        el_ = jnp.exp(sl_ - m)
        ec_ = jnp.exp(sc_ - m)
        inv = 1.0 / (jnp.sum(el_, axis=-1, keepdims=True) + jnp.sum(ec_, axis=-1, keepdims=True))
        probs.append(jnp.concatenate([(el_ * inv).astype(BF16), (ec_ * inv).astype(BF16)], axis=1))
    vs = [jnp.concatenate([r[0, :, sl] for r in v_refs] + [vc_ref[0, :, sl]], axis=0) for sl in sls]
    outs = [_dot(pr, v) for pr, v in zip(probs, vs)]
    for sl, o in zip(sls, outs):
        o_ref[0, :, sl] = jnp.where(first, o[:nq], o[nq:]).astype(BF16)


def _natten_table(rpb):
    col = np.arange(GRID_W)
    c0 = np.clip(col - WIN_W // 2, 0, GRID_W - WIN_W)
    in_win = (col[None, :] >= c0[:, None]) & (col[None, :] < c0[:, None] + WIN_W)
    dc_idx = np.clip(col[None, :] - col[:, None] + WIN_W - 1, 0, 2 * WIN_W - 2)
    b = jnp.where(jnp.asarray(in_win)[None, None], rpb.astype(F32)[:, :, dc_idx], MASKED)
    b = jnp.transpose(b, (1, 0, 2, 3)).reshape(2 * WIN_H - 1, N_HEADS // 2, 2 * GRID_W, GRID_W)
    two = jnp.concatenate([b[:-1], b[1:]], axis=-1)
    return jnp.concatenate([two, jnp.full((1,) + two.shape[1:], MASKED, F32)], axis=0)


def _natten(qr, kr, vb, table, n_ctx):
    bsz, t, d = qr.shape
    nqc = n_ctx // GRID_W
    rows = (t - n_ctx) // GRID_W
    assert rows >= WIN_H
    masked_entry = table.shape[0] - 1

    def r0(s):
        return jnp.clip(s - nqc - WIN_H // 2, 0, rows - WIN_H)

    def entry(s, c):
        return jnp.where(s < nqc, masked_entry, 2 * c + WIN_H - 1 - (s - nqc - r0(s)))

    blk = (1, GRID_W, d)
    kv_specs = [pl.BlockSpec(blk, functools.partial(lambda b, s, i: (b, nqc + r0(s) + i, 0), i=i))
                for i in range(WIN_H)]
    ctx_spec = pl.BlockSpec((1, n_ctx, d), lambda b, s: (b, 0, 0))
    tb_specs = [pl.BlockSpec((1,) + table.shape[1:], functools.partial(lambda b, s, c: (entry(s, c), 0, 0, 0), c=c))
                for c in range(WIN_H // 2)]
    return pl.pallas_call(
        _natten_kernel,
        out_shape=jax.ShapeDtypeStruct((bsz, t, d), BF16),
        grid=(bsz, t // GRID_W),
        in_specs=[pl.BlockSpec(blk, lambda b, s: (b, s, 0))] + kv_specs + kv_specs + [ctx_spec, ctx_spec] + tb_specs,
        out_specs=pl.BlockSpec(blk, lambda b, s: (b, s, 0)),
        compiler_params=_cparams(("parallel", "arbitrary")),
        name="natten",
    )(qr, *([kr] * WIN_H), *([vb] * WIN_H), kr, vb, *([table] * (WIN_H // 2)))


def _block_diag_groups(w, per):
    two, nb, bs, _ = w.shape
    w = w.reshape(two, nb // per, per, bs, bs)
    eye = jnp.eye(per, dtype=w.dtype)
    return jnp.einsum('dgaij,ab->dgaibj', w, eye).reshape(two, nb // per, per * bs, per * bs)


def _rwkv_prep_kernel(*refs, tb, nt, nctx, has_vres):
    (zr, zr_p, zr_n, zk, zk_p, zk_n, zv, zv_p, zv_n, zl, zl_p, zl_n) = refs[:12]
    refs = refs[12:]
    if has_vres:
        vf_ref, refs = refs[0], refs[1:]
    (mu_ref, kk_ref, ka_ref, rk_ref, w0_ref, a0_ref, v0_ref, w2_ref,
     r_o, v_o, kk_o, g_o, bonus_o, lw_o, kd_o, bb_o) = refs
    blk = pl.program_id(1)
    seq_first = (blk == 0) | (blk == nctx)
    seq_last = (blk == nctx - 1) | (blk == nt - 1)
    row = lax.broadcasted_iota(jnp.int32, (tb, 1), 0)

    def nbr_mean(main, prev_row, next_row):
        prev_row = jnp.where(seq_first, 0.0, prev_row)
        next_row = jnp.where(seq_last, 0.0, next_row)
        up = jnp.where(row == 0, prev_row, pltpu.roll(main, 1, 0))
        dn = jnp.where(row == tb - 1, next_row, pltpu.roll(main, tb - 1, 0))
        return 0.5 * (up + dn)

    def shifted(m_ref, p_ref, n_ref, mu):
        main = m_ref[0]
        return main + mu * (nbr_mean(main, p_ref[0, 7:8, :], n_ref[0, 0:1, :]) - main)

    mu = mu_ref[...]
    r = shifted(zr, zr_p, zr_n, mu[0:1])
    k = shifted(zk, zk_p, zk_n, mu[1:2])
    v = shifted(zv, zv_p, zv_n, mu[2:3])

    lr = zl[0, :, 0:LR_W] + nbr_mean(zl[0, :, LR_W:2 * LR_W], zl_p[0, 7:8, LR_W:2 * LR_W],
                                     zl_n[0, 0:1, LR_W:2 * LR_W])
    col = lax.broadcasted_iota(jnp.int32, (1, LR_W), 1)
    act = jnp.where(col < 128, jnp.tanh(lr), jnp.where((col >= 256) & (col < 416), _sigmoid(lr), lr))
    actb = act.astype(BF16)
    s0 = actb[:, 0:256]
    s1 = actb[:, 256:512]
    g_o[0] = _dot(s1, w2_ref[4])
    if has_vres:
        v = v + (vf_ref[0] - v) * _sigmoid(v0_ref[...] + _dot(s1, w2_ref[5]))
    r_o[0] = r
    v_o[0] = v
    kkr = k * kk_ref[...]
    kk = kkr / jnp.maximum(jnp.sqrt(_seg_sum64(kkr * kkr)), 1e-12)
    kk_o[0] = kk
    ksum = jnp.zeros_like(k)
    for dd in range(2):
        zw = w0_ref[dd:dd + 1, :] + _dot(s0, w2_ref[dd])
        lw_o[dd, 0] = -math.exp(-0.5) * _sigmoid(zw)
        a = _sigmoid(a0_ref[dd:dd + 1, :] + _dot(s0, w2_ref[2 + dd]))
        kd = k * (1.0 + (a - 1.0) * ka_ref[...])
        kd_o[dd, 0] = kd
        bb_o[dd, 0] = kk * a
        ksum = ksum + kd
    bonus_o[0] = _seg_sum64(r * ksum * rk_ref[...]) * v


def _rwkv_prep(z, vfirst, mu_rkv, k_k, k_a, r_k, w0, a0, v0, w2cat, n_ctx):
    bsz, t, _ = z.shape
    d = D_MODEL
    tb = 256
    nt = t // tb
    nctx = n_ctx // tb
    hb = tb // 8
    has_vres = vfirst is not None

    def triple(g):
        return [pl.BlockSpec((1, tb, d), lambda b, i: (b, i, g)),
                pl.BlockSpec((1, 8, d), lambda b, i: (b, jnp.maximum(i * hb - 1, 0), g)),
                pl.BlockSpec((1, 8, d), lambda b, i: (b, jnp.minimum((i + 1) * hb, t // 8 - 1), g))]

    tok = pl.BlockSpec((1, tb, d), lambda b, i: (b, i, 0))
    tok2 = pl.BlockSpec((2, 1, tb, d), lambda b, i: (0, b, i, 0))
    const = lambda shape: pl.BlockSpec(shape, lambda b, i: (0,) * len(shape))
    in_specs = triple(G_R) + triple(G_K) + triple(G_V) + triple(G_LR)
    args = [z] * 12
    if has_vres:
        in_specs.append(tok)
        args.append(vfirst)
    in_specs += [const((3, d)), const((1, d)), const((1, d)), const((1, d)), const((2, d)), const((2, d)),
                 const((1, d)), const(w2cat.shape)]
    args += [mu_rkv, k_k.reshape(1, d), k_a.reshape(1, d), r_k.reshape(1, d), w0, a0, v0.reshape(1, d), w2cat]
    one = jax.ShapeDtypeStruct((bsz, t, d), F32)
    two = jax.ShapeDtypeStruct((2, bsz, t, d), F32)
    return pl.pallas_call(
        functools.partial(_rwkv_prep_kernel, tb=tb, nt=nt, nctx=nctx, has_vres=has_vres),
        out_shape=(one, one, one, one, one, two, two, two),
        grid=(bsz, nt),
        in_specs=in_specs,
        out_specs=(tok, tok, tok, tok, tok, tok2, tok2, tok2),
        compiler_params=_cparams(("parallel", "parallel")),
        name="rwkv_prep",
    )(*args)


def _seq_kernel(r_ref, v_ref, kk_ref, lw_ref, kd_ref, bb_ref,
                u_ref, up_ref, un_ref, cw_ref, cb_ref, wa_ref, ba_ref, wx_ref, bx_ref, lam_ref,
                y_ref, hl_ref, s_ref, a_s, b_s, h_s, *, nc, nctx):
    d = pl.program_id(0)
    i = pl.program_id(1)
    bsz, n = r_ref.shape[0], r_ref.shape[1]
    rev = d == 1
    blk = _seq_block(d, i, nc, nctx)

    @pl.when(i == 0)
    def _():
        s_ref[...] = jnp.zeros_like(s_ref)
        h_s[...] = jnp.zeros_like(h_s)

    seq_first = (blk == 0) | (blk == nctx)
    seq_last = (blk == nctx - 1) | (blk == nc - 1)
    row = lax.broadcasted_iota(jnp.int32, (n, 1), 0)
    cw = cw_ref[...]
    nl = -lam_ref[0]
    softplus = jnp.maximum(nl, 0.0) + jnp.log(1.0 + jnp.exp(-jnp.abs(nl)))
    gw = wa_ref.shape[-1]
    for b in range(bsz):
        x = u_ref[b]
        prev = jnp.where(seq_first, 0.0, up_ref[b])
        nxt = jnp.where(seq_last, 0.0, un_ref[b])
        xm1 = jnp.where(row == 0, prev[7:8], pltpu.roll(x, 1, 0))
        xm2 = jnp.where(row == 0, prev[6:7], jnp.where(row == 1, prev[7:8], pltpu.roll(x, 2, 0)))
        xp1 = jnp.where(row == n - 1, nxt[0:1], pltpu.roll(x, n - 1, 0))
        u = cw[0:1] * xm2 + cw[1:2] * xm1 + cw[2:3] * x + cw[3:4] * xp1 + cb_ref[...]
        ub = u.astype(BF16)
        ngrp = u.shape[-1] // gw
        gr = jnp.concatenate([_dot(ub[:, g * gw:(g + 1) * gw], wa_ref[0, g]) for g in range(ngrp)], axis=1)
        gi = jnp.concatenate([_dot(ub[:, g * gw:(g + 1) * gw], wx_ref[0, g]) for g in range(ngrp)], axis=1)
        log_a = -RG_C * _sigmoid(gr + ba_ref[0]) * softplus
        a_s[b] = jnp.exp(log_a)
        b_s[b] = jnp.sqrt(1.0 - jnp.exp(2.0 * log_a)) * (_sigmoid(gi + bx_ref[0]) * u)

    hs = [h_s[b] for b in range(bsz)]
    for t in range(n):
        tt = jnp.where(rev, n - 1 - t, t)
        for b in range(bsz):
            hs[b] = a_s[b, pl.ds(tt, 1), :] * hs[b] + b_s[b, pl.ds(tt, 1), :]
            hl_ref[0, b, pl.ds(tt, 1), :] = hs[b]
    for b in range(bsz):
        h_s[b] = hs[b]

    ti = lax.broadcasted_iota(jnp.int32, (n, n), 0)
    tj = lax.broadcasted_iota(jnp.int32, (n, n), 1)
    order = jnp.where(rev, tj - ti, ti - tj)
    tri = jnp.where(order >= 0, 1.0, 0.0).astype(BF16)
    wide = []
    gls = []
    for b in range(bsz):
        lw = lw_ref[0, b]
        h1 = lw.astype(BF16)
        r1 = lw - h1.astype(F32)
        h2 = r1.astype(BF16)
        h3 = (r1 - h2.astype(F32)).astype(BF16)
        c = _dot(tri, h1) + _dot(tri, h2) + _dot(tri, h3)
        c_last = jnp.where(rev, c[0:1], c[n - 1:n])
        kd = kd_ref[0, b]
        bb = bb_ref[0, b]
        enc = jnp.exp(-c)
        el = jnp.exp(c_last - c)
        wide.append([t.astype(BF16) for t in (-kk_ref[b] * jnp.exp(c - lw), r_ref[b] * jnp.exp(c), bb * enc, kd * enc,
                                              v_ref[b], bb * el, kd * el)])
        gls.append(jnp.exp(c_last))

    lane = lax.broadcasted_iota(jnp.int32, (n, PAIR), 1)
    first = lane < HEAD_DIM
    si = lax.broadcasted_iota(jnp.int32, (2 * n, 2 * n), 0) & (n - 1)
    sj = lax.broadcasted_iota(jnp.int32, (2 * n, 2 * n), 1) & (n - 1)
    row2 = lax.broadcasted_iota(jnp.int32, (2 * n, 2 * n), 0)
    col2 = lax.broadcasted_iota(jnp.int32, (2 * n, 2 * n), 1)
    qi = jnp.where(rev, n - 1 - si, si)
    qj = jnp.where(rev, n - 1 - sj, sj)
    strict = qi > qj
    incl = qi >= qj
    eye = jnp.where(row2 == col2, 1.0, 0.0)
    levels = []
    blk = 1
    while blk < n:
        bi = qi // blk
        bj = qj // blk
        levels.append((bi // 2 == bj // 2) & (bi % 2 == 1) & (bj % 2 == 0))
        blk *= 2

    zero = jnp.zeros((n, PAIR), BF16)

    def stack(x, sl):
        return jnp.concatenate([jnp.where(first, x[:, sl], zero), jnp.where(first, zero, x[:, sl])], axis=0)

    zero2 = jnp.zeros((2 * n, 2 * n), BF16)
    chains = [(b, p) for b in range(bsz) for p in range(N_HEADS // 2)]
    for g0 in range(0, len(chains), SCAN_GROUP):
        grp = chains[g0:g0 + SCAN_GROUP]
        sls = [slice(p * PAIR, (p + 1) * PAIR) for _, p in grp]
        opnd = [[stack(t, sl) for t in wide[b]] for (b, _), sl in zip(grp, sls)]
        s2 = [s_ref[b, p] for b, p in grp]
        s2b = [s.astype(BF16) for s in s2]
        aa = [_dot_nt(jnp.concatenate([am, rm], axis=0), jnp.concatenate([bm, km], axis=0))
              for am, rm, bm, km, _, _, _ in opnd]
        a_ab = [jnp.where(strict, a[:2 * n, :2 * n], 0.0) for a in aa]
        a_abb = [a.astype(BF16) for a in a_ab]
        a_ak = [jnp.where(strict, a[:2 * n, 2 * n:], 0.0).astype(BF16) for a in aa]
        a_r = [jnp.concatenate([jnp.where(incl, a[2 * n:, :2 * n], 0.0), jnp.where(incl, a[2 * n:, 2 * n:], 0.0)],
                               axis=1).astype(BF16) for a in aa]
        x = [_dot_nt(o[0], sb) + _dot(ak, o[4]) for o, sb, ak in zip(opnd, s2b, a_ak)]
        tinv = [eye + jnp.where(levels[0], a, 0.0) for a in a_ab]
        for m in levels[1:]:
            tb = [t.astype(BF16) for t in tinv]
            half = [_dot(t, jnp.where(m, a, zero2)) for t, a in zip(tb, a_abb)]
            tinv = [t + _dot(h.astype(BF16), q) for t, h, q in zip(tinv, half, tb)]
        uv = [jnp.concatenate([_dot(t.astype(BF16), xx.astype(BF16)).astype(BF16), o[4]], axis=0)
              for t, xx, o in zip(tinv, x, opnd)]
        for j, (b, p) in enumerate(grp):
            rm, blm, klm = opnd[j][1], opnd[j][5], opnd[j][6]
            o = _dot_nt(rm, s2b[j]) + _dot(a_r[j], uv[j])
            y_ref[0, b, :, sls[j]] = o[:n] + o[n:]
            s_ref[b, p] = s2[j] * gls[b][:, sls[j]] + _dot_tn(uv[j], jnp.concatenate([blm, klm], axis=0))


def _seq_mixers(r, v, kk, lw, kd, bb, z, conv_w, conv_b, wa, ba, wx, bx, lam, n_ctx):
    bsz, t, d = r.shape
    nc = t // CHUNK
    nctx = n_ctx // CHUNK
    hb = CHUNK // 8
    gw = wa.shape[-1]

    def blk_of(dd, i):
        return _seq_block(dd, i, nc, nctx)

    tok = pl.BlockSpec((bsz, CHUNK, d), lambda dd, i: (0, blk_of(dd, i), 0))
    tok2 = pl.BlockSpec((1, bsz, CHUNK, d), lambda dd, i: (dd, 0, blk_of(dd, i), 0))
    vec = lambda: pl.BlockSpec((1, 1, d), lambda dd, i: (dd, 0, 0))
    wspec = lambda: pl.BlockSpec((1, d // gw, gw, gw), lambda dd, i: (dd, 0, 0, 0))
    out = jax.ShapeDtypeStruct((2, bsz, t, d), F32)
    return pl.pallas_call(
        functools.partial(_seq_kernel, nc=nc, nctx=nctx),
        out_shape=(out, out),
        grid=(2, nc),
        in_specs=[tok, tok, tok, tok2, tok2, tok2,
                  pl.BlockSpec((bsz, CHUNK, d), lambda dd, i: (0, blk_of(dd, i), G_XB)),
                  pl.BlockSpec((bsz, 8, d), lambda dd, i: (0, jnp.maximum(blk_of(dd, i) * hb - 1, 0), G_XB)),
                  pl.BlockSpec((bsz, 8, d), lambda dd, i: (0, jnp.minimum((blk_of(dd, i) + 1) * hb, t // 8 - 1), G_XB)),
                  pl.BlockSpec((4, d), lambda dd, i: (0, 0)),
                  pl.BlockSpec((1, d), lambda dd, i: (0, 0)),
                  wspec(), vec(), wspec(), vec(), vec()],
        out_specs=(tok2, tok2),
        scratch_shapes=[pltpu.VMEM((bsz, N_HEADS // 2, PAIR, PAIR), F32),
                        pltpu.VMEM((bsz, CHUNK, d), F32), pltpu.VMEM((bsz, CHUNK, d), F32),
                        pltpu.VMEM((bsz, 1, d), F32)],
        compiler_params=_cparams(("arbitrary", "arbitrary")),
        name="seq_mixers",
    )(r, v, kk, lw, kd, bb, z, z, z, conv_w, conv_b.reshape(1, d), wa, ba.reshape(2, 1, d), wx, bx.reshape(2, 1, d),
      lam.reshape(2, 1, d))


def _gelu_tanh(x):
    return 0.5 * x * (1.0 + jnp.tanh(math.sqrt(2.0 / math.pi) * (x + 0.044715 * (x * x * x))))


def _merge_kernel(x_ref, ya_ref, hl_ref, ug_ref, yw_ref, bonus_ref, g_ref, ma_ref, mb_ref, mc_ref,
                  mod_ref, lnw_ref, lnb_ref, wa_ref, wb_ref, wc_ref, wo_ref, o_ref, *, tm, n_ctx):
    i = pl.program_id(1)
    yb = ((hl_ref[0, 0] + hl_ref[1, 0]) * _gelu_tanh(ug_ref[0])).astype(BF16)
    y = yw_ref[0, 0] + yw_ref[1, 0]
    inv = 1.0 / HEAD_DIM
    mean = _seg_sum64(y) * inv
    yc = y - mean
    var = _seg_sum64(yc * yc) * inv
    yn = yc * lax.rsqrt(var + LNX_EPS) * lnw_ref[...] + lnb_ref[...]
    oc = ((yn + bonus_ref[0]) * g_ref[0]).astype(BF16)
    mixed = (_sigmoid(ma_ref[0]) * _dot(ya_ref[0], wa_ref[...])
             + _sigmoid(mb_ref[0]) * _dot(yb, wb_ref[...])
             + _sigmoid(mc_ref[0]) * _dot(oc, wc_ref[...]))
    yo = _dot(mixed.astype(BF16), wo_ref[...])
    rows = i * tm + lax.broadcasted_iota(jnp.int32, (tm, 1), 0)
    o_ref[0] = x_ref[0] + _row_select(mod_ref, 2, rows, n_ctx) * yo


def _merge(x, ynat, hlru, z, yrw, bonus, g, mods, ln_w, ln_b, wa, wb, wc, wo, n_ctx):
    bsz, t, d = x.shape
    tm = 256
    tok = pl.BlockSpec((1, tm, d), lambda b, i: (b, i, 0))
    tok2 = pl.BlockSpec((2, 1, tm, d), lambda b, i: (0, b, i, 0))
    zspec = lambda gidx: pl.BlockSpec((1, tm, d), lambda b, i: (b, i, gidx))
    vec = pl.BlockSpec((1, d), lambda b, i: (0, 0))
    wspec = pl.BlockSpec((d, d), lambda b, i: (0, 0))
    return pl.pallas_call(
        functools.partial(_merge_kernel, tm=tm, n_ctx=n_ctx),
        out_shape=jax.ShapeDtypeStruct((bsz, t, d), F32),
        grid=(bsz, t // tm),
        in_specs=[tok, tok, tok2, zspec(G_GB), tok2, tok, tok, zspec(G_MA), zspec(G_MB), zspec(G_MC),
                  pl.BlockSpec((1, 2, 6, d), lambda b, i: (b, 0, 0, 0)), vec, vec,
                  wspec, wspec, wspec, wspec],
        out_specs=tok,
        compiler_params=_cparams(("parallel", "parallel")),
        name="merge",
    )(x, ynat, hlru, z, yrw, bonus, g, z, z, z, mods, ln_w.reshape(1, d), ln_b.reshape(1, d), wa, wb, wc, wo)


def _mlp_kernel(x_ref, g_ref, mod_ref, w1_ref, w2_ref, fg_ref, o_ref, hb_ref, acc_ref, *, tm, n_ctx, final):
    i = pl.program_id(1)
    k = pl.program_id(2)
    rows = i * tm + lax.broadcasted_iota(jnp.int32, (tm, 1), 0)

    @pl.when(k == 0)
    def _():
        x = x_ref[0]
        ms = jnp.mean(x * x, axis=-1, keepdims=True)
        y = x * lax.rsqrt(ms + NORM_EPS) * g_ref[...]
        sh = _row_select(mod_ref, 3, rows, n_ctx)
        sc = _row_select(mod_ref, 4, rows, n_ctx)
        hb_ref[...] = (y * (1.0 + sc) + sh).astype(BF16)
        acc_ref[...] = jnp.zeros_like(acc_ref)

    a = jnp.maximum(_dot(hb_ref[...], w1_ref[...]), 0.0)
    acc_ref[...] += _dot((a * a).astype(BF16), w2_ref[...])

    @pl.when(k == pl.num_programs(2) - 1)
    def _():
        xn = x_ref[0] + _row_select(mod_ref, 5, rows, n_ctx) * acc_ref[...]
        if final:
            ms = jnp.mean(xn * xn, axis=-1, keepdims=True)
            xn = xn * lax.rsqrt(ms + NORM_EPS) * fg_ref[...]
        o_ref[0] = xn


def _mlp(x, g, mods, w1, w2, final_g, n_ctx, final):
    bsz, t, d = x.shape
    dff = w1.shape[1]
    tm = _pick(t, (768, 256))
    tk = 1024
    vec = pl.BlockSpec((1, d), lambda b, i, k: (0, 0))
    tok = pl.BlockSpec((1, tm, d), lambda b, i, k: (b, i, 0))
    return pl.pallas_call(
        functools.partial(_mlp_kernel, tm=tm, n_ctx=n_ctx, final=final),
        out_shape=jax.ShapeDtypeStruct((bsz, t, d), F32),
        grid=(bsz, t // tm, dff // tk),
        in_specs=[tok, vec, pl.BlockSpec((1, 2, 6, d), lambda b, i, k: (b, 0, 0, 0)),
                  pl.BlockSpec((d, tk), lambda b, i, k: (0, k)),
                  pl.BlockSpec((tk, d), lambda b, i, k: (k, 0)), vec],
        out_specs=tok,
        scratch_shapes=[pltpu.VMEM((tm, d), BF16), pltpu.VMEM((tm, d), F32)],
        compiler_params=_cparams(("parallel", "parallel", "arbitrary")),
        name="mlp",
    )(x, g.reshape(1, d), mods, w1, w2, final_g.reshape(1, d))


def _lowrank_weights(mu_h, w1, a1, g1, v1, mu_v):
    d = mu_h.shape[-1]
    blocks = [(w1[0], mu_h[0]), (w1[1], mu_h[0]), (a1[0], mu_h[1]), (a1[1], mu_h[1]), (g1, mu_h[2])]
    if v1 is not None:
        blocks.append((v1, mu_v))
    p = jnp.concatenate([w * (1.0 - m)[:, None] for w, m in blocks], axis=1)
    q = jnp.concatenate([w * m[:, None] for w, m in blocks], axis=1)
    pad = lambda a: jnp.pad(a, ((0, 0), (0, LR_W - a.shape[1])))
    return jnp.concatenate([pad(p), pad(q)], axis=1)


def _second_stage_weights(w2, a2, g2, v2):
    d = w2.shape[-1]
    def place(w, off):
        return jnp.zeros((256, d), F32).at[off:off + w.shape[0]].set(w)
    v2p = place(v2, 160) if v2 is not None else jnp.zeros((256, d), F32)
    return jnp.stack([place(w2[0], 0), place(w2[1], 64), place(a2[0], 128), place(a2[1], 192),
                      place(g2, 0), v2p]).astype(BF16)


def kernel(x, c, ctx, c_ctx, w_ada, b_ada, norm1_g, norm2_g, w_in, rpb, conv_w, conv_b, rg_wa, rg_ba, rg_wx, rg_bx, rg_lam, rw_mu_rkv, rw_mu_h, rw_w0, rw_w1, rw_w2, rw_a0, rw_a1, rw_a2, rw_g1, rw_g2, rw_k_k, rw_k_a, rw_r_k, rw_ln_w, rw_ln_b, vres_v0, vres_v1, vres_v2, vres_mu, w_br_a, w_br_b, w_br_c, w_out, w_ff1, w_ff2, final_g):
    bsz, seq, d = x.shape
    n_ctx = ctx.shape[1]
    depth = w_in.shape[0]
    t = n_ctx + seq
    xs = jnp.concatenate([ctx, x], axis=1)
    cvec = jnp.zeros((8, d), F32).at[:bsz].set(c).at[bsz].set(c_ctx)
    tables = _rope_tables(t, n_ctx)
    vfirst = None
    for i in range(depth):
        last = i == depth - 1
        mod = _ada(cvec, w_ada[i], b_ada[i])
        mods = jnp.stack([jnp.broadcast_to(mod[bsz].reshape(1, 6, d), (bsz, 6, d)),
                          mod[:bsz].reshape(bsz, 6, d)], axis=1)
        if i == 0:
            lr_w = _lowrank_weights(rw_mu_h[i], rw_w1[i], rw_a1[i], rw_g1[i], None, None)
            w2cat = _second_stage_weights(rw_w2[i], rw_a2[i], rw_g2[i], None)
            v0 = jnp.zeros((d,), F32)
        else:
            lr_w = _lowrank_weights(rw_mu_h[i], rw_w1[i], rw_a1[i], rw_g1[i], vres_v1[i - 1], vres_mu[i - 1])
            w2cat = _second_stage_weights(rw_w2[i], rw_a2[i], rw_g2[i], vres_v2[i - 1])
            v0 = vres_v0[i - 1]
        w_all = jnp.concatenate([w_in[i], lr_w], axis=1).astype(BF16)
        z, qr, kr, vb = _inproj(xs, norm1_g[i], mods, w_all, tables, n_ctx)
        ynat = _natten(qr, kr, vb, _natten_table(rpb[i]), n_ctx)
        r, v, kk, g, bonus, lw, kd, bb = _rwkv_prep(
            z, vfirst, rw_mu_rkv[i], rw_k_k[i], rw_k_a[i], rw_r_k[i].reshape(-1), rw_w0[i], rw_a0[i], v0,
            w2cat, n_ctx)
        if i == 0:
            vfirst = v
        yrw, hlru = _seq_mixers(r, v, kk, lw, kd, bb, z, conv_w[i], conv_b[i],
                                _block_diag_groups(rg_wa[i], 4).astype(BF16), rg_ba[i],
                                _block_diag_groups(rg_wx[i], 4).astype(BF16), rg_bx[i], rg_lam[i], n_ctx)
        xs = _merge(xs, ynat, hlru, z, yrw, bonus, g, mods, rw_ln_w[i], rw_ln_b[i],
                    w_br_a[i].astype(BF16), w_br_b[i].astype(BF16), w_br_c[i].astype(BF16),
                    w_out[i].astype(BF16), n_ctx)
        xs = _mlp(xs, norm2_g[i], mods, w_ff1[i].astype(BF16), w_ff2[i].astype(BF16), final_g, n_ctx, last)
    return xs[:, n_ctx:]
```

```python
import functools
import math

import jax
import jax.numpy as jnp
import numpy as np
from jax import lax
from jax.experimental import pallas as pl
from jax.experimental.pallas import tpu as pltpu

F32 = jnp.float32
BF16 = jnp.bfloat16

D_MODEL = 1024
GRID_W = 64
N_HEADS = 16
HEAD_DIM = 64
WIN_H = 8
WIN_W = 16
ROPE_BASE = 10000.0
NB_B = 16
BS_B = 64
RG_C = 8.0
LNX_EPS = 64e-5
NORM_EPS = 1e-6
NEG_INF = -1e30
MASKED = -1e30
N_ATT = 3
G_XB, G_GB, G_R, G_K, G_V, G_MA, G_MB, G_MC, G_LR = range(9)
LR_W = 512
CHUNK = 64
VMEM_LIMIT = 56 * 1024 * 1024
PAIR = 2 * HEAD_DIM
SCAN_GROUP = 16


def _cparams(sem):
    return pltpu.CompilerParams(dimension_semantics=sem, vmem_limit_bytes=VMEM_LIMIT)


def _pick(n, cands):
    for c in cands:
        if n % c == 0:
            return c
    raise ValueError(f"no tile for {n}")


def _dot(a, b):
    return jnp.dot(a, b, preferred_element_type=F32)


def _dot_nt(a, b):
    return lax.dot_general(a, b, (((1,), (1,)), ((), ())), preferred_element_type=F32)


def _dot_tn(a, b):
    return lax.dot_general(a, b, (((0,), (0,)), ((), ())), preferred_element_type=F32)


def _sigmoid(x):
    return 0.5 * jnp.tanh(0.5 * x) + 0.5


def _row_select(mod_ref, idx, rows, n_ctx):
    return jnp.where(rows < n_ctx, mod_ref[0, 0, idx:idx + 1, :], mod_ref[0, 1, idx:idx + 1, :])


def _seg_sum64(x):
    h1 = x.astype(BF16)
    r1 = x - h1.astype(F32)
    h2 = r1.astype(BF16)
    h3 = (r1 - h2.astype(F32)).astype(BF16)
    gi = lax.broadcasted_iota(jnp.int32, (3 * PAIR, PAIR), 0) % PAIR
    gj = lax.broadcasted_iota(jnp.int32, (3 * PAIR, PAIR), 1)
    ones = jnp.where(gi // HEAD_DIM == gj // HEAD_DIM, 1.0, 0.0).astype(BF16)
    out = []
    for p in range(x.shape[-1] // PAIR):
        sl = slice(p * PAIR, (p + 1) * PAIR)
        out.append(_dot(jnp.concatenate([h1[:, sl], h2[:, sl], h3[:, sl]], axis=1), ones))
    return jnp.concatenate(out, axis=1)


def _seq_block(d, i, nt, nctx):
    rev = jnp.where(i < nctx, nctx - 1 - i, nt - 1 - (i - nctx))
    return jnp.where(d == 0, i, rev)


def _ada_kernel(a_ref, w_ref, b_ref, o_ref):
    a = a_ref[...]
    s = a * _sigmoid(a)
    o_ref[...] = _dot(s.astype(BF16), w_ref[...].astype(BF16)) + b_ref[...]


def _ada(cvec, w, b):
    m, k = cvec.shape
    n = w.shape[1]
    tn = _pick(n, (1536, 1024, 512, 128))
    return pl.pallas_call(
        _ada_kernel,
        out_shape=jax.ShapeDtypeStruct((m, n), F32),
        grid=(n // tn,),
        in_specs=[pl.BlockSpec((m, k), lambda j: (0, 0)),
                  pl.BlockSpec((k, tn), lambda j: (0, j)),
                  pl.BlockSpec((1, tn), lambda j: (0, j))],
        out_specs=pl.BlockSpec((m, tn), lambda j: (0, j)),
        compiler_params=_cparams(("arbitrary",)),
        name="ada",
    )(cvec, w, b.reshape(1, n))


def _inproj_kernel(x_ref, g_ref, mod_ref, w_ref, c_ref, s1_ref, s2_ref, z_ref, q_ref, k_ref, v_ref, hb_ref,
                   *, tm, n_ctx):
    i = pl.program_id(1)
    j = pl.program_id(2)

    @pl.when(j == 0)
    def _():
        x = x_ref[0]
        ms = jnp.mean(x * x, axis=-1, keepdims=True)
        y = x * lax.rsqrt(ms + NORM_EPS) * g_ref[...]
        rows = i * tm + lax.broadcasted_iota(jnp.int32, (tm, 1), 0)
        sh = _row_select(mod_ref, 0, rows, n_ctx)
        sc = _row_select(mod_ref, 1, rows, n_ctx)
        hb_ref[...] = (y * (1.0 + sc) + sh).astype(BF16)

    z_ref[0] = _dot(hb_ref[...], w_ref[...])
    d = z_ref.shape[-1]
    quarter = HEAD_DIM // 4

    def rope():
        cos, s1, s2 = c_ref[...], s1_ref[...], s2_ref[...]
        out = []
        for p in range(d // PAIR):
            xp = z_ref[0, :, p * PAIR:(p + 1) * PAIR]
            out.append(xp * cos + pltpu.roll(xp, PAIR - quarter, 1) * s1 + pltpu.roll(xp, quarter, 1) * s2)
        return jnp.concatenate(out, axis=1)

    @pl.when(j == 0)
    def _():
        q_ref[0] = (rope() * (HEAD_DIM ** -0.5)).astype(BF16)

    @pl.when(j == 1)
    def _():
        k_ref[0] = rope().astype(BF16)

    @pl.when(j == 2)
    def _():
        v_ref[0] = z_ref[0].astype(BF16)


def _inproj(x, g, mods, w, tables, n_ctx):
    bsz, t, d = x.shape
    n = w.shape[1]
    tm = _pick(t, (768, 256))
    tn = d
    tspec = pl.BlockSpec((tm, PAIR), lambda b, i, j: (i, 0))
    aspec = pl.BlockSpec((1, tm, d), lambda b, i, j: (b, i, 0))
    ashape = jax.ShapeDtypeStruct((bsz, t, d), BF16)
    return pl.pallas_call(
        functools.partial(_inproj_kernel, tm=tm, n_ctx=n_ctx),
        out_shape=(jax.ShapeDtypeStruct((bsz, t, n - N_ATT * d), F32), ashape, ashape, ashape),
        grid=(bsz, t // tm, n // tn),
        in_specs=[pl.BlockSpec((1, tm, d), lambda b, i, j: (b, i, 0)),
                  pl.BlockSpec((1, d), lambda b, i, j: (0, 0)),
                  pl.BlockSpec((1, 2, 6, d), lambda b, i, j: (b, 0, 0, 0)),
                  pl.BlockSpec((d, tn), lambda b, i, j: (0, j)),
                  tspec, tspec, tspec],
        out_specs=(pl.BlockSpec((1, tm, tn), lambda b, i, j: (b, i, jnp.maximum(j - N_ATT, 0))),
                   aspec, aspec, aspec),
        scratch_shapes=[pltpu.VMEM((tm, d), BF16)],
        compiler_params=_cparams(("parallel", "parallel", "arbitrary")),
        name="inproj",
    )(x, g.reshape(1, d), mods, w, *tables)


def _rope_tables(t, n_ctx):
    nf = HEAD_DIM // 4
    inv_freq = ROPE_BASE ** (-jnp.arange(nf, dtype=F32) / nf)
    tok = jnp.arange(t) - n_ctx
    row = (tok // GRID_W).astype(F32)
    col = (tok % GRID_W).astype(F32)
    lane = np.arange(PAIR) % HEAD_DIM
    use_col = jnp.asarray(lane >= HEAD_DIM // 2)
    first = jnp.asarray((lane % (HEAD_DIM // 2)) < nf)
    pos = jnp.where(use_col[None, :], col[:, None], row[:, None])
    ang = pos * inv_freq[lane % nf][None, :]
    is_lat = (tok >= 0)[:, None]
    cos = jnp.where(is_lat, jnp.cos(ang), 1.0)
    sin = jnp.where(is_lat, jnp.sin(ang), 0.0)
    return cos, jnp.where(first[None, :], -sin, 0.0), jnp.where(first[None, :], 0.0, sin)


def _natten_kernel(q_ref, *refs):
    k_refs = refs[0:WIN_H]
    v_refs = refs[WIN_H:2 * WIN_H]
    kc_ref, vc_ref = refs[2 * WIN_H:2 * WIN_H + 2]
    tb_refs = refs[2 * WIN_H + 2:-1]
    o_ref = refs[-1]
    nq = q_ref.shape[1]
    nloc = WIN_H * GRID_W
    lane = lax.broadcasted_iota(jnp.int32, (nq, PAIR), 1)
    first = lane < HEAD_DIM
    zero = jnp.zeros((nq, PAIR), BF16)
    sls = [slice(p * PAIR, (p + 1) * PAIR) for p in range(N_HEADS // 2)]
    qs = [jnp.concatenate([jnp.where(first, q_ref[0, :, sl], zero), jnp.where(first, zero, q_ref[0, :, sl])], axis=0)
          for sl in sls]
    ks = [jnp.concatenate([r[0, :, sl] for r in k_refs] + [kc_ref[0, :, sl]], axis=0) for sl in sls]
    scores = [_dot_nt(q, k) for q, k in zip(qs, ks)]
    probs = []
    for p, s in enumerate(scores):
        tb = jnp.concatenate([t[0, p] for t in tb_refs], axis=1)
        sl_ = s[:, :nloc] + tb
        sc_ = s[:, nloc:]
        m = jnp.maximum(jnp.max(sl_, axis=-1, keepdims=True), jnp.max(sc_, axis=-1, keepdims=True))
        el_ = jnp.exp(sl_ - m)
        ec_ = jnp.exp(sc_ - m)
        inv = 1.0 / (jnp.sum(el_, axis=-1, keepdims=True) + jnp.sum(ec_, axis=-1, keepdims=True))
        probs.append(jnp.concatenate([(el_ * inv).astype(BF16), (ec_ * inv).astype(BF16)], axis=1))
    vs = [jnp.concatenate([r[0, :, sl] for r in v_refs] + [vc_ref[0, :, sl]], axis=0) for sl in sls]
    outs = [_dot(pr, v) for pr, v in zip(probs, vs)]
    for sl, o in zip(sls, outs):
        o_ref[0, :, sl] = jnp.where(first, o[:nq], o[nq:]).astype(BF16)


def _natten_table(rpb):
    col = np.arange(GRID_W)
    c0 = np.clip(col - WIN_W // 2, 0, GRID_W - WIN_W)
    in_win = (col[None, :] >= c0[:, None]) & (col[None, :] < c0[:, None] + WIN_W)
    padw = GRID_W - WIN_W
    padded = jnp.pad(rpb.astype(F32), ((0, 0), (0, 0), (padw, padw)))
    bias = jnp.stack([padded[:, :, GRID_W - 1 - q:2 * GRID_W - 1 - q] for q in range(GRID_W)], axis=2)
    b = jnp.where(jnp.asarray(in_win)[None, None], bias, MASKED)
    b = jnp.transpose(b, (1, 0, 2, 3)).reshape(2 * WIN_H - 1, N_HEADS // 2, 2 * GRID_W, GRID_W)
    two = jnp.concatenate([b[:-1], b[1:]], axis=-1)
    return jnp.concatenate([two, jnp.full((1,) + two.shape[1:], MASKED, F32)], axis=0)


def _natten(qr, kr, vb, table, n_ctx):
    bsz, t, d = qr.shape
    nqc = n_ctx // GRID_W
    rows = (t - n_ctx) // GRID_W
    assert rows >= WIN_H
    masked_entry = table.shape[0] - 1

    def r0(s):
        return jnp.clip(s - nqc - WIN_H // 2, 0, rows - WIN_H)

    def entry(s, c):
        return jnp.where(s < nqc, masked_entry, 2 * c + WIN_H - 1 - (s - nqc - r0(s)))

    blk = (1, GRID_W, d)
    kv_specs = [pl.BlockSpec(blk, functools.partial(lambda b, s, i: (b, nqc + r0(s) + i, 0), i=i))
                for i in range(WIN_H)]
    ctx_spec = pl.BlockSpec((1, n_ctx, d), lambda b, s: (b, 0, 0))
    tb_specs = [pl.BlockSpec((1,) + table.shape[1:], functools.partial(lambda b, s, c: (entry(s, c), 0, 0, 0), c=c))
                for c in range(WIN_H // 2)]
    return pl.pallas_call(
        _natten_kernel,
        out_shape=jax.ShapeDtypeStruct((bsz, t, d), BF16),
        grid=(bsz, t // GRID_W),
        in_specs=[pl.BlockSpec(blk, lambda b, s: (b, s, 0))] + kv_specs + kv_specs + [ctx_spec, ctx_spec] + tb_specs,
        out_specs=pl.BlockSpec(blk, lambda b, s: (b, s, 0)),
        compiler_params=_cparams(("parallel", "arbitrary")),
        name="natten",
    )(qr, *([kr] * WIN_H), *([vb] * WIN_H), kr, vb, *([table] * (WIN_H // 2)))


def _block_diag_groups(w, per):
    two, nb, bs, _ = w.shape
    w = w.reshape(two, nb // per, per, bs, bs)
    eye = jnp.eye(per, dtype=w.dtype)
    return jnp.einsum('dgaij,ab->dgaibj', w, eye).reshape(two, nb // per, per * bs, per * bs)


def _rwkv_prep_kernel(*refs, tb, nt, nctx, has_vres):
    (zr, zr_p, zr_n, zk, zk_p, zk_n, zv, zv_p, zv_n, zl, zl_p, zl_n) = refs[:12]
    refs = refs[12:]
    if has_vres:
        vf_ref, refs = refs[0], refs[1:]
    (mu_ref, kk_ref, ka_ref, rk_ref, w0_ref, a0_ref, v0_ref, w2_ref,
     r_o, v_o, kk_o, g_o, bonus_o, lw_o, kd_o, bb_o) = refs
    blk = pl.program_id(1)
    seq_first = (blk == 0) | (blk == nctx)
    seq_last = (blk == nctx - 1) | (blk == nt - 1)
    row = lax.broadcasted_iota(jnp.int32, (tb, 1), 0)

    def nbr_mean(main, prev_row, next_row):
        prev_row = jnp.where(seq_first, 0.0, prev_row)
        next_row = jnp.where(seq_last, 0.0, next_row)
        up = jnp.where(row == 0, prev_row, pltpu.roll(main, 1, 0))
        dn = jnp.where(row == tb - 1, next_row, pltpu.roll(main, tb - 1, 0))
        return 0.5 * (up + dn)

    def shifted(m_ref, p_ref, n_ref, mu):
        main = m_ref[0]
        return main + mu * (nbr_mean(main, p_ref[0, 7:8, :], n_ref[0, 0:1, :]) - main)

    mu = mu_ref[...]
    r = shifted(zr, zr_p, zr_n, mu[0:1])
    k = shifted(zk, zk_p, zk_n, mu[1:2])
    v = shifted(zv, zv_p, zv_n, mu[2:3])

    lr = zl[0, :, 0:LR_W] + nbr_mean(zl[0, :, LR_W:2 * LR_W], zl_p[0, 7:8, LR_W:2 * LR_W],
                                     zl_n[0, 0:1, LR_W:2 * LR_W])
    col = lax.broadcasted_iota(jnp.int32, (1, LR_W), 1)
    act = jnp.where(col < 128, jnp.tanh(lr), jnp.where((col >= 256) & (col < 416), _sigmoid(lr), lr))
    actb = act.astype(BF16)
    s0 = actb[:, 0:256]
    s1 = actb[:, 256:512]
    g_o[0] = _dot(s1, w2_ref[4])
    if has_vres:
        v = v + (vf_ref[0] - v) * _sigmoid(v0_ref[...] + _dot(s1, w2_ref[5]))
    r_o[0] = r
    v_o[0] = v
    kkr = k * kk_ref[...]
    kk = kkr / jnp.maximum(jnp.sqrt(_seg_sum64(kkr * kkr)), 1e-12)
    kk_o[0] = kk
    ksum = jnp.zeros_like(k)
    for dd in range(2):
        zw = w0_ref[dd:dd + 1, :] + _dot(s0, w2_ref[dd])
        lw_o[dd, 0] = -math.exp(-0.5) * _sigmoid(zw)
        a = _sigmoid(a0_ref[dd:dd + 1, :] + _dot(s0, w2_ref[2 + dd]))
        kd = k * (1.0 + (a - 1.0) * ka_ref[...])
        kd_o[dd, 0] = kd
        bb_o[dd, 0] = kk * a
        ksum = ksum + kd
    bonus_o[0] = _seg_sum64(r * ksum * rk_ref[...]) * v


def _rwkv_prep(z, vfirst, mu_rkv, k_k, k_a, r_k, w0, a0, v0, w2cat, n_ctx):
    bsz, t, _ = z.shape
    d = D_MODEL
    tb = 256
    nt = t // tb
    nctx = n_ctx // tb
    hb = tb // 8
    has_vres = vfirst is not None

    def triple(g):
        return [pl.BlockSpec((1, tb, d), lambda b, i: (b, i, g)),
                pl.BlockSpec((1, 8, d), lambda b, i: (b, jnp.maximum(i * hb - 1, 0), g)),
                pl.BlockSpec((1, 8, d), lambda b, i: (b, jnp.minimum((i + 1) * hb, t // 8 - 1), g))]

    tok = pl.BlockSpec((1, tb, d), lambda b, i: (b, i, 0))
    tok2 = pl.BlockSpec((2, 1, tb, d), lambda b, i: (0, b, i, 0))
    const = lambda shape: pl.BlockSpec(shape, lambda b, i: (0,) * len(shape))
    in_specs = triple(G_R) + triple(G_K) + triple(G_V) + triple(G_LR)
    args = [z] * 12
    if has_vres:
        in_specs.append(tok)
        args.append(vfirst)
    in_specs += [const((3, d)), const((1, d)), const((1, d)), const((1, d)), const((2, d)), const((2, d)),
                 const((1, d)), const(w2cat.shape)]
    args += [mu_rkv, k_k.reshape(1, d), k_a.reshape(1, d), r_k.reshape(1, d), w0, a0, v0.reshape(1, d), w2cat]
    one = jax.ShapeDtypeStruct((bsz, t, d), F32)
    two = jax.ShapeDtypeStruct((2, bsz, t, d), F32)
    return pl.pallas_call(
        functools.partial(_rwkv_prep_kernel, tb=tb, nt=nt, nctx=nctx, has_vres=has_vres),
        out_shape=(one, one, one, one, one, two, two, two),
        grid=(bsz, nt),
        in_specs=in_specs,
        out_specs=(tok, tok, tok, tok, tok, tok2, tok2, tok2),
        compiler_params=_cparams(("parallel", "parallel")),
        name="rwkv_prep",
    )(*args)


def _seq_kernel(r_ref, v_ref, kk_ref, lw_ref, kd_ref, bb_ref,
                u_ref, up_ref, un_ref, cw_ref, cb_ref, wa_ref, ba_ref, wx_ref, bx_ref, lam_ref,
                y_ref, hl_ref, s_ref, a_s, b_s, h_s, *, nc, nctx):
    d = pl.program_id(0)
    i = pl.program_id(1)
    bsz, n = r_ref.shape[0], r_ref.shape[1]
    rev = d == 1
    blk = _seq_block(d, i, nc, nctx)

    @pl.when(i == 0)
    def _():
        s_ref[...] = jnp.zeros_like(s_ref)
        h_s[...] = jnp.zeros_like(h_s)

    seq_first = (blk == 0) | (blk == nctx)
    seq_last = (blk == nctx - 1) | (blk == nc - 1)
    row = lax.broadcasted_iota(jnp.int32, (n, 1), 0)
    cw = cw_ref[...]
    nl = -lam_ref[0]
    softplus = jnp.maximum(nl, 0.0) + jnp.log(1.0 + jnp.exp(-jnp.abs(nl)))
    gw = wa_ref.shape[-1]
    for b in range(bsz):
        x = u_ref[b]
        prev = jnp.where(seq_first, 0.0, up_ref[b])
        nxt = jnp.where(seq_last, 0.0, un_ref[b])
        xm1 = jnp.where(row == 0, prev[7:8], pltpu.roll(x, 1, 0))
        xm2 = jnp.where(row == 0, prev[6:7], jnp.where(row == 1, prev[7:8], pltpu.roll(x, 2, 0)))
        xp1 = jnp.where(row == n - 1, nxt[0:1], pltpu.roll(x, n - 1, 0))
        u = cw[0:1] * xm2 + cw[1:2] * xm1 + cw[2:3] * x + cw[3:4] * xp1 + cb_ref[...]
        ub = u.astype(BF16)
        ngrp = u.shape[-1] // gw
        gr = jnp.concatenate([_dot(ub[:, g * gw:(g + 1) * gw], wa_ref[0, g]) for g in range(ngrp)], axis=1)
        gi = jnp.concatenate([_dot(ub[:, g * gw:(g + 1) * gw], wx_ref[0, g]) for g in range(ngrp)], axis=1)
        log_a = -RG_C * _sigmoid(gr + ba_ref[0]) * softplus
        a_s[b] = jnp.exp(log_a)
        b_s[b] = jnp.sqrt(1.0 - jnp.exp(2.0 * log_a)) * (_sigmoid(gi + bx_ref[0]) * u)

    hs = [h_s[b] for b in range(bsz)]
    for t in range(n):
        tt = jnp.where(rev, n - 1 - t, t)
        for b in range(bsz):
            hs[b] = a_s[b, pl.ds(tt, 1), :] * hs[b] + b_s[b, pl.ds(tt, 1), :]
            hl_ref[0, b, pl.ds(tt, 1), :] = hs[b]
    for b in range(bsz):
        h_s[b] = hs[b]

    ti = lax.broadcasted_iota(jnp.int32, (n, n), 0)
    tj = lax.broadcasted_iota(jnp.int32, (n, n), 1)
    order = jnp.where(rev, tj - ti, ti - tj)
    tri = jnp.where(order >= 0, 1.0, 0.0).astype(BF16)
    wide = []
    gls = []
    for b in range(bsz):
        lw = lw_ref[0, b]
        h1 = lw.astype(BF16)
        r1 = lw - h1.astype(F32)
        h2 = r1.astype(BF16)
        h3 = (r1 - h2.astype(F32)).astype(BF16)
        c = _dot(tri, h1) + _dot(tri, h2) + _dot(tri, h3)
        c_last = jnp.where(rev, c[0:1], c[n - 1:n])
        kd = kd_ref[0, b]
        bb = bb_ref[0, b]
        enc = jnp.exp(-c)
        el = jnp.exp(c_last - c)
        wide.append([t.astype(BF16) for t in (-kk_ref[b] * jnp.exp(c - lw), r_ref[b] * jnp.exp(c), bb * enc, kd * enc,
                                              v_ref[b], bb * el, kd * el)])
        gls.append(jnp.exp(c_last))

    lane = lax.broadcasted_iota(jnp.int32, (n, PAIR), 1)
    first = lane < HEAD_DIM
    si = lax.broadcasted_iota(jnp.int32, (2 * n, 2 * n), 0) & (n - 1)
    sj = lax.broadcasted_iota(jnp.int32, (2 * n, 2 * n), 1) & (n - 1)
    row2 = lax.broadcasted_iota(jnp.int32, (2 * n, 2 * n), 0)
    col2 = lax.broadcasted_iota(jnp.int32, (2 * n, 2 * n), 1)
    qi = jnp.where(rev, n - 1 - si, si)
    qj = jnp.where(rev, n - 1 - sj, sj)
    strict = qi > qj
    incl = qi >= qj
    eye = jnp.where(row2 == col2, 1.0, 0.0)
    levels = []
    blk = 1
    while blk < n:
        bi = qi // blk
        bj = qj // blk
        levels.append((bi // 2 == bj // 2) & (bi % 2 == 1) & (bj % 2 == 0))
        blk *= 2

    zero = jnp.zeros((n, PAIR), BF16)

    def stack(x, sl):
        return jnp.concatenate([jnp.where(first, x[:, sl], zero), jnp.where(first, zero, x[:, sl])], axis=0)

    zero2 = jnp.zeros((2 * n, 2 * n), BF16)
    chains = [(b, p) for b in range(bsz) for p in range(N_HEADS // 2)]
    for g0 in range(0, len(chains), SCAN_GROUP):
        grp = chains[g0:g0 + SCAN_GROUP]
        sls = [slice(p * PAIR, (p + 1) * PAIR) for _, p in grp]
        opnd = [[stack(t, sl) for t in wide[b]] for (b, _), sl in zip(grp, sls)]
        s2 = [s_ref[b, p] for b, p in grp]
        s2b = [s.astype(BF16) for s in s2]
        aa = [_dot_nt(jnp.concatenate([am, rm], axis=0), jnp.concatenate([bm, km], axis=0))
              for am, rm, bm, km, _, _, _ in opnd]
        a_ab = [jnp.where(strict, a[:2 * n, :2 * n], 0.0) for a in aa]
        a_abb = [a.astype(BF16) for a in a_ab]
        a_ak = [jnp.where(strict, a[:2 * n, 2 * n:], 0.0).astype(BF16) for a in aa]
        a_r = [jnp.concatenate([jnp.where(incl, a[2 * n:, :2 * n], 0.0), jnp.where(incl, a[2 * n:, 2 * n:], 0.0)],
                               axis=1).astype(BF16) for a in aa]
        x = [_dot_nt(o[0], sb) + _dot(ak, o[4]) for o, sb, ak in zip(opnd, s2b, a_ak)]
        tinv = [eye + jnp.where(levels[0], a, 0.0) for a in a_ab]
        for m in levels[1:]:
            tb = [t.astype(BF16) for t in tinv]
            half = [_dot(t, jnp.where(m, a, zero2)) for t, a in zip(tb, a_abb)]
            tinv = [t + _dot(h.astype(BF16), q) for t, h, q in zip(tinv, half, tb)]
        uv = [jnp.concatenate([_dot(t.astype(BF16), xx.astype(BF16)).astype(BF16), o[4]], axis=0)
              for t, xx, o in zip(tinv, x, opnd)]
        for j, (b, p) in enumerate(grp):
            rm, blm, klm = opnd[j][1], opnd[j][5], opnd[j][6]
            o = _dot_nt(rm, s2b[j]) + _dot(a_r[j], uv[j])
            y_ref[0, b, :, sls[j]] = o[:n] + o[n:]
            s_ref[b, p] = s2[j] * gls[b][:, sls[j]] + _dot_tn(uv[j], jnp.concatenate([blm, klm], axis=0))


def _seq_mixers(r, v, kk, lw, kd, bb, z, conv_w, conv_b, wa, ba, wx, bx, lam, n_ctx):
    bsz, t, d = r.shape
    nc = t // CHUNK
    nctx = n_ctx // CHUNK
    hb = CHUNK // 8
    gw = wa.shape[-1]

    def blk_of(dd, i):
        return _seq_block(dd, i, nc, nctx)

    tok = pl.BlockSpec((bsz, CHUNK, d), lambda dd, i: (0, blk_of(dd, i), 0))
    tok2 = pl.BlockSpec((1, bsz, CHUNK, d), lambda dd, i: (dd, 0, blk_of(dd, i), 0))
    vec = lambda: pl.BlockSpec((1, 1, d), lambda dd, i: (dd, 0, 0))
    wspec = lambda: pl.BlockSpec((1, d // gw, gw, gw), lambda dd, i: (dd, 0, 0, 0))
    out = jax.ShapeDtypeStruct((2, bsz, t, d), F32)
    return pl.pallas_call(
        functools.partial(_seq_kernel, nc=nc, nctx=nctx),
        out_shape=(out, out),
        grid=(2, nc),
        in_specs=[tok, tok, tok, tok2, tok2, tok2,
                  pl.BlockSpec((bsz, CHUNK, d), lambda dd, i: (0, blk_of(dd, i), G_XB)),
                  pl.BlockSpec((bsz, 8, d), lambda dd, i: (0, jnp.maximum(blk_of(dd, i) * hb - 1, 0), G_XB)),
                  pl.BlockSpec((bsz, 8, d), lambda dd, i: (0, jnp.minimum((blk_of(dd, i) + 1) * hb, t // 8 - 1), G_XB)),
                  pl.BlockSpec((4, d), lambda dd, i: (0, 0)),
                  pl.BlockSpec((1, d), lambda dd, i: (0, 0)),
                  wspec(), vec(), wspec(), vec(), vec()],
        out_specs=(tok2, tok2),
        scratch_shapes=[pltpu.VMEM((bsz, N_HEADS // 2, PAIR, PAIR), F32),
                        pltpu.VMEM((bsz, CHUNK, d), F32), pltpu.VMEM((bsz, CHUNK, d), F32),
                        pltpu.VMEM((bsz, 1, d), F32)],
        compiler_params=_cparams(("arbitrary", "arbitrary")),
        name="seq_mixers",
    )(r, v, kk, lw, kd, bb, z, z, z, conv_w, conv_b.reshape(1, d), wa, ba.reshape(2, 1, d), wx, bx.reshape(2, 1, d),
      lam.reshape(2, 1, d))


def _gelu_tanh(x):
    return 0.5 * x * (1.0 + jnp.tanh(math.sqrt(2.0 / math.pi) * (x + 0.044715 * (x * x * x))))


def _merge_kernel(x_ref, ya_ref, hl_ref, ug_ref, yw_ref, bonus_ref, g_ref, ma_ref, mb_ref, mc_ref,
                  mod_ref, lnw_ref, lnb_ref, wa_ref, wb_ref, wc_ref, wo_ref, o_ref, *, tm, n_ctx):
    i = pl.program_id(1)
    yb = ((hl_ref[0, 0] + hl_ref[1, 0]) * _gelu_tanh(ug_ref[0])).astype(BF16)
    y = yw_ref[0, 0] + yw_ref[1, 0]
    inv = 1.0 / HEAD_DIM
    mean = _seg_sum64(y) * inv
    yc = y - mean
    var = _seg_sum64(yc * yc) * inv
    yn = yc * lax.rsqrt(var + LNX_EPS) * lnw_ref[...] + lnb_ref[...]
    oc = ((yn + bonus_ref[0]) * g_ref[0]).astype(BF16)
    mixed = (_sigmoid(ma_ref[0]) * _dot(ya_ref[0], wa_ref[...])
             + _sigmoid(mb_ref[0]) * _dot(yb, wb_ref[...])
             + _sigmoid(mc_ref[0]) * _dot(oc, wc_ref[...]))
    yo = _dot(mixed.astype(BF16), wo_ref[...])
    rows = i * tm + lax.broadcasted_iota(jnp.int32, (tm, 1), 0)
    o_ref[0] = x_ref[0] + _row_select(mod_ref, 2, rows, n_ctx) * yo


def _merge(x, ynat, hlru, z, yrw, bonus, g, mods, ln_w, ln_b, wa, wb, wc, wo, n_ctx):
    bsz, t, d = x.shape
    tm = 256
    tok = pl.BlockSpec((1, tm, d), lambda b, i: (b, i, 0))
    tok2 = pl.BlockSpec((2, 1, tm, d), lambda b, i: (0, b, i, 0))
    zspec = lambda gidx: pl.BlockSpec((1, tm, d), lambda b, i: (b, i, gidx))
    vec = pl.BlockSpec((1, d), lambda b, i: (0, 0))
    wspec = pl.BlockSpec((d, d), lambda b, i: (0, 0))
    return pl.pallas_call(
        functools.partial(_merge_kernel, tm=tm, n_ctx=n_ctx),
        out_shape=jax.ShapeDtypeStruct((bsz, t, d), F32),
        grid=(bsz, t // tm),
        in_specs=[tok, tok, tok2, zspec(G_GB), tok2, tok, tok, zspec(G_MA), zspec(G_MB), zspec(G_MC),
                  pl.BlockSpec((1, 2, 6, d), lambda b, i: (b, 0, 0, 0)), vec, vec,
                  wspec, wspec, wspec, wspec],
        out_specs=tok,
        compiler_params=_cparams(("parallel", "parallel")),
        name="merge",
    )(x, ynat, hlru, z, yrw, bonus, g, z, z, z, mods, ln_w.reshape(1, d), ln_b.reshape(1, d), wa, wb, wc, wo)


def _mlp_kernel(x_ref, g_ref, mod_ref, w1_ref, w2_ref, fg_ref, o_ref, hb_ref, acc_ref, *, tm, n_ctx, final):
    i = pl.program_id(1)
    k = pl.program_id(2)
    rows = i * tm + lax.broadcasted_iota(jnp.int32, (tm, 1), 0)

    @pl.when(k == 0)
    def _():
        x = x_ref[0]
        ms = jnp.mean(x * x, axis=-1, keepdims=True)
        y = x * lax.rsqrt(ms + NORM_EPS) * g_ref[...]
        sh = _row_select(mod_ref, 3, rows, n_ctx)
        sc = _row_select(mod_ref, 4, rows, n_ctx)
        hb_ref[...] = (y * (1.0 + sc) + sh).astype(BF16)
        acc_ref[...] = jnp.zeros_like(acc_ref)

    a = jnp.maximum(_dot(hb_ref[...], w1_ref[...]), 0.0)
    acc_ref[...] += _dot((a * a).astype(BF16), w2_ref[...])

    @pl.when(k == pl.num_programs(2) - 1)
    def _():
        xn = x_ref[0] + _row_select(mod_ref, 5, rows, n_ctx) * acc_ref[...]
        if final:
            ms = jnp.mean(xn * xn, axis=-1, keepdims=True)
            xn = xn * lax.rsqrt(ms + NORM_EPS) * fg_ref[...]
        o_ref[0] = xn


def _mlp(x, g, mods, w1, w2, final_g, n_ctx, final):
    bsz, t, d = x.shape
    dff = w1.shape[1]
    tm = _pick(t, (768, 256))
    tk = 1024
    vec = pl.BlockSpec((1, d), lambda b, i, k: (0, 0))
    tok = pl.BlockSpec((1, tm, d), lambda b, i, k: (b, i, 0))
    return pl.pallas_call(
        functools.partial(_mlp_kernel, tm=tm, n_ctx=n_ctx, final=final),
        out_shape=jax.ShapeDtypeStruct((bsz, t, d), F32),
        grid=(bsz, t // tm, dff // tk),
        in_specs=[tok, vec, pl.BlockSpec((1, 2, 6, d), lambda b, i, k: (b, 0, 0, 0)),
                  pl.BlockSpec((d, tk), lambda b, i, k: (0, k)),
                  pl.BlockSpec((tk, d), lambda b, i, k: (k, 0)), vec],
        out_specs=tok,
        scratch_shapes=[pltpu.VMEM((tm, d), BF16), pltpu.VMEM((tm, d), F32)],
        compiler_params=_cparams(("parallel", "parallel", "arbitrary")),
        name="mlp",
    )(x, g.reshape(1, d), mods, w1, w2, final_g.reshape(1, d))


def _lowrank_weights(mu_h, w1, a1, g1, v1, mu_v):
    d = mu_h.shape[-1]
    blocks = [(w1[0], mu_h[0]), (w1[1], mu_h[0]), (a1[0], mu_h[1]), (a1[1], mu_h[1]), (g1, mu_h[2])]
    if v1 is not None:
        blocks.append((v1, mu_v))
    p = jnp.concatenate([w * (1.0 - m)[:, None] for w, m in blocks], axis=1)
    q = jnp.concatenate([w * m[:, None] for w, m in blocks], axis=1)
    pad = lambda a: jnp.pad(a, ((0, 0), (0, LR_W - a.shape[1])))
    return jnp.concatenate([pad(p), pad(q)], axis=1)


def _second_stage_weights(w2, a2, g2, v2):
    d = w2.shape[-1]
    def place(w, off):
        return jnp.zeros((256, d), F32).at[off:off + w.shape[0]].set(w)
    v2p = place(v2, 160) if v2 is not None else jnp.zeros((256, d), F32)
    return jnp.stack([place(w2[0], 0), place(w2[1], 64), place(a2[0], 128), place(a2[1], 192),
                      place(g2, 0), v2p]).astype(BF16)


def kernel(x, c, ctx, c_ctx, w_ada, b_ada, norm1_g, norm2_g, w_in, rpb, conv_w, conv_b, rg_wa, rg_ba, rg_wx, rg_bx, rg_lam, rw_mu_rkv, rw_mu_h, rw_w0, rw_w1, rw_w2, rw_a0, rw_a1, rw_a2, rw_g1, rw_g2, rw_k_k, rw_k_a, rw_r_k, rw_ln_w, rw_ln_b, vres_v0, vres_v1, vres_v2, vres_mu, w_br_a, w_br_b, w_br_c, w_out, w_ff1, w_ff2, final_g):
    bsz, seq, d = x.shape
    n_ctx = ctx.shape[1]
    depth = w_in.shape[0]
    t = n_ctx + seq
    xs = jnp.concatenate([ctx, x], axis=1)
    cvec = jnp.zeros((8, d), F32).at[:bsz].set(c).at[bsz].set(c_ctx)
    tables = _rope_tables(t, n_ctx)
    vfirst = None
    for i in range(depth):
        last = i == depth - 1
        mod = _ada(cvec, w_ada[i], b_ada[i])
        mods = jnp.stack([jnp.broadcast_to(mod[bsz].reshape(1, 6, d), (bsz, 6, d)),
                          mod[:bsz].reshape(bsz, 6, d)], axis=1)
        if i == 0:
            lr_w = _lowrank_weights(rw_mu_h[i], rw_w1[i], rw_a1[i], rw_g1[i], None, None)
            w2cat = _second_stage_weights(rw_w2[i], rw_a2[i], rw_g2[i], None)
            v0 = jnp.zeros((d,), F32)
        else:
            lr_w = _lowrank_weights(rw_mu_h[i], rw_w1[i], rw_a1[i], rw_g1[i], vres_v1[i - 1], vres_mu[i - 1])
            w2cat = _second_stage_weights(rw_w2[i], rw_a2[i], rw_g2[i], vres_v2[i - 1])
            v0 = vres_v0[i - 1]
        w_all = jnp.concatenate([w_in[i], lr_w], axis=1).astype(BF16)
        z, qr, kr, vb = _inproj(xs, norm1_g[i], mods, w_all, tables, n_ctx)
        ynat = _natten(qr, kr, vb, _natten_table(rpb[i]), n_ctx)
        r, v, kk, g, bonus, lw, kd, bb = _rwkv_prep(
            z, vfirst, rw_mu_rkv[i], rw_k_k[i], rw_k_a[i], rw_r_k[i].reshape(-1), rw_w0[i], rw_a0[i], v0,
            w2cat, n_ctx)
        if i == 0:
            vfirst = v
        yrw, hlru = _seq_mixers(r, v, kk, lw, kd, bb, z, conv_w[i], conv_b[i],
                                _block_diag_groups(rg_wa[i], 4).astype(BF16), rg_ba[i],
                                _block_diag_groups(rg_wx[i], 4).astype(BF16), rg_bx[i], rg_lam[i], n_ctx)
        xs = _merge(xs, ynat, hlru, z, yrw, bonus, g, mods, rw_ln_w[i], rw_ln_b[i],
                    w_br_a[i].astype(BF16), w_br_b[i].astype(BF16), w_br_c[i].astype(BF16),
                    w_out[i].astype(BF16), n_ctx)
        xs = _mlp(xs, norm2_g[i], mods, w_ff1[i].astype(BF16), w_ff2[i].astype(BF16), final_g, n_ctx, last)
    return xs[:, n_ctx:]
```

```python
import functools
import math

import jax
import jax.numpy as jnp
import numpy as np
from jax import lax
from jax.experimental import pallas as pl
from jax.experimental.pallas import tpu as pltpu

F32 = jnp.float32
BF16 = jnp.bfloat16

D_MODEL = 1024
GRID_W = 64
N_HEADS = 16
HEAD_DIM = 64
WIN_H = 8
WIN_W = 16
ROPE_BASE = 10000.0
NB_B = 16
BS_B = 64
RG_C = 8.0
LNX_EPS = 64e-5
NORM_EPS = 1e-6
NEG_INF = -1e30
MASKED = -1e30
N_ATT = 3
G_XB, G_GB, G_R, G_K, G_V, G_MA, G_MB, G_MC, G_LR = range(9)
LR_W = 512
CHUNK = 64
VMEM_LIMIT = 58 * 1024 * 1024
PAIR = 2 * HEAD_DIM
SCAN_GROUP = 16


def _cparams(sem):
    return pltpu.CompilerParams(dimension_semantics=sem, vmem_limit_bytes=VMEM_LIMIT)


def _pick(n, cands):
    for c in cands:
        if n % c == 0:
            return c
    raise ValueError(f"no tile for {n}")


def _dot(a, b):
    return jnp.dot(a, b, preferred_element_type=F32)


def _dot_nt(a, b):
    return lax.dot_general(a, b, (((1,), (1,)), ((), ())), preferred_element_type=F32)


def _dot_tn(a, b):
    return lax.dot_general(a, b, (((0,), (0,)), ((), ())), preferred_element_type=F32)


def _sigmoid(x):
    return 0.5 * jnp.tanh(0.5 * x) + 0.5


def _row_select(mod_ref, idx, rows, n_ctx):
    return jnp.where(rows < n_ctx, mod_ref[0, 0, idx:idx + 1, :], mod_ref[0, 1, idx:idx + 1, :])


def _seg_sum64(x):
    h1 = x.astype(BF16)
    r1 = x - h1.astype(F32)
    h2 = r1.astype(BF16)
    h3 = (r1 - h2.astype(F32)).astype(BF16)
    gi = lax.broadcasted_iota(jnp.int32, (3 * PAIR, PAIR), 0) % PAIR
    gj = lax.broadcasted_iota(jnp.int32, (3 * PAIR, PAIR), 1)
    ones = jnp.where(gi // HEAD_DIM == gj // HEAD_DIM, 1.0, 0.0).astype(BF16)
    out = []
    for p in range(x.shape[-1] // PAIR):
        sl = slice(p * PAIR, (p + 1) * PAIR)
        out.append(_dot(jnp.concatenate([h1[:, sl], h2[:, sl], h3[:, sl]], axis=1), ones))
    return jnp.concatenate(out, axis=1)


def _seq_block(d, i, nt, nctx):
    rev = jnp.where(i < nctx, nctx - 1 - i, nt - 1 - (i - nctx))
    return jnp.where(d == 0, i, rev)


def _ada_kernel(a_ref, w_ref, b_ref, o_ref):
    a = a_ref[...]
    s = a * _sigmoid(a)
    o_ref[...] = _dot(s.astype(BF16), w_ref[...].astype(BF16)) + b_ref[...]


def _ada(cvec, w, b):
    m, k = cvec.shape
    n = w.shape[1]
    tn = _pick(n, (1536, 1024, 512, 128))
    return pl.pallas_call(
        _ada_kernel,
        out_shape=jax.ShapeDtypeStruct((m, n), F32),
        grid=(n // tn,),
        in_specs=[pl.BlockSpec((m, k), lambda j: (0, 0)),
                  pl.BlockSpec((k, tn), lambda j: (0, j)),
                  pl.BlockSpec((1, tn), lambda j: (0, j))],
        out_specs=pl.BlockSpec((m, tn), lambda j: (0, j)),
        compiler_params=_cparams(("arbitrary",)),
        name="ada",
    )(cvec, w, b.reshape(1, n))


def _inproj_kernel(x_ref, g_ref, mod_ref, w_ref, c_ref, s1_ref, s2_ref, z_ref, q_ref, k_ref, v_ref, hb_ref,
                   *, tm, n_ctx):
    i = pl.program_id(1)
    j = pl.program_id(2)

    @pl.when(j == 0)
    def _():
        x = x_ref[0]
        ms = jnp.mean(x * x, axis=-1, keepdims=True)
        y = x * lax.rsqrt(ms + NORM_EPS) * g_ref[...]
        rows = i * tm + lax.broadcasted_iota(jnp.int32, (tm, 1), 0)
        sh = _row_select(mod_ref, 0, rows, n_ctx)
        sc = _row_select(mod_ref, 1, rows, n_ctx)
        hb_ref[...] = (y * (1.0 + sc) + sh).astype(BF16)

    z_ref[0] = _dot(hb_ref[...], w_ref[...])
    d = z_ref.shape[-1]
    quarter = HEAD_DIM // 4

    def rope():
        cos, s1, s2 = c_ref[...], s1_ref[...], s2_ref[...]
        out = []
        for p in range(d // PAIR):
            xp = z_ref[0, :, p * PAIR:(p + 1) * PAIR]
            out.append(xp * cos + pltpu.roll(xp, PAIR - quarter, 1) * s1 + pltpu.roll(xp, quarter, 1) * s2)
        return jnp.concatenate(out, axis=1)

    @pl.when(j == 0)
    def _():
        q_ref[0] = (rope() * (HEAD_DIM ** -0.5)).astype(BF16)

    @pl.when(j == 1)
    def _():
        k_ref[0] = rope().astype(BF16)

    @pl.when(j == 2)
    def _():
        v_ref[0] = z_ref[0].astype(BF16)


def _inproj(x, g, mods, w, tables, n_ctx):
    bsz, t, d = x.shape
    n = w.shape[1]
    tm = _pick(t, (1408, 768, 256))
    tn = d
    tspec = pl.BlockSpec((tm, PAIR), lambda b, i, j: (i, 0))
    aspec = pl.BlockSpec((1, tm, d), lambda b, i, j: (b, i, 0))
    ashape = jax.ShapeDtypeStruct((bsz, t, d), BF16)
    return pl.pallas_call(
        functools.partial(_inproj_kernel, tm=tm, n_ctx=n_ctx),
        out_shape=(jax.ShapeDtypeStruct((bsz, t, n - N_ATT * d), F32), ashape, ashape, ashape),
        grid=(bsz, t // tm, n // tn),
        in_specs=[pl.BlockSpec((1, tm, d), lambda b, i, j: (b, i, 0)),
                  pl.BlockSpec((1, d), lambda b, i, j: (0, 0)),
                  pl.BlockSpec((1, 2, 6, d), lambda b, i, j: (b, 0, 0, 0)),
                  pl.BlockSpec((d, tn), lambda b, i, j: (0, j)),
                  tspec, tspec, tspec],
        out_specs=(pl.BlockSpec((1, tm, tn), lambda b, i, j: (b, i, jnp.maximum(j - N_ATT, 0))),
                   aspec, aspec, aspec),
        scratch_shapes=[pltpu.VMEM((tm, d), BF16)],
        compiler_params=_cparams(("parallel", "parallel", "arbitrary")),
        name="inproj",
    )(x, g.reshape(1, d), mods, w, *tables)


def _rope_tables(t, n_ctx):
    nf = HEAD_DIM // 4
    inv_freq = ROPE_BASE ** (-jnp.arange(nf, dtype=F32) / nf)
    tok = jnp.arange(t) - n_ctx
    row = (tok // GRID_W).astype(F32)
    col = (tok % GRID_W).astype(F32)
    lane = np.arange(PAIR) % HEAD_DIM
    use_col = jnp.asarray(lane >= HEAD_DIM // 2)
    first = jnp.asarray((lane % (HEAD_DIM // 2)) < nf)
    pos = jnp.where(use_col[None, :], col[:, None], row[:, None])
    ang = pos * inv_freq[lane % nf][None, :]
    is_lat = (tok >= 0)[:, None]
    cos = jnp.where(is_lat, jnp.cos(ang), 1.0)
    sin = jnp.where(is_lat, jnp.sin(ang), 0.0)
    return cos, jnp.where(first[None, :], -sin, 0.0), jnp.where(first[None, :], 0.0, sin)


def _natten_kernel(q_ref, *refs):
    k_refs = refs[0:WIN_H]
    v_refs = refs[WIN_H:2 * WIN_H]
    kc_ref, vc_ref = refs[2 * WIN_H:2 * WIN_H + 2]
    tb_refs = refs[2 * WIN_H + 2:-1]
    o_ref = refs[-1]
    nq = q_ref.shape[1]
    nloc = WIN_H * GRID_W
    lane = lax.broadcasted_iota(jnp.int32, (nq, PAIR), 1)
    first = lane < HEAD_DIM
    zero = jnp.zeros((nq, PAIR), BF16)
    sls = [slice(p * PAIR, (p + 1) * PAIR) for p in range(N_HEADS // 2)]
    qs = [jnp.concatenate([jnp.where(first, q_ref[0, :, sl], zero), jnp.where(first, zero, q_ref[0, :, sl])], axis=0)
          for sl in sls]
    ks = [jnp.concatenate([r[0, :, sl] for r in k_refs] + [kc_ref[0, :, sl]], axis=0) for sl in sls]
    scores = [_dot_nt(q, k) for q, k in zip(qs, ks)]
    probs = []
    for p, s in enumerate(scores):
        tb = jnp.concatenate([t[0, p] for t in tb_refs], axis=1)
        sl_ = s[:, :nloc] + tb
        sc_ = s[:, nloc:]
        m = jnp.maximum(jnp.max(sl_, axis=-1, keepdims=True), jnp.max(sc_, axis=-1, keepdims=True))
        el_ = jnp.exp(sl_ - m)
        ec_ = jnp.exp(sc_ - m)
        inv = 1.0 / (jnp.sum(el_, axis=-1, keepdims=True) + jnp.sum(ec_, axis=-1, keepdims=True))
        probs.append(jnp.concatenate([(el_ * inv).astype(BF16), (ec_ * inv).astype(BF16)], axis=1))
    vs = [jnp.concatenate([r[0, :, sl] for r in v_refs] + [vc_ref[0, :, sl]], axis=0) for sl in sls]
    outs = [_dot(pr, v) for pr, v in zip(probs, vs)]
    for sl, o in zip(sls, outs):
        o_ref[0, :, sl] = jnp.where(first, o[:nq], o[nq:]).astype(BF16)


def _natten_table(rpb):
    col = np.arange(GRID_W)
    c0 = np.clip(col - WIN_W // 2, 0, GRID_W - WIN_W)
    in_win = (col[None, :] >= c0[:, None]) & (col[None, :] < c0[:, None] + WIN_W)
    padw = GRID_W - WIN_W
    padded = jnp.pad(rpb.astype(F32), ((0, 0), (0, 0), (padw, padw)))
    bias = jnp.stack([padded[:, :, GRID_W - 1 - q:2 * GRID_W - 1 - q] for q in range(GRID_W)], axis=2)
    b = jnp.where(jnp.asarray(in_win)[None, None], bias, MASKED)
    b = jnp.transpose(b, (1, 0, 2, 3)).reshape(2 * WIN_H - 1, N_HEADS // 2, 2 * GRID_W, GRID_W)
    two = jnp.concatenate([b[:-1], b[1:]], axis=-1)
    return jnp.concatenate([two, jnp.full((1,) + two.shape[1:], MASKED, F32)], axis=0)


def _natten(qr, kr, vb, table, n_ctx):
    bsz, t, d = qr.shape
    nqc = n_ctx // GRID_W
    rows = (t - n_ctx) // GRID_W
    assert rows >= WIN_H
    masked_entry = table.shape[0] - 1

    def r0(s):
        return jnp.clip(s - nqc - WIN_H // 2, 0, rows - WIN_H)

    def entry(s, c):
        return jnp.where(s < nqc, masked_entry, 2 * c + WIN_H - 1 - (s - nqc - r0(s)))

    blk = (1, GRID_W, d)
    kv_specs = [pl.BlockSpec(blk, functools.partial(lambda b, s, i: (b, nqc + r0(s) + i, 0), i=i))
                for i in range(WIN_H)]
    ctx_spec = pl.BlockSpec((1, n_ctx, d), lambda b, s: (b, 0, 0))
    tb_specs = [pl.BlockSpec((1,) + table.shape[1:], functools.partial(lambda b, s, c: (entry(s, c), 0, 0, 0), c=c))
                for c in range(WIN_H // 2)]
    return pl.pallas_call(
        _natten_kernel,
        out_shape=jax.ShapeDtypeStruct((bsz, t, d), BF16),
        grid=(bsz, t // GRID_W),
        in_specs=[pl.BlockSpec(blk, lambda b, s: (b, s, 0))] + kv_specs + kv_specs + [ctx_spec, ctx_spec] + tb_specs,
        out_specs=pl.BlockSpec(blk, lambda b, s: (b, s, 0)),
        compiler_params=_cparams(("parallel", "arbitrary")),
        name="natten",
    )(qr, *([kr] * WIN_H), *([vb] * WIN_H), kr, vb, *([table] * (WIN_H // 2)))


def _block_diag_groups(w, per):
    two, nb, bs, _ = w.shape
    w = w.reshape(two, nb // per, per, bs, bs)
    eye = jnp.eye(per, dtype=w.dtype)
    return jnp.einsum('dgaij,ab->dgaibj', w, eye).reshape(two, nb // per, per * bs, per * bs)


def _rwkv_prep_kernel(*refs, tb, nt, nctx, has_vres):
    (zr, zr_p, zr_n, zk, zk_p, zk_n, zv, zv_p, zv_n, zl, zl_p, zl_n) = refs[:12]
    refs = refs[12:]
    if has_vres:
        vf_ref, refs = refs[0], refs[1:]
    (mu_ref, kk_ref, ka_ref, rk_ref, w0_ref, a0_ref, v0_ref, w2_ref,
     r_o, v_o, kk_o, g_o, bonus_o, lw_o, kd_o, bb_o) = refs
    blk = pl.program_id(1)
    seq_first = (blk == 0) | (blk == nctx)
    seq_last = (blk == nctx - 1) | (blk == nt - 1)
    row = lax.broadcasted_iota(jnp.int32, (tb, 1), 0)

    def nbr_mean(main, prev_row, next_row):
        prev_row = jnp.where(seq_first, 0.0, prev_row)
        next_row = jnp.where(seq_last, 0.0, next_row)
        up = jnp.where(row == 0, prev_row, pltpu.roll(main, 1, 0))
        dn = jnp.where(row == tb - 1, next_row, pltpu.roll(main, tb - 1, 0))
        return 0.5 * (up + dn)

    def shifted(m_ref, p_ref, n_ref, mu):
        main = m_ref[0]
        return main + mu * (nbr_mean(main, p_ref[0, 7:8, :], n_ref[0, 0:1, :]) - main)

    mu = mu_ref[...]
    r = shifted(zr, zr_p, zr_n, mu[0:1])
    k = shifted(zk, zk_p, zk_n, mu[1:2])
    v = shifted(zv, zv_p, zv_n, mu[2:3])

    lr = zl[0, :, 0:LR_W] + nbr_mean(zl[0, :, LR_W:2 * LR_W], zl_p[0, 7:8, LR_W:2 * LR_W],
                                     zl_n[0, 0:1, LR_W:2 * LR_W])
    col = lax.broadcasted_iota(jnp.int32, (1, LR_W), 1)
    act = jnp.where(col < 128, jnp.tanh(lr), jnp.where((col >= 256) & (col < 416), _sigmoid(lr), lr))
    actb = act.astype(BF16)
    s0 = actb[:, 0:256]
    s1 = actb[:, 256:512]
    g_o[0] = _dot(s1, w2_ref[4])
    if has_vres:
        v = v + (vf_ref[0] - v) * _sigmoid(v0_ref[...] + _dot(s1, w2_ref[5]))
    r_o[0] = r
    v_o[0] = v
    kkr = k * kk_ref[...]
    kk = kkr / jnp.maximum(jnp.sqrt(_seg_sum64(kkr * kkr)), 1e-12)
    kk_o[0] = kk
    ksum = jnp.zeros_like(k)
    for dd in range(2):
        zw = w0_ref[dd:dd + 1, :] + _dot(s0, w2_ref[dd])
        lw_o[dd, 0] = -math.exp(-0.5) * _sigmoid(zw)
        a = _sigmoid(a0_ref[dd:dd + 1, :] + _dot(s0, w2_ref[2 + dd]))
        kd = k * (1.0 + (a - 1.0) * ka_ref[...])
        kd_o[dd, 0] = kd
        bb_o[dd, 0] = kk * a
        ksum = ksum + kd
    bonus_o[0] = _seg_sum64(r * ksum * rk_ref[...]) * v


def _rwkv_prep(z, vfirst, mu_rkv, k_k, k_a, r_k, w0, a0, v0, w2cat, n_ctx):
    bsz, t, _ = z.shape
    d = D_MODEL
    tb = 256
    nt = t // tb
    nctx = n_ctx // tb
    hb = tb // 8
    has_vres = vfirst is not None

    def triple(g):
        return [pl.BlockSpec((1, tb, d), lambda b, i: (b, i, g)),
                pl.BlockSpec((1, 8, d), lambda b, i: (b, jnp.maximum(i * hb - 1, 0), g)),
                pl.BlockSpec((1, 8, d), lambda b, i: (b, jnp.minimum((i + 1) * hb, t // 8 - 1), g))]

    tok = pl.BlockSpec((1, tb, d), lambda b, i: (b, i, 0))
    tok2 = pl.BlockSpec((2, 1, tb, d), lambda b, i: (0, b, i, 0))
    const = lambda shape: pl.BlockSpec(shape, lambda b, i: (0,) * len(shape))
    in_specs = triple(G_R) + triple(G_K) + triple(G_V) + triple(G_LR)
    args = [z] * 12
    if has_vres:
        in_specs.append(tok)
        args.append(vfirst)
    in_specs += [const((3, d)), const((1, d)), const((1, d)), const((1, d)), const((2, d)), const((2, d)),
                 const((1, d)), const(w2cat.shape)]
    args += [mu_rkv, k_k.reshape(1, d), k_a.reshape(1, d), r_k.reshape(1, d), w0, a0, v0.reshape(1, d), w2cat]
    one = jax.ShapeDtypeStruct((bsz, t, d), F32)
    two = jax.ShapeDtypeStruct((2, bsz, t, d), F32)
    return pl.pallas_call(
        functools.partial(_rwkv_prep_kernel, tb=tb, nt=nt, nctx=nctx, has_vres=has_vres),
        out_shape=(one, one, one, one, one, two, two, two),
        grid=(bsz, nt),
        in_specs=in_specs,
        out_specs=(tok, tok, tok, tok, tok, tok2, tok2, tok2),
        compiler_params=_cparams(("parallel", "parallel")),
        name="rwkv_prep",
    )(*args)


def _seq_kernel(r_ref, v_ref, kk_ref, lw_ref, kd_ref, bb_ref,
                u_ref, up_ref, un_ref, cw_ref, cb_ref, wa_ref, ba_ref, wx_ref, bx_ref, lam_ref,
                y_ref, hl_ref, s_ref, a_s, b_s, h_s, *, nc, nctx):
    d = pl.program_id(0)
    i = pl.program_id(1)
    bsz, n = r_ref.shape[0], r_ref.shape[1]
    rev = d == 1
    blk = _seq_block(d, i, nc, nctx)

    @pl.when(i == 0)
    def _():
        s_ref[...] = jnp.zeros_like(s_ref)
        h_s[...] = jnp.zeros_like(h_s)

    seq_first = (blk == 0) | (blk == nctx)
    seq_last = (blk == nctx - 1) | (blk == nc - 1)
    row = lax.broadcasted_iota(jnp.int32, (n, 1), 0)
    cw = cw_ref[...]
    nl = -lam_ref[0]
    softplus = jnp.maximum(nl, 0.0) + jnp.log(1.0 + jnp.exp(-jnp.abs(nl)))
    gw = wa_ref.shape[-1]
    for b in range(bsz):
        x = u_ref[b]
        prev = jnp.where(seq_first, 0.0, up_ref[b])
        nxt = jnp.where(seq_last, 0.0, un_ref[b])
        xm1 = jnp.where(row == 0, prev[7:8], pltpu.roll(x, 1, 0))
        xm2 = jnp.where(row == 0, prev[6:7], jnp.where(row == 1, prev[7:8], pltpu.roll(x, 2, 0)))
        xp1 = jnp.where(row == n - 1, nxt[0:1], pltpu.roll(x, n - 1, 0))
        u = cw[0:1] * xm2 + cw[1:2] * xm1 + cw[2:3] * x + cw[3:4] * xp1 + cb_ref[...]
        ub = u.astype(BF16)
        ngrp = u.shape[-1] // gw
        gr = jnp.concatenate([_dot(ub[:, g * gw:(g + 1) * gw], wa_ref[0, g]) for g in range(ngrp)], axis=1)
        gi = jnp.concatenate([_dot(ub[:, g * gw:(g + 1) * gw], wx_ref[0, g]) for g in range(ngrp)], axis=1)
        log_a = -RG_C * _sigmoid(gr + ba_ref[0]) * softplus
        a_s[b] = jnp.exp(log_a)
        b_s[b] = jnp.sqrt(1.0 - jnp.exp(2.0 * log_a)) * (_sigmoid(gi + bx_ref[0]) * u)

    hs = [h_s[b] for b in range(bsz)]
    for t in range(n):
        tt = jnp.where(rev, n - 1 - t, t)
        for b in range(bsz):
            hs[b] = a_s[b, pl.ds(tt, 1), :] * hs[b] + b_s[b, pl.ds(tt, 1), :]
            hl_ref[0, b, pl.ds(tt, 1), :] = hs[b]
    for b in range(bsz):
        h_s[b] = hs[b]

    ti = lax.broadcasted_iota(jnp.int32, (n, n), 0)
    tj = lax.broadcasted_iota(jnp.int32, (n, n), 1)
    order = jnp.where(rev, tj - ti, ti - tj)
    tri = jnp.where(order >= 0, 1.0, 0.0).astype(BF16)
    wide = []
    gls = []
    for b in range(bsz):
        lw = lw_ref[0, b]
        h1 = lw.astype(BF16)
        r1 = lw - h1.astype(F32)
        h2 = r1.astype(BF16)
        h3 = (r1 - h2.astype(F32)).astype(BF16)
        c = _dot(tri, h1) + _dot(tri, h2) + _dot(tri, h3)
        c_last = jnp.where(rev, c[0:1], c[n - 1:n])
        kd = kd_ref[0, b]
        bb = bb_ref[0, b]
        enc = jnp.exp(-c)
        el = jnp.exp(c_last - c)
        wide.append([t.astype(BF16) for t in (-kk_ref[b] * jnp.exp(c - lw), r_ref[b] * jnp.exp(c), bb * enc, kd * enc,
                                              v_ref[b], bb * el, kd * el)])
        gls.append(jnp.exp(c_last))

    lane = lax.broadcasted_iota(jnp.int32, (n, PAIR), 1)
    first = lane < HEAD_DIM
    si = lax.broadcasted_iota(jnp.int32, (2 * n, 2 * n), 0) & (n - 1)
    sj = lax.broadcasted_iota(jnp.int32, (2 * n, 2 * n), 1) & (n - 1)
    row2 = lax.broadcasted_iota(jnp.int32, (2 * n, 2 * n), 0)
    col2 = lax.broadcasted_iota(jnp.int32, (2 * n, 2 * n), 1)
    qi = jnp.where(rev, n - 1 - si, si)
    qj = jnp.where(rev, n - 1 - sj, sj)
    strict = qi > qj
    incl = qi >= qj
    eye = jnp.where(row2 == col2, 1.0, 0.0)
    levels = []
    blk = 1
    while blk < n:
        bi = qi // blk
        bj = qj // blk
        levels.append((bi // 2 == bj // 2) & (bi % 2 == 1) & (bj % 2 == 0))
        blk *= 2

    zero = jnp.zeros((n, PAIR), BF16)

    def stack(x, sl):
        return jnp.concatenate([jnp.where(first, x[:, sl], zero), jnp.where(first, zero, x[:, sl])], axis=0)

    zero2 = jnp.zeros((2 * n, 2 * n), BF16)
    chains = [(b, p) for b in range(bsz) for p in range(N_HEADS // 2)]
    for g0 in range(0, len(chains), SCAN_GROUP):
        grp = chains[g0:g0 + SCAN_GROUP]
        sls = [slice(p * PAIR, (p + 1) * PAIR) for _, p in grp]
        opnd = [[stack(t, sl) for t in wide[b]] for (b, _), sl in zip(grp, sls)]
        s2 = [s_ref[b, p] for b, p in grp]
        s2b = [s.astype(BF16) for s in s2]
        aa = [_dot_nt(jnp.concatenate([am, rm], axis=0), jnp.concatenate([bm, km], axis=0))
              for am, rm, bm, km, _, _, _ in opnd]
        a_ab = [jnp.where(strict, a[:2 * n, :2 * n], 0.0) for a in aa]
        a_abb = [a.astype(BF16) for a in a_ab]
        a_ak = [jnp.where(strict, a[:2 * n, 2 * n:], 0.0).astype(BF16) for a in aa]
        a_r = [jnp.concatenate([jnp.where(incl, a[2 * n:, :2 * n], 0.0), jnp.where(incl, a[2 * n:, 2 * n:], 0.0)],
                               axis=1).astype(BF16) for a in aa]
        x = [_dot_nt(o[0], sb) + _dot(ak, o[4]) for o, sb, ak in zip(opnd, s2b, a_ak)]
        tinv = [eye + jnp.where(levels[0], a, 0.0) for a in a_ab]
        for m in levels[1:]:
            tb = [t.astype(BF16) for t in tinv]
            half = [_dot(t, jnp.where(m, a, zero2)) for t, a in zip(tb, a_abb)]
            tinv = [t + _dot(h.astype(BF16), q) for t, h, q in zip(tinv, half, tb)]
        uv = [jnp.concatenate([_dot(t.astype(BF16), xx.astype(BF16)).astype(BF16), o[4]], axis=0)
              for t, xx, o in zip(tinv, x, opnd)]
        for j, (b, p) in enumerate(grp):
            rm, blm, klm = opnd[j][1], opnd[j][5], opnd[j][6]
            o = _dot_nt(rm, s2b[j]) + _dot(a_r[j], uv[j])
            y_ref[0, b, :, sls[j]] = o[:n] + o[n:]
            s_ref[b, p] = s2[j] * gls[b][:, sls[j]] + _dot_tn(uv[j], jnp.concatenate([blm, klm], axis=0))


def _seq_mixers(r, v, kk, lw, kd, bb, z, conv_w, conv_b, wa, ba, wx, bx, lam, n_ctx):
    bsz, t, d = r.shape
    nc = t // CHUNK
    nctx = n_ctx // CHUNK
    hb = CHUNK // 8
    gw = wa.shape[-1]

    def blk_of(dd, i):
        return _seq_block(dd, i, nc, nctx)

    tok = pl.BlockSpec((bsz, CHUNK, d), lambda dd, i: (0, blk_of(dd, i), 0))
    tok2 = pl.BlockSpec((1, bsz, CHUNK, d), lambda dd, i: (dd, 0, blk_of(dd, i), 0))
    vec = lambda: pl.BlockSpec((1, 1, d), lambda dd, i: (dd, 0, 0))
    wspec = lambda: pl.BlockSpec((1, d // gw, gw, gw), lambda dd, i: (dd, 0, 0, 0))
    out = jax.ShapeDtypeStruct((2, bsz, t, d), F32)
    return pl.pallas_call(
        functools.partial(_seq_kernel, nc=nc, nctx=nctx),
        out_shape=(out, out),
        grid=(2, nc),
        in_specs=[tok, tok, tok, tok2, tok2, tok2,
                  pl.BlockSpec((bsz, CHUNK, d), lambda dd, i: (0, blk_of(dd, i), G_XB)),
                  pl.BlockSpec((bsz, 8, d), lambda dd, i: (0, jnp.maximum(blk_of(dd, i) * hb - 1, 0), G_XB)),
                  pl.BlockSpec((bsz, 8, d), lambda dd, i: (0, jnp.minimum((blk_of(dd, i) + 1) * hb, t // 8 - 1), G_XB)),
                  pl.BlockSpec((4, d), lambda dd, i: (0, 0)),
                  pl.BlockSpec((1, d), lambda dd, i: (0, 0)),
                  wspec(), vec(), wspec(), vec(), vec()],
        out_specs=(tok2, tok2),
        scratch_shapes=[pltpu.VMEM((bsz, N_HEADS // 2, PAIR, PAIR), F32),
                        pltpu.VMEM((bsz, CHUNK, d), F32), pltpu.VMEM((bsz, CHUNK, d), F32),
                        pltpu.VMEM((bsz, 1, d), F32)],
        compiler_params=_cparams(("arbitrary", "arbitrary")),
        name="seq_mixers",
    )(r, v, kk, lw, kd, bb, z, z, z, conv_w, conv_b.reshape(1, d), wa, ba.reshape(2, 1, d), wx, bx.reshape(2, 1, d),
      lam.reshape(2, 1, d))


def _gelu_tanh(x):
    return 0.5 * x * (1.0 + jnp.tanh(math.sqrt(2.0 / math.pi) * (x + 0.044715 * (x * x * x))))


def _merge_kernel(x_ref, ya_ref, hl_ref, ug_ref, yw_ref, bonus_ref, g_ref, ma_ref, mb_ref, mc_ref,
                  mod_ref, lnw_ref, lnb_ref, wa_ref, wb_ref, wc_ref, wo_ref, o_ref, *, tm, n_ctx):
    i = pl.program_id(1)
    yb = ((hl_ref[0, 0] + hl_ref[1, 0]) * _gelu_tanh(ug_ref[0])).astype(BF16)
    y = yw_ref[0, 0] + yw_ref[1, 0]
    inv = 1.0 / HEAD_DIM
    mean = _seg_sum64(y) * inv
    yc = y - mean
    var = _seg_sum64(yc * yc) * inv
    yn = yc * lax.rsqrt(var + LNX_EPS) * lnw_ref[...] + lnb_ref[...]
    oc = ((yn + bonus_ref[0]) * g_ref[0]).astype(BF16)
    mixed = (_sigmoid(ma_ref[0]) * _dot(ya_ref[0], wa_ref[...])
             + _sigmoid(mb_ref[0]) * _dot(yb, wb_ref[...])
             + _sigmoid(mc_ref[0]) * _dot(oc, wc_ref[...]))
    yo = _dot(mixed.astype(BF16), wo_ref[...])
    rows = i * tm + lax.broadcasted_iota(jnp.int32, (tm, 1), 0)
    o_ref[0] = x_ref[0] + _row_select(mod_ref, 2, rows, n_ctx) * yo


def _merge(x, ynat, hlru, z, yrw, bonus, g, mods, ln_w, ln_b, wa, wb, wc, wo, n_ctx):
    bsz, t, d = x.shape
    tm = 256
    tok = pl.BlockSpec((1, tm, d), lambda b, i: (b, i, 0))
    tok2 = pl.BlockSpec((2, 1, tm, d), lambda b, i: (0, b, i, 0))
    zspec = lambda gidx: pl.BlockSpec((1, tm, d), lambda b, i: (b, i, gidx))
    vec = pl.BlockSpec((1, d), lambda b, i: (0, 0))
    wspec = pl.BlockSpec((d, d), lambda b, i: (0, 0))
    return pl.pallas_call(
        functools.partial(_merge_kernel, tm=tm, n_ctx=n_ctx),
        out_shape=jax.ShapeDtypeStruct((bsz, t, d), F32),
        grid=(bsz, t // tm),
        in_specs=[tok, tok, tok2, zspec(G_GB), tok2, tok, tok, zspec(G_MA), zspec(G_MB), zspec(G_MC),
                  pl.BlockSpec((1, 2, 6, d), lambda b, i: (b, 0, 0, 0)), vec, vec,
                  wspec, wspec, wspec, wspec],
        out_specs=tok,
        compiler_params=_cparams(("parallel", "parallel")),
        name="merge",
    )(x, ynat, hlru, z, yrw, bonus, g, z, z, z, mods, ln_w.reshape(1, d), ln_b.reshape(1, d), wa, wb, wc, wo)


def _mlp_kernel(x_ref, g_ref, mod_ref, w1_ref, w2_ref, fg_ref, o_ref, hb_ref, acc_ref, *, tm, n_ctx, final):
    i = pl.program_id(1)
    k = pl.program_id(2)
    rows = i * tm + lax.broadcasted_iota(jnp.int32, (tm, 1), 0)

    @pl.when(k == 0)
    def _():
        x = x_ref[0]
        ms = jnp.mean(x * x, axis=-1, keepdims=True)
        y = x * lax.rsqrt(ms + NORM_EPS) * g_ref[...]
        sh = _row_select(mod_ref, 3, rows, n_ctx)
        sc = _row_select(mod_ref, 4, rows, n_ctx)
        hb_ref[...] = (y * (1.0 + sc) + sh).astype(BF16)
        acc_ref[...] = jnp.zeros_like(acc_ref)

    a = jnp.maximum(_dot(hb_ref[...], w1_ref[...]), 0.0)
    acc_ref[...] += _dot((a * a).astype(BF16), w2_ref[...])

    @pl.when(k == pl.num_programs(2) - 1)
    def _():
        xn = x_ref[0] + _row_select(mod_ref, 5, rows, n_ctx) * acc_ref[...]
        if final:
            ms = jnp.mean(xn * xn, axis=-1, keepdims=True)
            xn = xn * lax.rsqrt(ms + NORM_EPS) * fg_ref[...]
        o_ref[0] = xn


def _mlp(x, g, mods, w1, w2, final_g, n_ctx, final):
    bsz, t, d = x.shape
    dff = w1.shape[1]
    tm = _pick(t, (1408, 768, 256))
    tk = 1024
    vec = pl.BlockSpec((1, d), lambda b, i, k: (0, 0))
    tok = pl.BlockSpec((1, tm, d), lambda b, i, k: (b, i, 0))
    return pl.pallas_call(
        functools.partial(_mlp_kernel, tm=tm, n_ctx=n_ctx, final=final),
        out_shape=jax.ShapeDtypeStruct((bsz, t, d), F32),
        grid=(bsz, t // tm, dff // tk),
        in_specs=[tok, vec, pl.BlockSpec((1, 2, 6, d), lambda b, i, k: (b, 0, 0, 0)),
                  pl.BlockSpec((d, tk), lambda b, i, k: (0, k)),
                  pl.BlockSpec((tk, d), lambda b, i, k: (k, 0)), vec],
        out_specs=tok,
        scratch_shapes=[pltpu.VMEM((tm, d), BF16), pltpu.VMEM((tm, d), F32)],
        compiler_params=_cparams(("parallel", "parallel", "arbitrary")),
        name="mlp",
    )(x, g.reshape(1, d), mods, w1, w2, final_g.reshape(1, d))


def _lowrank_weights(mu_h, w1, a1, g1, v1, mu_v):
    d = mu_h.shape[-1]
    blocks = [(w1[0], mu_h[0]), (w1[1], mu_h[0]), (a1[0], mu_h[1]), (a1[1], mu_h[1]), (g1, mu_h[2])]
    if v1 is not None:
        blocks.append((v1, mu_v))
    p = jnp.concatenate([w * (1.0 - m)[:, None] for w, m in blocks], axis=1)
    q = jnp.concatenate([w * m[:, None] for w, m in blocks], axis=1)
    pad = lambda a: jnp.pad(a, ((0, 0), (0, LR_W - a.shape[1])))
    return jnp.concatenate([pad(p), pad(q)], axis=1)


def _second_stage_weights(w2, a2, g2, v2):
    d = w2.shape[-1]
    def place(w, off):
        return jnp.zeros((256, d), F32).at[off:off + w.shape[0]].set(w)
    v2p = place(v2, 160) if v2 is not None else jnp.zeros((256, d), F32)
    return jnp.stack([place(w2[0], 0), place(w2[1], 64), place(a2[0], 128), place(a2[1], 192),
                      place(g2, 0), v2p]).astype(BF16)


def kernel(x, c, ctx, c_ctx, w_ada, b_ada, norm1_g, norm2_g, w_in, rpb, conv_w, conv_b, rg_wa, rg_ba, rg_wx, rg_bx, rg_lam, rw_mu_rkv, rw_mu_h, rw_w0, rw_w1, rw_w2, rw_a0, rw_a1, rw_a2, rw_g1, rw_g2, rw_k_k, rw_k_a, rw_r_k, rw_ln_w, rw_ln_b, vres_v0, vres_v1, vres_v2, vres_mu, w_br_a, w_br_b, w_br_c, w_out, w_ff1, w_ff2, final_g):
    bsz, seq, d = x.shape
    n_ctx = ctx.shape[1]
    depth = w_in.shape[0]
    t = n_ctx + seq
    xs = jnp.concatenate([ctx, x], axis=1)
    cvec = jnp.zeros((8, d), F32).at[:bsz].set(c).at[bsz].set(c_ctx)
    tables = _rope_tables(t, n_ctx)
    vfirst = None
    for i in range(depth):
        last = i == depth - 1
        mod = _ada(cvec, w_ada[i], b_ada[i])
        mods = jnp.stack([jnp.broadcast_to(mod[bsz].reshape(1, 6, d), (bsz, 6, d)),
                          mod[:bsz].reshape(bsz, 6, d)], axis=1)
        if i == 0:
            lr_w = _lowrank_weights(rw_mu_h[i], rw_w1[i], rw_a1[i], rw_g1[i], None, None)
            w2cat = _second_stage_weights(rw_w2[i], rw_a2[i], rw_g2[i], None)
            v0 = jnp.zeros((d,), F32)
        else:
            lr_w = _lowrank_weights(rw_mu_h[i], rw_w1[i], rw_a1[i], rw_g1[i], vres_v1[i - 1], vres_mu[i - 1])
            w2cat = _second_stage_weights(rw_w2[i], rw_a2[i], rw_g2[i], vres_v2[i - 1])
            v0 = vres_v0[i - 1]
        w_all = jnp.concatenate([w_in[i], lr_w], axis=1).astype(BF16)
        z, qr, kr, vb = _inproj(xs, norm1_g[i], mods, w_all, tables, n_ctx)
        ynat = _natten(qr, kr, vb, _natten_table(rpb[i]), n_ctx)
        r, v, kk, g, bonus, lw, kd, bb = _rwkv_prep(
            z, vfirst, rw_mu_rkv[i], rw_k_k[i], rw_k_a[i], rw_r_k[i].reshape(-1), rw_w0[i], rw_a0[i], v0,
            w2cat, n_ctx)
        if i == 0:
            vfirst = v
        yrw, hlru = _seq_mixers(r, v, kk, lw, kd, bb, z, conv_w[i], conv_b[i],
                                _block_diag_groups(rg_wa[i], 4).astype(BF16), rg_ba[i],
                                _block_diag_groups(rg_wx[i], 4).astype(BF16), rg_bx[i], rg_lam[i], n_ctx)
        xs = _merge(xs, ynat, hlru, z, yrw, bonus, g, mods, rw_ln_w[i], rw_ln_b[i],
                    w_br_a[i].astype(BF16), w_br_b[i].astype(BF16), w_br_c[i].astype(BF16),
                    w_out[i].astype(BF16), n_ctx)
        xs = _mlp(xs, norm2_g[i], mods, w_ff1[i].astype(BF16), w_ff2[i].astype(BF16), final_g, n_ctx, last)
    return xs[:, n_ctx:]
```

```python
import functools
import math

import jax
import jax.numpy as jnp
import numpy as np
from jax import lax
from jax.experimental import pallas as pl
from jax.experimental.pallas import tpu as pltpu

F32 = jnp.float32
BF16 = jnp.bfloat16

D_MODEL = 1024
GRID_W = 64
N_HEADS = 16
HEAD_DIM = 64
WIN_H = 8
WIN_W = 16
ROPE_BASE = 10000.0
NB_B = 16
BS_B = 64
RG_C = 8.0
LNX_EPS = 64e-5
NORM_EPS = 1e-6
NEG_INF = -1e30
MASKED = -1e30
N_ATT = 3
G_XB, G_GB, G_R, G_K, G_V, G_MA, G_MB, G_MC, G_LR = range(9)
LR_W = 512
CHUNK = 64
VMEM_LIMIT = 58 * 1024 * 1024
PAIR = 2 * HEAD_DIM
SCAN_GROUP = 16


def _cparams(sem):
    return pltpu.CompilerParams(dimension_semantics=sem, vmem_limit_bytes=VMEM_LIMIT)


def _pick(n, cands):
    for c in cands:
        if n % c == 0:
            return c
    raise ValueError(f"no tile for {n}")


def _dot(a, b):
    return jnp.dot(a, b, preferred_element_type=F32)


def _dot_nt(a, b):
    return lax.dot_general(a, b, (((1,), (1,)), ((), ())), preferred_element_type=F32)


def _dot_tn(a, b):
    return lax.dot_general(a, b, (((0,), (0,)), ((), ())), preferred_element_type=F32)


def _sigmoid(x):
    return 0.5 * jnp.tanh(0.5 * x) + 0.5


def _row_select(mod_ref, idx, rows, n_ctx):
    return jnp.where(rows < n_ctx, mod_ref[0, 0, idx:idx + 1, :], mod_ref[0, 1, idx:idx + 1, :])


def _seg_sum64(x):
    h1 = x.astype(BF16)
    r1 = x - h1.astype(F32)
    h2 = r1.astype(BF16)
    h3 = (r1 - h2.astype(F32)).astype(BF16)
    gi = lax.broadcasted_iota(jnp.int32, (3 * PAIR, PAIR), 0) % PAIR
    gj = lax.broadcasted_iota(jnp.int32, (3 * PAIR, PAIR), 1)
    ones = jnp.where(gi // HEAD_DIM == gj // HEAD_DIM, 1.0, 0.0).astype(BF16)
    out = []
    for p in range(x.shape[-1] // PAIR):
        sl = slice(p * PAIR, (p + 1) * PAIR)
        out.append(_dot(jnp.concatenate([h1[:, sl], h2[:, sl], h3[:, sl]], axis=1), ones))
    return jnp.concatenate(out, axis=1)


def _seq_block(d, i, nt, nctx):
    rev = jnp.where(i < nctx, nctx - 1 - i, nt - 1 - (i - nctx))
    return jnp.where(d == 0, i, rev)


def _ada_kernel(a_ref, w_ref, b_ref, o_ref):
    a = a_ref[...]
    s = a * _sigmoid(a)
    o_ref[...] = _dot(s.astype(BF16), w_ref[...].astype(BF16)) + b_ref[...]


def _ada(cvec, w, b):
    m, k = cvec.shape
    n = w.shape[1]
    tn = _pick(n, (1536, 1024, 512, 128))
    return pl.pallas_call(
        _ada_kernel,
        out_shape=jax.ShapeDtypeStruct((m, n), F32),
        grid=(n // tn,),
        in_specs=[pl.BlockSpec((m, k), lambda j: (0, 0)),
                  pl.BlockSpec((k, tn), lambda j: (0, j)),
                  pl.BlockSpec((1, tn), lambda j: (0, j))],
        out_specs=pl.BlockSpec((m, tn), lambda j: (0, j)),
        compiler_params=_cparams(("arbitrary",)),
        name="ada",
    )(cvec, w, b.reshape(1, n))


def _inproj_kernel(x_ref, g_ref, mod_ref, w_ref, c_ref, s1_ref, s2_ref, z_ref, q_ref, k_ref, v_ref, hb_ref,
                   *, tm, n_ctx):
    i = pl.program_id(1)
    j = pl.program_id(2)

    @pl.when(j == 0)
    def _():
        x = x_ref[0]
        ms = jnp.mean(x * x, axis=-1, keepdims=True)
        y = x * lax.rsqrt(ms + NORM_EPS) * g_ref[...]
        rows = i * tm + lax.broadcasted_iota(jnp.int32, (tm, 1), 0)
        sh = _row_select(mod_ref, 0, rows, n_ctx)
        sc = _row_select(mod_ref, 1, rows, n_ctx)
        hb_ref[...] = (y * (1.0 + sc) + sh).astype(BF16)

    z_ref[0, 0] = _dot(hb_ref[...], w_ref[0])
    d = z_ref.shape[-1]
    quarter = HEAD_DIM // 4

    def rope():
        cos, s1, s2 = c_ref[...], s1_ref[...], s2_ref[...]
        out = []
        for p in range(d // PAIR):
            xp = z_ref[0, 0, :, p * PAIR:(p + 1) * PAIR]
            out.append(xp * cos + pltpu.roll(xp, PAIR - quarter, 1) * s1 + pltpu.roll(xp, quarter, 1) * s2)
        return jnp.concatenate(out, axis=1)

    @pl.when(j == 0)
    def _():
        q_ref[0] = (rope() * (HEAD_DIM ** -0.5)).astype(BF16)

    @pl.when(j == 1)
    def _():
        k_ref[0] = rope().astype(BF16)

    @pl.when(j == 2)
    def _():
        v_ref[0] = z_ref[0, 0].astype(BF16)


def _inproj(x, g, mods, w, tables, n_ctx):
    bsz, t, d = x.shape
    ngrp = w.shape[0]
    tm = _pick(t, (1408, 768, 256))
    tspec = pl.BlockSpec((tm, PAIR), lambda b, i, j: (i, 0))
    aspec = pl.BlockSpec((1, tm, d), lambda b, i, j: (b, i, 0))
    ashape = jax.ShapeDtypeStruct((bsz, t, d), BF16)
    return pl.pallas_call(
        functools.partial(_inproj_kernel, tm=tm, n_ctx=n_ctx),
        out_shape=(jax.ShapeDtypeStruct((ngrp - N_ATT, bsz, t, d), F32), ashape, ashape, ashape),
        grid=(bsz, t // tm, ngrp),
        in_specs=[pl.BlockSpec((1, tm, d), lambda b, i, j: (b, i, 0)),
                  pl.BlockSpec((1, d), lambda b, i, j: (0, 0)),
                  pl.BlockSpec((1, 2, 6, d), lambda b, i, j: (b, 0, 0, 0)),
                  pl.BlockSpec((1, d, d), lambda b, i, j: (j, 0, 0)),
                  tspec, tspec, tspec],
        out_specs=(pl.BlockSpec((1, 1, tm, d), lambda b, i, j: (jnp.maximum(j - N_ATT, 0), b, i, 0)),
                   aspec, aspec, aspec),
        scratch_shapes=[pltpu.VMEM((tm, d), BF16)],
        compiler_params=_cparams(("parallel", "parallel", "arbitrary")),
        name="inproj",
    )(x, g.reshape(1, d), mods, w, *tables)


def _rope_tables(t, n_ctx):
    nf = HEAD_DIM // 4
    inv_freq = ROPE_BASE ** (-jnp.arange(nf, dtype=F32) / nf)
    tok = jnp.arange(t) - n_ctx
    row = (tok // GRID_W).astype(F32)
    col = (tok % GRID_W).astype(F32)
    lane = np.arange(PAIR) % HEAD_DIM
    use_col = jnp.asarray(lane >= HEAD_DIM // 2)
    first = jnp.asarray((lane % (HEAD_DIM // 2)) < nf)
    pos = jnp.where(use_col[None, :], col[:, None], row[:, None])
    ang = pos * inv_freq[lane % nf][None, :]
    is_lat = (tok >= 0)[:, None]
    cos = jnp.where(is_lat, jnp.cos(ang), 1.0)
    sin = jnp.where(is_lat, jnp.sin(ang), 0.0)
    return cos, jnp.where(first[None, :], -sin, 0.0), jnp.where(first[None, :], 0.0, sin)


def _natten_kernel(q_ref, *refs):
    k_refs = refs[0:WIN_H]
    v_refs = refs[WIN_H:2 * WIN_H]
    kc_ref, vc_ref = refs[2 * WIN_H:2 * WIN_H + 2]
    tb_refs = refs[2 * WIN_H + 2:-1]
    o_ref = refs[-1]
    nq = q_ref.shape[1]
    nloc = WIN_H * GRID_W
    lane = lax.broadcasted_iota(jnp.int32, (nq, PAIR), 1)
    first = lane < HEAD_DIM
    zero = jnp.zeros((nq, PAIR), BF16)
    sls = [slice(p * PAIR, (p + 1) * PAIR) for p in range(N_HEADS // 2)]
    qs = [jnp.concatenate([jnp.where(first, q_ref[0, :, sl], zero), jnp.where(first, zero, q_ref[0, :, sl])], axis=0)
          for sl in sls]
    ks = [jnp.concatenate([r[0, :, sl] for r in k_refs] + [kc_ref[0, :, sl]], axis=0) for sl in sls]
    scores = [_dot_nt(q, k) for q, k in zip(qs, ks)]
    probs = []
    for p, s in enumerate(scores):
        tb = jnp.concatenate([t[0, p] for t in tb_refs], axis=1)
        sl_ = s[:, :nloc] + tb
        sc_ = s[:, nloc:]
        m = jnp.maximum(jnp.max(sl_, axis=-1, keepdims=True), jnp.max(sc_, axis=-1, keepdims=True))
        el_ = jnp.exp(sl_ - m)
        ec_ = jnp.exp(sc_ - m)
        inv = 1.0 / (jnp.sum(el_, axis=-1, keepdims=True) + jnp.sum(ec_, axis=-1, keepdims=True))
        probs.append(jnp.concatenate([(el_ * inv).astype(BF16), (ec_ * inv).astype(BF16)], axis=1))
    vs = [jnp.concatenate([r[0, :, sl] for r in v_refs] + [vc_ref[0, :, sl]], axis=0) for sl in sls]
    outs = [_dot(pr, v) for pr, v in zip(probs, vs)]
    for sl, o in zip(sls, outs):
        o_ref[0, :, sl] = jnp.where(first, o[:nq], o[nq:]).astype(BF16)


def _natten_table(rpb):
    col = np.arange(GRID_W)
    c0 = np.clip(col - WIN_W // 2, 0, GRID_W - WIN_W)
    in_win = (col[None, :] >= c0[:, None]) & (col[None, :] < c0[:, None] + WIN_W)
    padw = GRID_W - WIN_W
    padded = jnp.pad(rpb.astype(F32), ((0, 0), (0, 0), (padw, padw)))
    bias = jnp.stack([padded[:, :, GRID_W - 1 - q:2 * GRID_W - 1 - q] for q in range(GRID_W)], axis=2)
    b = jnp.where(jnp.asarray(in_win)[None, None], bias, MASKED)
    b = jnp.transpose(b, (1, 0, 2, 3)).reshape(2 * WIN_H - 1, N_HEADS // 2, 2 * GRID_W, GRID_W)
    two = jnp.concatenate([b[:-1], b[1:]], axis=-1)
    return jnp.concatenate([two, jnp.full((1,) + two.shape[1:], MASKED, F32)], axis=0)


def _natten(qr, kr, vb, table, n_ctx):
    bsz, t, d = qr.shape
    nqc = n_ctx // GRID_W
    rows = (t - n_ctx) // GRID_W
    assert rows >= WIN_H
    masked_entry = table.shape[0] - 1

    def r0(s):
        return jnp.clip(s - nqc - WIN_H // 2, 0, rows - WIN_H)

    def entry(s, c):
        return jnp.where(s < nqc, masked_entry, 2 * c + WIN_H - 1 - (s - nqc - r0(s)))

    blk = (1, GRID_W, d)
    kv_specs = [pl.BlockSpec(blk, functools.partial(lambda b, s, i: (b, nqc + r0(s) + i, 0), i=i))
                for i in range(WIN_H)]
    ctx_spec = pl.BlockSpec((1, n_ctx, d), lambda b, s: (b, 0, 0))
    tb_specs = [pl.BlockSpec((1,) + table.shape[1:], functools.partial(lambda b, s, c: (entry(s, c), 0, 0, 0), c=c))
                for c in range(WIN_H // 2)]
    return pl.pallas_call(
        _natten_kernel,
        out_shape=jax.ShapeDtypeStruct((bsz, t, d), BF16),
        grid=(bsz, t // GRID_W),
        in_specs=[pl.BlockSpec(blk, lambda b, s: (b, s, 0))] + kv_specs + kv_specs + [ctx_spec, ctx_spec] + tb_specs,
        out_specs=pl.BlockSpec(blk, lambda b, s: (b, s, 0)),
        compiler_params=_cparams(("parallel", "arbitrary")),
        name="natten",
    )(qr, *([kr] * WIN_H), *([vb] * WIN_H), kr, vb, *([table] * (WIN_H // 2)))


def _block_diag_groups(w, per):
    two, nb, bs, _ = w.shape
    w = w.reshape(two, nb // per, per, bs, bs)
    eye = jnp.eye(per, dtype=w.dtype)
    return jnp.einsum('dgaij,ab->dgaibj', w, eye).reshape(two, nb // per, per * bs, per * bs)


def _rwkv_prep_kernel(*refs, tb, nt, nctx, has_vres):
    (zr, zr_p, zr_n, zk, zk_p, zk_n, zv, zv_p, zv_n, zl, zl_p, zl_n) = refs[:12]
    refs = refs[12:]
    if has_vres:
        vf_ref, refs = refs[0], refs[1:]
    (mu_ref, kk_ref, ka_ref, rk_ref, w0_ref, a0_ref, v0_ref, w2_ref,
     r_o, v_o, kk_o, g_o, bonus_o, lw_o, kd_o, bb_o) = refs
    blk = pl.program_id(1)
    seq_first = (blk == 0) | (blk == nctx)
    seq_last = (blk == nctx - 1) | (blk == nt - 1)
    row = lax.broadcasted_iota(jnp.int32, (tb, 1), 0)

    def nbr_mean(main, prev_row, next_row):
        prev_row = jnp.where(seq_first, 0.0, prev_row)
        next_row = jnp.where(seq_last, 0.0, next_row)
        up = jnp.where(row == 0, prev_row, pltpu.roll(main, 1, 0))
        dn = jnp.where(row == tb - 1, next_row, pltpu.roll(main, tb - 1, 0))
        return 0.5 * (up + dn)

    def shifted(m_ref, p_ref, n_ref, mu):
        main = m_ref[0, 0]
        return main + mu * (nbr_mean(main, p_ref[0, 0, 7:8, :], n_ref[0, 0, 0:1, :]) - main)

    mu = mu_ref[...]
    r = shifted(zr, zr_p, zr_n, mu[0:1])
    k = shifted(zk, zk_p, zk_n, mu[1:2])
    v = shifted(zv, zv_p, zv_n, mu[2:3])

    lr = zl[0, 0, :, 0:LR_W] + nbr_mean(zl[0, 0, :, LR_W:2 * LR_W], zl_p[0, 0, 7:8, LR_W:2 * LR_W],
                                        zl_n[0, 0, 0:1, LR_W:2 * LR_W])
    col = lax.broadcasted_iota(jnp.int32, (1, LR_W), 1)
    act = jnp.where(col < 128, jnp.tanh(lr), jnp.where((col >= 256) & (col < 416), _sigmoid(lr), lr))
    actb = act.astype(BF16)
    s0 = actb[:, 0:256]
    s1 = actb[:, 256:512]
    g_o[0] = _dot(s1, w2_ref[4])
    if has_vres:
        v = v + (vf_ref[0] - v) * _sigmoid(v0_ref[...] + _dot(s1, w2_ref[5]))
    r_o[0] = r
    v_o[0] = v
    kkr = k * kk_ref[...]
    kk = kkr / jnp.maximum(jnp.sqrt(_seg_sum64(kkr * kkr)), 1e-12)
    kk_o[0] = kk
    ksum = jnp.zeros_like(k)
    for dd in range(2):
        zw = w0_ref[dd:dd + 1, :] + _dot(s0, w2_ref[dd])
        lw_o[dd, 0] = -math.exp(-0.5) * _sigmoid(zw)
        a = _sigmoid(a0_ref[dd:dd + 1, :] + _dot(s0, w2_ref[2 + dd]))
        kd = k * (1.0 + (a - 1.0) * ka_ref[...])
        kd_o[dd, 0] = kd
        bb_o[dd, 0] = kk * a
        ksum = ksum + kd
    bonus_o[0] = _seg_sum64(r * ksum * rk_ref[...]) * v


def _rwkv_prep(z, vfirst, mu_rkv, k_k, k_a, r_k, w0, a0, v0, w2cat, n_ctx):
    _, bsz, t, d = z.shape
    tb = 256
    nt = t // tb
    nctx = n_ctx // tb
    hb = tb // 8
    has_vres = vfirst is not None

    def triple(g):
        return [pl.BlockSpec((1, 1, tb, d), lambda b, i: (g, b, i, 0)),
                pl.BlockSpec((1, 1, 8, d), lambda b, i: (g, b, jnp.maximum(i * hb - 1, 0), 0)),
                pl.BlockSpec((1, 1, 8, d), lambda b, i: (g, b, jnp.minimum((i + 1) * hb, t // 8 - 1), 0))]

    tok = pl.BlockSpec((1, tb, d), lambda b, i: (b, i, 0))
    tok2 = pl.BlockSpec((2, 1, tb, d), lambda b, i: (0, b, i, 0))
    const = lambda shape: pl.BlockSpec(shape, lambda b, i: (0,) * len(shape))
    in_specs = triple(G_R) + triple(G_K) + triple(G_V) + triple(G_LR)
    args = [z] * 12
    if has_vres:
        in_specs.append(tok)
        args.append(vfirst)
    in_specs += [const((3, d)), const((1, d)), const((1, d)), const((1, d)), const((2, d)), const((2, d)),
                 const((1, d)), const(w2cat.shape)]
    args += [mu_rkv, k_k.reshape(1, d), k_a.reshape(1, d), r_k.reshape(1, d), w0, a0, v0.reshape(1, d), w2cat]
    one = jax.ShapeDtypeStruct((bsz, t, d), F32)
    two = jax.ShapeDtypeStruct((2, bsz, t, d), F32)
    return pl.pallas_call(
        functools.partial(_rwkv_prep_kernel, tb=tb, nt=nt, nctx=nctx, has_vres=has_vres),
        out_shape=(one, one, one, one, one, two, two, two),
        grid=(bsz, nt),
        in_specs=in_specs,
        out_specs=(tok, tok, tok, tok, tok, tok2, tok2, tok2),
        compiler_params=_cparams(("parallel", "parallel")),
        name="rwkv_prep",
    )(*args)


def _seq_kernel(r_ref, v_ref, kk_ref, lw_ref, kd_ref, bb_ref,
                u_ref, up_ref, un_ref, cw_ref, cb_ref, wa_ref, ba_ref, wx_ref, bx_ref, lam_ref,
                y_ref, hl_ref, s_ref, a_s, b_s, h_s, *, nc, nctx):
    d = pl.program_id(0)
    i = pl.program_id(1)
    bsz, n = r_ref.shape[0], r_ref.shape[1]
    rev = d == 1
    blk = _seq_block(d, i, nc, nctx)

    @pl.when(i == 0)
    def _():
        s_ref[...] = jnp.zeros_like(s_ref)
        h_s[...] = jnp.zeros_like(h_s)

    seq_first = (blk == 0) | (blk == nctx)
    seq_last = (blk == nctx - 1) | (blk == nc - 1)
    row = lax.broadcasted_iota(jnp.int32, (n, 1), 0)
    cw = cw_ref[...]
    nl = -lam_ref[0]
    softplus = jnp.maximum(nl, 0.0) + jnp.log(1.0 + jnp.exp(-jnp.abs(nl)))
    gw = wa_ref.shape[-1]
    for b in range(bsz):
        x = u_ref[0, b]
        prev = jnp.where(seq_first, 0.0, up_ref[0, b])
        nxt = jnp.where(seq_last, 0.0, un_ref[0, b])
        xm1 = jnp.where(row == 0, prev[7:8], pltpu.roll(x, 1, 0))
        xm2 = jnp.where(row == 0, prev[6:7], jnp.where(row == 1, prev[7:8], pltpu.roll(x, 2, 0)))
        xp1 = jnp.where(row == n - 1, nxt[0:1], pltpu.roll(x, n - 1, 0))
        u = cw[0:1] * xm2 + cw[1:2] * xm1 + cw[2:3] * x + cw[3:4] * xp1 + cb_ref[...]
        ub = u.astype(BF16)
        ngrp = u.shape[-1] // gw
        gr = jnp.concatenate([_dot(ub[:, g * gw:(g + 1) * gw], wa_ref[0, g]) for g in range(ngrp)], axis=1)
        gi = jnp.concatenate([_dot(ub[:, g * gw:(g + 1) * gw], wx_ref[0, g]) for g in range(ngrp)], axis=1)
        log_a = -RG_C * _sigmoid(gr + ba_ref[0]) * softplus
        a_s[b] = jnp.exp(log_a)
        b_s[b] = jnp.sqrt(1.0 - jnp.exp(2.0 * log_a)) * (_sigmoid(gi + bx_ref[0]) * u)

    hs = [h_s[b] for b in range(bsz)]
    for t in range(n):
        tt = jnp.where(rev, n - 1 - t, t)
        for b in range(bsz):
            hs[b] = a_s[b, pl.ds(tt, 1), :] * hs[b] + b_s[b, pl.ds(tt, 1), :]
            hl_ref[0, b, pl.ds(tt, 1), :] = hs[b]
    for b in range(bsz):
        h_s[b] = hs[b]

    ti = lax.broadcasted_iota(jnp.int32, (n, n), 0)
    tj = lax.broadcasted_iota(jnp.int32, (n, n), 1)
    order = jnp.where(rev, tj - ti, ti - tj)
    tri = jnp.where(order >= 0, 1.0, 0.0).astype(BF16)
    wide = []
    gls = []
    for b in range(bsz):
        lw = lw_ref[0, b]
        h1 = lw.astype(BF16)
        r1 = lw - h1.astype(F32)
        h2 = r1.astype(BF16)
        h3 = (r1 - h2.astype(F32)).astype(BF16)
        c = _dot(tri, h1) + _dot(tri, h2) + _dot(tri, h3)
        c_last = jnp.where(rev, c[0:1], c[n - 1:n])
        kd = kd_ref[0, b]
        bb = bb_ref[0, b]
        enc = jnp.exp(-c)
        el = jnp.exp(c_last - c)
        wide.append([t.astype(BF16) for t in (-kk_ref[b] * jnp.exp(c - lw), r_ref[b] * jnp.exp(c), bb * enc, kd * enc,
                                              v_ref[b], bb * el, kd * el)])
        gls.append(jnp.exp(c_last))

    lane = lax.broadcasted_iota(jnp.int32, (n, PAIR), 1)
    first = lane < HEAD_DIM
    si = lax.broadcasted_iota(jnp.int32, (2 * n, 2 * n), 0) & (n - 1)
    sj = lax.broadcasted_iota(jnp.int32, (2 * n, 2 * n), 1) & (n - 1)
    row2 = lax.broadcasted_iota(jnp.int32, (2 * n, 2 * n), 0)
    col2 = lax.broadcasted_iota(jnp.int32, (2 * n, 2 * n), 1)
    qi = jnp.where(rev, n - 1 - si, si)
    qj = jnp.where(rev, n - 1 - sj, sj)
    strict = qi > qj
    incl = qi >= qj
    eye = jnp.where(row2 == col2, 1.0, 0.0)
    levels = []
    blk = 1
    while blk < n:
        bi = qi // blk
        bj = qj // blk
        levels.append((bi // 2 == bj // 2) & (bi % 2 == 1) & (bj % 2 == 0))
        blk *= 2

    zero = jnp.zeros((n, PAIR), BF16)

    def stack(x, sl):
        return jnp.concatenate([jnp.where(first, x[:, sl], zero), jnp.where(first, zero, x[:, sl])], axis=0)

    zero2 = jnp.zeros((2 * n, 2 * n), BF16)
    chains = [(b, p) for b in range(bsz) for p in range(N_HEADS // 2)]
    for g0 in range(0, len(chains), SCAN_GROUP):
        grp = chains[g0:g0 + SCAN_GROUP]
        sls = [slice(p * PAIR, (p + 1) * PAIR) for _, p in grp]
        opnd = [[stack(t, sl) for t in wide[b]] for (b, _), sl in zip(grp, sls)]
        s2 = [s_ref[b, p] for b, p in grp]
        s2b = [s.astype(BF16) for s in s2]
        aa = [_dot_nt(jnp.concatenate([am, rm], axis=0), jnp.concatenate([bm, km], axis=0))
              for am, rm, bm, km, _, _, _ in opnd]
        a_ab = [jnp.where(strict, a[:2 * n, :2 * n], 0.0) for a in aa]
        a_abb = [a.astype(BF16) for a in a_ab]
        a_ak = [jnp.where(strict, a[:2 * n, 2 * n:], 0.0).astype(BF16) for a in aa]
        a_r = [jnp.concatenate([jnp.where(incl, a[2 * n:, :2 * n], 0.0), jnp.where(incl, a[2 * n:, 2 * n:], 0.0)],
                               axis=1).astype(BF16) for a in aa]
        x = [_dot_nt(o[0], sb) + _dot(ak, o[4]) for o, sb, ak in zip(opnd, s2b, a_ak)]
        tinv = [eye + jnp.where(levels[0], a, 0.0) for a in a_ab]
        for m in levels[1:]:
            tb = [t.astype(BF16) for t in tinv]
            half = [_dot(t, jnp.where(m, a, zero2)) for t, a in zip(tb, a_abb)]
            tinv = [t + _dot(h.astype(BF16), q) for t, h, q in zip(tinv, half, tb)]
        uv = [jnp.concatenate([_dot(t.astype(BF16), xx.astype(BF16)).astype(BF16), o[4]], axis=0)
              for t, xx, o in zip(tinv, x, opnd)]
        for j, (b, p) in enumerate(grp):
            rm, blm, klm = opnd[j][1], opnd[j][5], opnd[j][6]
            o = _dot_nt(rm, s2b[j]) + _dot(a_r[j], uv[j])
            y_ref[0, b, :, sls[j]] = o[:n] + o[n:]
            s_ref[b, p] = s2[j] * gls[b][:, sls[j]] + _dot_tn(uv[j], jnp.concatenate([blm, klm], axis=0))


def _seq_mixers(r, v, kk, lw, kd, bb, z, conv_w, conv_b, wa, ba, wx, bx, lam, n_ctx):
    bsz, t, d = r.shape
    nc = t // CHUNK
    nctx = n_ctx // CHUNK
    hb = CHUNK // 8
    gw = wa.shape[-1]

    def blk_of(dd, i):
        return _seq_block(dd, i, nc, nctx)

    tok = pl.BlockSpec((bsz, CHUNK, d), lambda dd, i: (0, blk_of(dd, i), 0))
    tok2 = pl.BlockSpec((1, bsz, CHUNK, d), lambda dd, i: (dd, 0, blk_of(dd, i), 0))
    vec = lambda: pl.BlockSpec((1, 1, d), lambda dd, i: (dd, 0, 0))
    wspec = lambda: pl.BlockSpec((1, d // gw, gw, gw), lambda dd, i: (dd, 0, 0, 0))
    out = jax.ShapeDtypeStruct((2, bsz, t, d), F32)
    return pl.pallas_call(
        functools.partial(_seq_kernel, nc=nc, nctx=nctx),
        out_shape=(out, out),
        grid=(2, nc),
        in_specs=[tok, tok, tok, tok2, tok2, tok2,
                  pl.BlockSpec((1, bsz, CHUNK, d), lambda dd, i: (G_XB, 0, blk_of(dd, i), 0)),
                  pl.BlockSpec((1, bsz, 8, d), lambda dd, i: (G_XB, 0, jnp.maximum(blk_of(dd, i) * hb - 1, 0), 0)),
                  pl.BlockSpec((1, bsz, 8, d),
                               lambda dd, i: (G_XB, 0, jnp.minimum((blk_of(dd, i) + 1) * hb, t // 8 - 1), 0)),
                  pl.BlockSpec((4, d), lambda dd, i: (0, 0)),
                  pl.BlockSpec((1, d), lambda dd, i: (0, 0)),
                  wspec(), vec(), wspec(), vec(), vec()],
        out_specs=(tok2, tok2),
        scratch_shapes=[pltpu.VMEM((bsz, N_HEADS // 2, PAIR, PAIR), F32),
                        pltpu.VMEM((bsz, CHUNK, d), F32), pltpu.VMEM((bsz, CHUNK, d), F32),
                        pltpu.VMEM((bsz, 1, d), F32)],
        compiler_params=_cparams(("arbitrary", "arbitrary")),
        name="seq_mixers",
    )(r, v, kk, lw, kd, bb, z, z, z, conv_w, conv_b.reshape(1, d), wa, ba.reshape(2, 1, d), wx, bx.reshape(2, 1, d),
      lam.reshape(2, 1, d))


def _gelu_tanh(x):
    return 0.5 * x * (1.0 + jnp.tanh(math.sqrt(2.0 / math.pi) * (x + 0.044715 * (x * x * x))))


def _merge_kernel(x_ref, ya_ref, hl_ref, ug_ref, yw_ref, bonus_ref, g_ref, ma_ref, mb_ref, mc_ref,
                  mod_ref, lnw_ref, lnb_ref, wa_ref, wb_ref, wc_ref, wo_ref, o_ref, *, tm, n_ctx):
    i = pl.program_id(1)
    yb = ((hl_ref[0, 0] + hl_ref[1, 0]) * _gelu_tanh(ug_ref[0, 0])).astype(BF16)
    y = yw_ref[0, 0] + yw_ref[1, 0]
    inv = 1.0 / HEAD_DIM
    mean = _seg_sum64(y) * inv
    yc = y - mean
    var = _seg_sum64(yc * yc) * inv
    yn = yc * lax.rsqrt(var + LNX_EPS) * lnw_ref[...] + lnb_ref[...]
    oc = ((yn + bonus_ref[0]) * g_ref[0]).astype(BF16)
    mixed = (_sigmoid(ma_ref[0, 0]) * _dot(ya_ref[0], wa_ref[...])
             + _sigmoid(mb_ref[0, 0]) * _dot(yb, wb_ref[...])
             + _sigmoid(mc_ref[0, 0]) * _dot(oc, wc_ref[...]))
    yo = _dot(mixed.astype(BF16), wo_ref[...])
    rows = i * tm + lax.broadcasted_iota(jnp.int32, (tm, 1), 0)
    o_ref[0] = x_ref[0] + _row_select(mod_ref, 2, rows, n_ctx) * yo


def _merge(x, ynat, hlru, z, yrw, bonus, g, mods, ln_w, ln_b, wa, wb, wc, wo, n_ctx):
    bsz, t, d = x.shape
    tm = 256
    tok = pl.BlockSpec((1, tm, d), lambda b, i: (b, i, 0))
    tok2 = pl.BlockSpec((2, 1, tm, d), lambda b, i: (0, b, i, 0))
    zspec = lambda gidx: pl.BlockSpec((1, 1, tm, d), lambda b, i: (gidx, b, i, 0))
    vec = pl.BlockSpec((1, d), lambda b, i: (0, 0))
    wspec = pl.BlockSpec((d, d), lambda b, i: (0, 0))
    return pl.pallas_call(
        functools.partial(_merge_kernel, tm=tm, n_ctx=n_ctx),
        out_shape=jax.ShapeDtypeStruct((bsz, t, d), F32),
        grid=(bsz, t // tm),
        in_specs=[tok, tok, tok2, zspec(G_GB), tok2, tok, tok, zspec(G_MA), zspec(G_MB), zspec(G_MC),
                  pl.BlockSpec((1, 2, 6, d), lambda b, i: (b, 0, 0, 0)), vec, vec,
                  wspec, wspec, wspec, wspec],
        out_specs=tok,
        compiler_params=_cparams(("parallel", "parallel")),
        name="merge",
    )(x, ynat, hlru, z, yrw, bonus, g, z, z, z, mods, ln_w.reshape(1, d), ln_b.reshape(1, d), wa, wb, wc, wo)


def _mlp_kernel(x_ref, g_ref, mod_ref, w1_ref, w2_ref, fg_ref, o_ref, hb_ref, acc_ref, *, tm, n_ctx, final):
    i = pl.program_id(1)
    k = pl.program_id(2)
    rows = i * tm + lax.broadcasted_iota(jnp.int32, (tm, 1), 0)

    @pl.when(k == 0)
    def _():
        x = x_ref[0]
        ms = jnp.mean(x * x, axis=-1, keepdims=True)
        y = x * lax.rsqrt(ms + NORM_EPS) * g_ref[...]
        sh = _row_select(mod_ref, 3, rows, n_ctx)
        sc = _row_select(mod_ref, 4, rows, n_ctx)
        hb_ref[...] = (y * (1.0 + sc) + sh).astype(BF16)
        acc_ref[...] = jnp.zeros_like(acc_ref)

    a = jnp.maximum(_dot(hb_ref[...], w1_ref[0]), 0.0)
    acc_ref[...] += _dot((a * a).astype(BF16), w2_ref[...])

    @pl.when(k == pl.num_programs(2) - 1)
    def _():
        xn = x_ref[0] + _row_select(mod_ref, 5, rows, n_ctx) * acc_ref[...]
        if final:
            ms = jnp.mean(xn * xn, axis=-1, keepdims=True)
            xn = xn * lax.rsqrt(ms + NORM_EPS) * fg_ref[...]
        o_ref[0] = xn


def _mlp(x, g, mods, w1, w2, final_g, n_ctx, final):
    bsz, t, d = x.shape
    nk, _, tk = w1.shape
    dff = nk * tk
    tm = _pick(t, (1408, 768, 256))
    vec = pl.BlockSpec((1, d), lambda b, i, k: (0, 0))
    tok = pl.BlockSpec((1, tm, d), lambda b, i, k: (b, i, 0))
    return pl.pallas_call(
        functools.partial(_mlp_kernel, tm=tm, n_ctx=n_ctx, final=final),
        out_shape=jax.ShapeDtypeStruct((bsz, t, d), F32),
        grid=(bsz, t // tm, dff // tk),
        in_specs=[tok, vec, pl.BlockSpec((1, 2, 6, d), lambda b, i, k: (b, 0, 0, 0)),
                  pl.BlockSpec((1, d, tk), lambda b, i, k: (k, 0, 0)),
                  pl.BlockSpec((tk, d), lambda b, i, k: (k, 0)), vec],
        out_specs=tok,
        scratch_shapes=[pltpu.VMEM((tm, d), BF16), pltpu.VMEM((tm, d), F32)],
        compiler_params=_cparams(("parallel", "parallel", "arbitrary")),
        name="mlp",
    )(x, g.reshape(1, d), mods, w1, w2, final_g.reshape(1, d))


def _column_tiles(w, tn):
    k, n = w.shape
    return jnp.transpose(w.reshape(k, n // tn, tn), (1, 0, 2))


def _lowrank_weights(mu_h, w1, a1, g1, v1, mu_v):
    d = mu_h.shape[-1]
    blocks = [(w1[0], mu_h[0]), (w1[1], mu_h[0]), (a1[0], mu_h[1]), (a1[1], mu_h[1]), (g1, mu_h[2])]
    if v1 is not None:
        blocks.append((v1, mu_v))
    p = jnp.concatenate([w * (1.0 - m)[:, None] for w, m in blocks], axis=1)
    q = jnp.concatenate([w * m[:, None] for w, m in blocks], axis=1)
    pad = lambda a: jnp.pad(a, ((0, 0), (0, LR_W - a.shape[1])))
    return jnp.concatenate([pad(p), pad(q)], axis=1)


def _second_stage_weights(w2, a2, g2, v2):
    d = w2.shape[-1]
    def place(w, off):
        return jnp.zeros((256, d), F32).at[off:off + w.shape[0]].set(w)
    v2p = place(v2, 160) if v2 is not None else jnp.zeros((256, d), F32)
    return jnp.stack([place(w2[0], 0), place(w2[1], 64), place(a2[0], 128), place(a2[1], 192),
                      place(g2, 0), v2p]).astype(BF16)


def kernel(x, c, ctx, c_ctx, w_ada, b_ada, norm1_g, norm2_g, w_in, rpb, conv_w, conv_b, rg_wa, rg_ba, rg_wx, rg_bx, rg_lam, rw_mu_rkv, rw_mu_h, rw_w0, rw_w1, rw_w2, rw_a0, rw_a1, rw_a2, rw_g1, rw_g2, rw_k_k, rw_k_a, rw_r_k, rw_ln_w, rw_ln_b, vres_v0, vres_v1, vres_v2, vres_mu, w_br_a, w_br_b, w_br_c, w_out, w_ff1, w_ff2, final_g):
    bsz, seq, d = x.shape
    n_ctx = ctx.shape[1]
    depth = w_in.shape[0]
    t = n_ctx + seq
    xs = jnp.concatenate([ctx, x], axis=1)
    cvec = jnp.zeros((8, d), F32).at[:bsz].set(c).at[bsz].set(c_ctx)
    tables = _rope_tables(t, n_ctx)
    vfirst = None
    for i in range(depth):
        last = i == depth - 1
        mod = _ada(cvec, w_ada[i], b_ada[i])
        mods = jnp.stack([jnp.broadcast_to(mod[bsz].reshape(1, 6, d), (bsz, 6, d)),
                          mod[:bsz].reshape(bsz, 6, d)], axis=1)
        if i == 0:
            lr_w = _lowrank_weights(rw_mu_h[i], rw_w1[i], rw_a1[i], rw_g1[i], None, None)
            w2cat = _second_stage_weights(rw_w2[i], rw_a2[i], rw_g2[i], None)
            v0 = jnp.zeros((d,), F32)
        else:
            lr_w = _lowrank_weights(rw_mu_h[i], rw_w1[i], rw_a1[i], rw_g1[i], vres_v1[i - 1], vres_mu[i - 1])
            w2cat = _second_stage_weights(rw_w2[i], rw_a2[i], rw_g2[i], vres_v2[i - 1])
            v0 = vres_v0[i - 1]
        w_all = _column_tiles(jnp.concatenate([w_in[i], lr_w], axis=1).astype(BF16), d)
        z, qr, kr, vb = _inproj(xs, norm1_g[i], mods, w_all, tables, n_ctx)
        ynat = _natten(qr, kr, vb, _natten_table(rpb[i]), n_ctx)
        r, v, kk, g, bonus, lw, kd, bb = _rwkv_prep(
            z, vfirst, rw_mu_rkv[i], rw_k_k[i], rw_k_a[i], rw_r_k[i].reshape(-1), rw_w0[i], rw_a0[i], v0,
            w2cat, n_ctx)
        if i == 0:
            vfirst = v
        yrw, hlru = _seq_mixers(r, v, kk, lw, kd, bb, z, conv_w[i], conv_b[i],
                                _block_diag_groups(rg_wa[i], 4).astype(BF16), rg_ba[i],
                                _block_diag_groups(rg_wx[i], 4).astype(BF16), rg_bx[i], rg_lam[i], n_ctx)
        xs = _merge(xs, ynat, hlru, z, yrw, bonus, g, mods, rw_ln_w[i], rw_ln_b[i],
                    w_br_a[i].astype(BF16), w_br_b[i].astype(BF16), w_br_c[i].astype(BF16),
                    w_out[i].astype(BF16), n_ctx)
        xs = _mlp(xs, norm2_g[i], mods, _column_tiles(w_ff1[i].astype(BF16), d), w_ff2[i].astype(BF16), final_g, n_ctx, last)
    return xs[:, n_ctx:]
```

```python
import functools
import math

import jax
import jax.numpy as jnp
import numpy as np
from jax import lax
from jax.experimental import pallas as pl
from jax.experimental.pallas import tpu as pltpu

F32 = jnp.float32
BF16 = jnp.bfloat16

D_MODEL = 1024
GRID_W = 64
N_HEADS = 16
HEAD_DIM = 64
WIN_H = 8
WIN_W = 16
ROPE_BASE = 10000.0
NB_B = 16
BS_B = 64
RG_C = 8.0
LNX_EPS = 64e-5
NORM_EPS = 1e-6
NEG_INF = -1e30
MASKED = -1e30
N_ATT = 3
G_XB, G_GB, G_R, G_K, G_V, G_MA, G_MB, G_MC, G_LR = range(9)
LR_W = 512
CHUNK = 64
VMEM_LIMIT = 58 * 1024 * 1024
PAIR = 2 * HEAD_DIM
SCAN_GROUP = 16


def _cparams(sem):
    return pltpu.CompilerParams(dimension_semantics=sem, vmem_limit_bytes=VMEM_LIMIT)


def _pick(n, cands):
    for c in cands:
        if n % c == 0:
            return c
    raise ValueError(f"no tile for {n}")


def _dot(a, b):
    return jnp.dot(a, b, preferred_element_type=F32)


def _dot_nt(a, b):
    return lax.dot_general(a, b, (((1,), (1,)), ((), ())), preferred_element_type=F32)


def _dot_tn(a, b):
    return lax.dot_general(a, b, (((0,), (0,)), ((), ())), preferred_element_type=F32)


def _sigmoid(x):
    return 0.5 * jnp.tanh(0.5 * x) + 0.5


def _row_select(mod_ref, idx, rows, n_ctx):
    return jnp.where(rows < n_ctx, mod_ref[0, 0, idx:idx + 1, :], mod_ref[0, 1, idx:idx + 1, :])


def _seg_sum64(x):
    h1 = x.astype(BF16)
    r1 = x - h1.astype(F32)
    h2 = r1.astype(BF16)
    h3 = (r1 - h2.astype(F32)).astype(BF16)
    gi = lax.broadcasted_iota(jnp.int32, (3 * PAIR, PAIR), 0) % PAIR
    gj = lax.broadcasted_iota(jnp.int32, (3 * PAIR, PAIR), 1)
    ones = jnp.where(gi // HEAD_DIM == gj // HEAD_DIM, 1.0, 0.0).astype(BF16)
    out = []
    for p in range(x.shape[-1] // PAIR):
        sl = slice(p * PAIR, (p + 1) * PAIR)
        out.append(_dot(jnp.concatenate([h1[:, sl], h2[:, sl], h3[:, sl]], axis=1), ones))
    return jnp.concatenate(out, axis=1)


def _seq_block(d, i, nt, nctx):
    rev = jnp.where(i < nctx, nctx - 1 - i, nt - 1 - (i - nctx))
    return jnp.where(d == 0, i, rev)


def _ada_kernel(a_ref, w_ref, b_ref, o_ref):
    a = a_ref[...]
    s = a * _sigmoid(a)
    o_ref[...] = _dot(s.astype(BF16), w_ref[...].astype(BF16)) + b_ref[...]


def _ada(cvec, w, b):
    m, k = cvec.shape
    n = w.shape[1]
    tn = _pick(n, (1536, 1024, 512, 128))
    return pl.pallas_call(
        _ada_kernel,
        out_shape=jax.ShapeDtypeStruct((m, n), F32),
        grid=(n // tn,),
        in_specs=[pl.BlockSpec((m, k), lambda j: (0, 0)),
                  pl.BlockSpec((k, tn), lambda j: (0, j)),
                  pl.BlockSpec((1, tn), lambda j: (0, j))],
        out_specs=pl.BlockSpec((m, tn), lambda j: (0, j)),
        compiler_params=_cparams(("arbitrary",)),
        name="ada",
    )(cvec, w, b.reshape(1, n))


def _inproj_kernel(x_ref, g_ref, mod_ref, w_ref, c_ref, s1_ref, s2_ref, z_ref, q_ref, k_ref, v_ref, hb_ref,
                   *, tm, n_ctx):
    i = pl.program_id(1)
    j = pl.program_id(2)

    @pl.when(j == 0)
    def _():
        x = x_ref[0]
        ms = jnp.mean(x * x, axis=-1, keepdims=True)
        y = x * lax.rsqrt(ms + NORM_EPS) * g_ref[...]
        rows = i * tm + lax.broadcasted_iota(jnp.int32, (tm, 1), 0)
        sh = _row_select(mod_ref, 0, rows, n_ctx)
        sc = _row_select(mod_ref, 1, rows, n_ctx)
        hb_ref[...] = (y * (1.0 + sc) + sh).astype(BF16)

    z_ref[0, 0] = _dot(hb_ref[...], w_ref[...])
    d = z_ref.shape[-1]
    quarter = HEAD_DIM // 4

    def rope():
        cos, s1, s2 = c_ref[...], s1_ref[...], s2_ref[...]
        out = []
        for p in range(d // PAIR):
            xp = z_ref[0, 0, :, p * PAIR:(p + 1) * PAIR]
            out.append(xp * cos + pltpu.roll(xp, PAIR - quarter, 1) * s1 + pltpu.roll(xp, quarter, 1) * s2)
        return jnp.concatenate(out, axis=1)

    @pl.when(j == 0)
    def _():
        q_ref[0] = (rope() * (HEAD_DIM ** -0.5)).astype(BF16)

    @pl.when(j == 1)
    def _():
        k_ref[0] = rope().astype(BF16)

    @pl.when(j == 2)
    def _():
        v_ref[0] = z_ref[0, 0].astype(BF16)


def _inproj(x, g, mods, w, tables, n_ctx):
    bsz, t, d = x.shape
    ngrp = w.shape[1] // d
    tm = _pick(t, (1408, 768, 256))
    tspec = pl.BlockSpec((tm, PAIR), lambda b, i, j: (i, 0))
    aspec = pl.BlockSpec((1, tm, d), lambda b, i, j: (b, i, 0))
    ashape = jax.ShapeDtypeStruct((bsz, t, d), BF16)
    return pl.pallas_call(
        functools.partial(_inproj_kernel, tm=tm, n_ctx=n_ctx),
        out_shape=(jax.ShapeDtypeStruct((ngrp - N_ATT, bsz, t, d), F32), ashape, ashape, ashape),
        grid=(bsz, t // tm, ngrp),
        in_specs=[pl.BlockSpec((1, tm, d), lambda b, i, j: (b, i, 0)),
                  pl.BlockSpec((1, d), lambda b, i, j: (0, 0)),
                  pl.BlockSpec((1, 2, 6, d), lambda b, i, j: (b, 0, 0, 0)),
                  pl.BlockSpec((d, d), lambda b, i, j: (0, j)),
                  tspec, tspec, tspec],
        out_specs=(pl.BlockSpec((1, 1, tm, d), lambda b, i, j: (jnp.maximum(j - N_ATT, 0), b, i, 0)),
                   aspec, aspec, aspec),
        scratch_shapes=[pltpu.VMEM((tm, d), BF16)],
        compiler_params=_cparams(("parallel", "parallel", "arbitrary")),
        name="inproj",
    )(x, g.reshape(1, d), mods, w, *tables)


def _rope_tables(t, n_ctx):
    nf = HEAD_DIM // 4
    inv_freq = ROPE_BASE ** (-jnp.arange(nf, dtype=F32) / nf)
    tok = jnp.arange(t) - n_ctx
    row = (tok // GRID_W).astype(F32)
    col = (tok % GRID_W).astype(F32)
    lane = np.arange(PAIR) % HEAD_DIM
    use_col = jnp.asarray(lane >= HEAD_DIM // 2)
    first = jnp.asarray((lane % (HEAD_DIM // 2)) < nf)
    pos = jnp.where(use_col[None, :], col[:, None], row[:, None])
    ang = pos * inv_freq[lane % nf][None, :]
    is_lat = (tok >= 0)[:, None]
    cos = jnp.where(is_lat, jnp.cos(ang), 1.0)
    sin = jnp.where(is_lat, jnp.sin(ang), 0.0)
    return cos, jnp.where(first[None, :], -sin, 0.0), jnp.where(first[None, :], 0.0, sin)


def _natten_kernel(q_ref, *refs):
    k_refs = refs[0:WIN_H]
    v_refs = refs[WIN_H:2 * WIN_H]
    kc_ref, vc_ref = refs[2 * WIN_H:2 * WIN_H + 2]
    tb_refs = refs[2 * WIN_H + 2:-1]
    o_ref = refs[-1]
    nq = q_ref.shape[1]
    nloc = WIN_H * GRID_W
    lane = lax.broadcasted_iota(jnp.int32, (nq, PAIR), 1)
    first = lane < HEAD_DIM
    zero = jnp.zeros((nq, PAIR), BF16)
    sls = [slice(p * PAIR, (p + 1) * PAIR) for p in range(N_HEADS // 2)]
    qs = [jnp.concatenate([jnp.where(first, q_ref[0, :, sl], zero), jnp.where(first, zero, q_ref[0, :, sl])], axis=0)
          for sl in sls]
    ks = [jnp.concatenate([r[0, :, sl] for r in k_refs] + [kc_ref[0, :, sl]], axis=0) for sl in sls]
    scores = [_dot_nt(q, k) for q, k in zip(qs, ks)]
    probs = []
    invs = []
    for p, s in enumerate(scores):
        tb = jnp.concatenate([t[0, p] for t in tb_refs], axis=1)
        sl_ = s[:, :nloc] + tb
        sc_ = s[:, nloc:]
        m = jnp.maximum(jnp.max(sl_, axis=-1, keepdims=True), jnp.max(sc_, axis=-1, keepdims=True))
        el_ = jnp.exp(sl_ - m)
        ec_ = jnp.exp(sc_ - m)
        invs.append(1.0 / (jnp.sum(el_, axis=-1, keepdims=True) + jnp.sum(ec_, axis=-1, keepdims=True)))
        probs.append(jnp.concatenate([el_.astype(BF16), ec_.astype(BF16)], axis=1))
    vs = [jnp.concatenate([r[0, :, sl] for r in v_refs] + [vc_ref[0, :, sl]], axis=0) for sl in sls]
    outs = [_dot(pr, v) * inv for pr, v, inv in zip(probs, vs, invs)]
    for sl, o in zip(sls, outs):
        o_ref[0, :, sl] = jnp.where(first, o[:nq], o[nq:]).astype(BF16)


def _natten_table(rpb):
    col = np.arange(GRID_W)
    c0 = np.clip(col - WIN_W // 2, 0, GRID_W - WIN_W)
    in_win = (col[None, :] >= c0[:, None]) & (col[None, :] < c0[:, None] + WIN_W)
    padw = GRID_W - WIN_W
    padded = jnp.pad(rpb.astype(F32), ((0, 0), (0, 0), (padw, padw)))
    bias = jnp.stack([padded[:, :, GRID_W - 1 - q:2 * GRID_W - 1 - q] for q in range(GRID_W)], axis=2)
    b = jnp.where(jnp.asarray(in_win)[None, None], bias, MASKED)
    b = jnp.transpose(b, (1, 0, 2, 3)).reshape(2 * WIN_H - 1, N_HEADS // 2, 2 * GRID_W, GRID_W)
    two = jnp.concatenate([b[:-1], b[1:]], axis=-1)
    return jnp.concatenate([two, jnp.full((1,) + two.shape[1:], MASKED, F32)], axis=0)


def _natten(qr, kr, vb, table, n_ctx):
    bsz, t, d = qr.shape
    nqc = n_ctx // GRID_W
    rows = (t - n_ctx) // GRID_W
    assert rows >= WIN_H
    masked_entry = table.shape[0] - 1

    def r0(s):
        return jnp.clip(s - nqc - WIN_H // 2, 0, rows - WIN_H)

    def entry(s, c):
        return jnp.where(s < nqc, masked_entry, 2 * c + WIN_H - 1 - (s - nqc - r0(s)))

    blk = (1, GRID_W, d)
    kv_specs = [pl.BlockSpec(blk, functools.partial(lambda b, s, i: (b, nqc + r0(s) + i, 0), i=i))
                for i in range(WIN_H)]
    ctx_spec = pl.BlockSpec((1, n_ctx, d), lambda b, s: (b, 0, 0))
    tb_specs = [pl.BlockSpec((1,) + table.shape[1:], functools.partial(lambda b, s, c: (entry(s, c), 0, 0, 0), c=c))
                for c in range(WIN_H // 2)]
    return pl.pallas_call(
        _natten_kernel,
        out_shape=jax.ShapeDtypeStruct((bsz, t, d), BF16),
        grid=(bsz, t // GRID_W),
        in_specs=[pl.BlockSpec(blk, lambda b, s: (b, s, 0))] + kv_specs + kv_specs + [ctx_spec, ctx_spec] + tb_specs,
        out_specs=pl.BlockSpec(blk, lambda b, s: (b, s, 0)),
        compiler_params=_cparams(("parallel", "arbitrary")),
        name="natten",
    )(qr, *([kr] * WIN_H), *([vb] * WIN_H), kr, vb, *([table] * (WIN_H // 2)))


def _block_diag_groups(w, per):
    two, nb, bs, _ = w.shape
    w = w.reshape(two, nb // per, per, bs, bs)
    eye = jnp.eye(per, dtype=w.dtype)
    return jnp.einsum('dgaij,ab->dgaibj', w, eye).reshape(two, nb // per, per * bs, per * bs)


def _rwkv_prep_kernel(*refs, tb, nt, nctx, has_vres):
    (zr, zr_p, zr_n, zk, zk_p, zk_n, zv, zv_p, zv_n, zl, zl_p, zl_n) = refs[:12]
    refs = refs[12:]
    if has_vres:
        vf_ref, refs = refs[0], refs[1:]
    (mu_ref, kk_ref, ka_ref, rk_ref, w0_ref, a0_ref, v0_ref, w2_ref,
     r_o, v_o, kk_o, g_o, bonus_o, lw_o, kd_o, bb_o) = refs
    blk = pl.program_id(1)
    seq_first = (blk == 0) | (blk == nctx)
    seq_last = (blk == nctx - 1) | (blk == nt - 1)
    row = lax.broadcasted_iota(jnp.int32, (tb, 1), 0)

    def nbr_mean(main, prev_row, next_row):
        prev_row = jnp.where(seq_first, 0.0, prev_row)
        next_row = jnp.where(seq_last, 0.0, next_row)
        up = jnp.where(row == 0, prev_row, pltpu.roll(main, 1, 0))
        dn = jnp.where(row == tb - 1, next_row, pltpu.roll(main, tb - 1, 0))
        return 0.5 * (up + dn)

    def shifted(m_ref, p_ref, n_ref, mu):
        main = m_ref[0, 0]
        return main + mu * (nbr_mean(main, p_ref[0, 0, 7:8, :], n_ref[0, 0, 0:1, :]) - main)

    mu = mu_ref[...]
    r = shifted(zr, zr_p, zr_n, mu[0:1])
    k = shifted(zk, zk_p, zk_n, mu[1:2])
    v = shifted(zv, zv_p, zv_n, mu[2:3])

    lr = zl[0, 0, :, 0:LR_W] + nbr_mean(zl[0, 0, :, LR_W:2 * LR_W], zl_p[0, 0, 7:8, LR_W:2 * LR_W],
                                        zl_n[0, 0, 0:1, LR_W:2 * LR_W])
    col = lax.broadcasted_iota(jnp.int32, (1, LR_W), 1)
    act = jnp.where(col < 128, jnp.tanh(lr), jnp.where((col >= 256) & (col < 416), _sigmoid(lr), lr))
    actb = act.astype(BF16)
    s0 = actb[:, 0:256]
    s1 = actb[:, 256:512]
    g_o[0] = _dot(s1, w2_ref[4])
    if has_vres:
        v = v + (vf_ref[0] - v) * _sigmoid(v0_ref[...] + _dot(s1, w2_ref[5]))
    r_o[0] = r
    v_o[0] = v
    kkr = k * kk_ref[...]
    kk = kkr / jnp.maximum(jnp.sqrt(_seg_sum64(kkr * kkr)), 1e-12)
    kk_o[0] = kk
    ksum = jnp.zeros_like(k)
    for dd in range(2):
        zw = w0_ref[dd:dd + 1, :] + _dot(s0, w2_ref[dd])
        lw_o[dd, 0] = -math.exp(-0.5) * _sigmoid(zw)
        a = _sigmoid(a0_ref[dd:dd + 1, :] + _dot(s0, w2_ref[2 + dd]))
        kd = k * (1.0 + (a - 1.0) * ka_ref[...])
        kd_o[dd, 0] = kd
        bb_o[dd, 0] = kk * a
        ksum = ksum + kd
    bonus_o[0] = _seg_sum64(r * ksum * rk_ref[...]) * v


def _rwkv_prep(z, vfirst, mu_rkv, k_k, k_a, r_k, w0, a0, v0, w2cat, n_ctx):
    _, bsz, t, d = z.shape
    tb = 256
    nt = t // tb
    nctx = n_ctx // tb
    hb = tb // 8
    has_vres = vfirst is not None

    def triple(g):
        return [pl.BlockSpec((1, 1, tb, d), lambda b, i: (g, b, i, 0)),
                pl.BlockSpec((1, 1, 8, d), lambda b, i: (g, b, jnp.maximum(i * hb - 1, 0), 0)),
                pl.BlockSpec((1, 1, 8, d), lambda b, i: (g, b, jnp.minimum((i + 1) * hb, t // 8 - 1), 0))]

    tok = pl.BlockSpec((1, tb, d), lambda b, i: (b, i, 0))
    tok2 = pl.BlockSpec((2, 1, tb, d), lambda b, i: (0, b, i, 0))
    const = lambda shape: pl.BlockSpec(shape, lambda b, i: (0,) * len(shape))
    in_specs = triple(G_R) + triple(G_K) + triple(G_V) + triple(G_LR)
    args = [z] * 12
    if has_vres:
        in_specs.append(tok)
        args.append(vfirst)
    in_specs += [const((3, d)), const((1, d)), const((1, d)), const((1, d)), const((2, d)), const((2, d)),
                 const((1, d)), const(w2cat.shape)]
    args += [mu_rkv, k_k.reshape(1, d), k_a.reshape(1, d), r_k.reshape(1, d), w0, a0, v0.reshape(1, d), w2cat]
    one = jax.ShapeDtypeStruct((bsz, t, d), F32)
    two = jax.ShapeDtypeStruct((2, bsz, t, d), F32)
    return pl.pallas_call(
        functools.partial(_rwkv_prep_kernel, tb=tb, nt=nt, nctx=nctx, has_vres=has_vres),
        out_shape=(one, one, one, one, one, two, two, two),
        grid=(bsz, nt),
        in_specs=in_specs,
        out_specs=(tok, tok, tok, tok, tok, tok2, tok2, tok2),
        compiler_params=_cparams(("parallel", "parallel")),
        name="rwkv_prep",
    )(*args)


def _seq_kernel(r_ref, v_ref, kk_ref, lw_ref, kd_ref, bb_ref,
                u_ref, up_ref, un_ref, cw_ref, cb_ref, wa_ref, ba_ref, wx_ref, bx_ref, lam_ref,
                y_ref, hl_ref, s_ref, a_s, b_s, h_s, *, nc, nctx):
    d = pl.program_id(0)
    i = pl.program_id(1)
    bsz, n = r_ref.shape[0], r_ref.shape[1]
    rev = d == 1
    blk = _seq_block(d, i, nc, nctx)

    @pl.when(i == 0)
    def _():
        s_ref[...] = jnp.zeros_like(s_ref)
        h_s[...] = jnp.zeros_like(h_s)

    seq_first = (blk == 0) | (blk == nctx)
    seq_last = (blk == nctx - 1) | (blk == nc - 1)
    row = lax.broadcasted_iota(jnp.int32, (n, 1), 0)
    cw = cw_ref[...]
    nl = -lam_ref[0]
    softplus = jnp.maximum(nl, 0.0) + jnp.log(1.0 + jnp.exp(-jnp.abs(nl)))
    gw = wa_ref.shape[-1]
    for b in range(bsz):
        x = u_ref[0, b]
        prev = jnp.where(seq_first, 0.0, up_ref[0, b])
        nxt = jnp.where(seq_last, 0.0, un_ref[0, b])
        xm1 = jnp.where(row == 0, prev[7:8], pltpu.roll(x, 1, 0))
        xm2 = jnp.where(row == 0, prev[6:7], jnp.where(row == 1, prev[7:8], pltpu.roll(x, 2, 0)))
        xp1 = jnp.where(row == n - 1, nxt[0:1], pltpu.roll(x, n - 1, 0))
        u = cw[0:1] * xm2 + cw[1:2] * xm1 + cw[2:3] * x + cw[3:4] * xp1 + cb_ref[...]
        ub = u.astype(BF16)
        ngrp = u.shape[-1] // gw
        gr = jnp.concatenate([_dot(ub[:, g * gw:(g + 1) * gw], wa_ref[0, g]) for g in range(ngrp)], axis=1)
        gi = jnp.concatenate([_dot(ub[:, g * gw:(g + 1) * gw], wx_ref[0, g]) for g in range(ngrp)], axis=1)
        log_a = -RG_C * _sigmoid(gr + ba_ref[0]) * softplus
        a_s[b] = jnp.exp(log_a)
        b_s[b] = jnp.sqrt(1.0 - jnp.exp(2.0 * log_a)) * (_sigmoid(gi + bx_ref[0]) * u)

    hs = [h_s[b] for b in range(bsz)]
    for t in range(n):
        tt = jnp.where(rev, n - 1 - t, t)
        for b in range(bsz):
            hs[b] = a_s[b, pl.ds(tt, 1), :] * hs[b] + b_s[b, pl.ds(tt, 1), :]
            hl_ref[0, b, pl.ds(tt, 1), :] = hs[b]
    for b in range(bsz):
        h_s[b] = hs[b]

    ti = lax.broadcasted_iota(jnp.int32, (n, n), 0)
    tj = lax.broadcasted_iota(jnp.int32, (n, n), 1)
    order = jnp.where(rev, tj - ti, ti - tj)
    tri = jnp.where(order >= 0, 1.0, 0.0).astype(BF16)
    wide = []
    gls = []
    for b in range(bsz):
        lw = lw_ref[0, b]
        h1 = lw.astype(BF16)
        r1 = lw - h1.astype(F32)
        h2 = r1.astype(BF16)
        h3 = (r1 - h2.astype(F32)).astype(BF16)
        c = _dot(tri, h1) + _dot(tri, h2) + _dot(tri, h3)
        c_last = jnp.where(rev, c[0:1], c[n - 1:n])
        kd = kd_ref[0, b]
        bb = bb_ref[0, b]
        enc = jnp.exp(-c)
        el = jnp.exp(c_last - c)
        wide.append([t.astype(BF16) for t in (-kk_ref[b] * jnp.exp(c - lw), r_ref[b] * jnp.exp(c), bb * enc, kd * enc,
                                              v_ref[b], bb * el, kd * el)])
        gls.append(jnp.exp(c_last))

    lane = lax.broadcasted_iota(jnp.int32, (n, PAIR), 1)
    first = lane < HEAD_DIM
    si = lax.broadcasted_iota(jnp.int32, (2 * n, 2 * n), 0) & (n - 1)
    sj = lax.broadcasted_iota(jnp.int32, (2 * n, 2 * n), 1) & (n - 1)
    row2 = lax.broadcasted_iota(jnp.int32, (2 * n, 2 * n), 0)
    col2 = lax.broadcasted_iota(jnp.int32, (2 * n, 2 * n), 1)
    qi = jnp.where(rev, n - 1 - si, si)
    qj = jnp.where(rev, n - 1 - sj, sj)
    strict = qi > qj
    incl = qi >= qj
    eye = jnp.where(row2 == col2, 1.0, 0.0)
    levels = []
    blk = 1
    while blk < n:
        bi = qi // blk
        bj = qj // blk
        levels.append((bi // 2 == bj // 2) & (bi % 2 == 1) & (bj % 2 == 0))
        blk *= 2

    zero = jnp.zeros((n, PAIR), BF16)

    def stack(x, sl):
        return jnp.concatenate([jnp.where(first, x[:, sl], zero), jnp.where(first, zero, x[:, sl])], axis=0)

    zero2 = jnp.zeros((2 * n, 2 * n), BF16)
    chains = [(b, p) for b in range(bsz) for p in range(N_HEADS // 2)]
    for g0 in range(0, len(chains), SCAN_GROUP):
        grp = chains[g0:g0 + SCAN_GROUP]
        sls = [slice(p * PAIR, (p + 1) * PAIR) for _, p in grp]
        opnd = [[stack(t, sl) for t in wide[b]] for (b, _), sl in zip(grp, sls)]
        s2 = [s_ref[b, p] for b, p in grp]
        s2b = [s.astype(BF16) for s in s2]
        aa = [_dot_nt(jnp.concatenate([am, rm], axis=0), jnp.concatenate([bm, km], axis=0))
              for am, rm, bm, km, _, _, _ in opnd]
        a_ab = [jnp.where(strict, a[:2 * n, :2 * n], 0.0) for a in aa]
        a_abb = [a.astype(BF16) for a in a_ab]
        a_ak = [jnp.where(strict, a[:2 * n, 2 * n:], 0.0).astype(BF16) for a in aa]
        a_r = [jnp.concatenate([jnp.where(incl, a[2 * n:, :2 * n], 0.0), jnp.where(incl, a[2 * n:, 2 * n:], 0.0)],
                               axis=1).astype(BF16) for a in aa]
        x = [_dot_nt(o[0], sb) + _dot(ak, o[4]) for o, sb, ak in zip(opnd, s2b, a_ak)]
        tinv = [eye + jnp.where(levels[0], a, 0.0) for a in a_ab]
        for m in levels[1:]:
            tb = [t.astype(BF16) for t in tinv]
            half = [_dot(t, jnp.where(m, a, zero2)) for t, a in zip(tb, a_abb)]
            tinv = [t + _dot(h.astype(BF16), q) for t, h, q in zip(tinv, half, tb)]
        uv = [jnp.concatenate([_dot(t.astype(BF16), xx.astype(BF16)).astype(BF16), o[4]], axis=0)
              for t, xx, o in zip(tinv, x, opnd)]
        for j, (b, p) in enumerate(grp):
            rm, blm, klm = opnd[j][1], opnd[j][5], opnd[j][6]
            o = _dot_nt(rm, s2b[j]) + _dot(a_r[j], uv[j])
            y_ref[0, b, :, sls[j]] = o[:n] + o[n:]
            s_ref[b, p] = s2[j] * gls[b][:, sls[j]] + _dot_tn(uv[j], jnp.concatenate([blm, klm], axis=0))


def _seq_mixers(r, v, kk, lw, kd, bb, z, conv_w, conv_b, wa, ba, wx, bx, lam, n_ctx):
    bsz, t, d = r.shape
    nc = t // CHUNK
    nctx = n_ctx // CHUNK
    hb = CHUNK // 8
    gw = wa.shape[-1]

    def blk_of(dd, i):
        return _seq_block(dd, i, nc, nctx)

    tok = pl.BlockSpec((bsz, CHUNK, d), lambda dd, i: (0, blk_of(dd, i), 0))
    tok2 = pl.BlockSpec((1, bsz, CHUNK, d), lambda dd, i: (dd, 0, blk_of(dd, i), 0))
    vec = lambda: pl.BlockSpec((1, 1, d), lambda dd, i: (dd, 0, 0))
    wspec = lambda: pl.BlockSpec((1, d // gw, gw, gw), lambda dd, i: (dd, 0, 0, 0))
    out = jax.ShapeDtypeStruct((2, bsz, t, d), F32)
    return pl.pallas_call(
        functools.partial(_seq_kernel, nc=nc, nctx=nctx),
        out_shape=(out, out),
        grid=(2, nc),
        in_specs=[tok, tok, tok, tok2, tok2, tok2,
                  pl.BlockSpec((1, bsz, CHUNK, d), lambda dd, i: (G_XB, 0, blk_of(dd, i), 0)),
                  pl.BlockSpec((1, bsz, 8, d), lambda dd, i: (G_XB, 0, jnp.maximum(blk_of(dd, i) * hb - 1, 0), 0)),
                  pl.BlockSpec((1, bsz, 8, d),
                               lambda dd, i: (G_XB, 0, jnp.minimum((blk_of(dd, i) + 1) * hb, t // 8 - 1), 0)),
                  pl.BlockSpec((4, d), lambda dd, i: (0, 0)),
                  pl.BlockSpec((1, d), lambda dd, i: (0, 0)),
                  wspec(), vec(), wspec(), vec(), vec()],
        out_specs=(tok2, tok2),
        scratch_shapes=[pltpu.VMEM((bsz, N_HEADS // 2, PAIR, PAIR), F32),
                        pltpu.VMEM((bsz, CHUNK, d), F32), pltpu.VMEM((bsz, CHUNK, d), F32),
                        pltpu.VMEM((bsz, 1, d), F32)],
        compiler_params=_cparams(("arbitrary", "arbitrary")),
        name="seq_mixers",
    )(r, v, kk, lw, kd, bb, z, z, z, conv_w, conv_b.reshape(1, d), wa, ba.reshape(2, 1, d), wx, bx.reshape(2, 1, d),
      lam.reshape(2, 1, d))


def _gelu_tanh(x):
    return 0.5 * x * (1.0 + jnp.tanh(math.sqrt(2.0 / math.pi) * (x + 0.044715 * (x * x * x))))


def _merge_kernel(x_ref, ya_ref, hl_ref, ug_ref, yw_ref, bonus_ref, g_ref, ma_ref, mb_ref, mc_ref,
                  mod_ref, lnw_ref, lnb_ref, wa_ref, wb_ref, wc_ref, wo_ref, o_ref, *, tm, n_ctx):
    i = pl.program_id(1)
    yb = ((hl_ref[0, 0] + hl_ref[1, 0]) * _gelu_tanh(ug_ref[0, 0])).astype(BF16)
    y = yw_ref[0, 0] + yw_ref[1, 0]
    inv = 1.0 / HEAD_DIM
    mean = _seg_sum64(y) * inv
    yc = y - mean
    var = _seg_sum64(yc * yc) * inv
    yn = yc * lax.rsqrt(var + LNX_EPS) * lnw_ref[...] + lnb_ref[...]
    oc = ((yn + bonus_ref[0]) * g_ref[0]).astype(BF16)
    mixed = (_sigmoid(ma_ref[0, 0]) * _dot(ya_ref[0], wa_ref[...])
             + _sigmoid(mb_ref[0, 0]) * _dot(yb, wb_ref[...])
             + _sigmoid(mc_ref[0, 0]) * _dot(oc, wc_ref[...]))
    yo = _dot(mixed.astype(BF16), wo_ref[...])
    rows = i * tm + lax.broadcasted_iota(jnp.int32, (tm, 1), 0)
    o_ref[0] = x_ref[0] + _row_select(mod_ref, 2, rows, n_ctx) * yo


def _merge(x, ynat, hlru, z, yrw, bonus, g, mods, ln_w, ln_b, wa, wb, wc, wo, n_ctx):
    bsz, t, d = x.shape
    tm = 256
    tok = pl.BlockSpec((1, tm, d), lambda b, i: (b, i, 0))
    tok2 = pl.BlockSpec((2, 1, tm, d), lambda b, i: (0, b, i, 0))
    zspec = lambda gidx: pl.BlockSpec((1, 1, tm, d), lambda b, i: (gidx, b, i, 0))
    vec = pl.BlockSpec((1, d), lambda b, i: (0, 0))
    wspec = pl.BlockSpec((d, d), lambda b, i: (0, 0))
    return pl.pallas_call(
        functools.partial(_merge_kernel, tm=tm, n_ctx=n_ctx),
        out_shape=jax.ShapeDtypeStruct((bsz, t, d), F32),
        grid=(bsz, t // tm),
        in_specs=[tok, tok, tok2, zspec(G_GB), tok2, tok, tok, zspec(G_MA), zspec(G_MB), zspec(G_MC),
                  pl.BlockSpec((1, 2, 6, d), lambda b, i: (b, 0, 0, 0)), vec, vec,
                  wspec, wspec, wspec, wspec],
        out_specs=tok,
        compiler_params=_cparams(("parallel", "parallel")),
        name="merge",
    )(x, ynat, hlru, z, yrw, bonus, g, z, z, z, mods, ln_w.reshape(1, d), ln_b.reshape(1, d), wa, wb, wc, wo)


def _mlp_kernel(x_ref, g_ref, mod_ref, w1_ref, w2_ref, fg_ref, o_ref, hb_ref, acc_ref, *, tm, n_ctx, final):
    i = pl.program_id(1)
    k = pl.program_id(2)
    rows = i * tm + lax.broadcasted_iota(jnp.int32, (tm, 1), 0)

    @pl.when(k == 0)
    def _():
        x = x_ref[0]
        ms = jnp.mean(x * x, axis=-1, keepdims=True)
        y = x * lax.rsqrt(ms + NORM_EPS) * g_ref[...]
        sh = _row_select(mod_ref, 3, rows, n_ctx)
        sc = _row_select(mod_ref, 4, rows, n_ctx)
        hb_ref[...] = (y * (1.0 + sc) + sh).astype(BF16)
        acc_ref[...] = jnp.zeros_like(acc_ref)

    a = jnp.maximum(_dot(hb_ref[...], w1_ref[...]), 0.0)
    acc_ref[...] += _dot((a * a).astype(BF16), w2_ref[...])

    @pl.when(k == pl.num_programs(2) - 1)
    def _():
        xn = x_ref[0] + _row_select(mod_ref, 5, rows, n_ctx) * acc_ref[...]
        if final:
            ms = jnp.mean(xn * xn, axis=-1, keepdims=True)
            xn = xn * lax.rsqrt(ms + NORM_EPS) * fg_ref[...]
        o_ref[0] = xn


def _mlp(x, g, mods, w1, w2, final_g, n_ctx, final):
    bsz, t, d = x.shape
    dff = w1.shape[1]
    tk = 1024
    tm = _pick(t, (1408, 768, 256))
    vec = pl.BlockSpec((1, d), lambda b, i, k: (0, 0))
    tok = pl.BlockSpec((1, tm, d), lambda b, i, k: (b, i, 0))
    return pl.pallas_call(
        functools.partial(_mlp_kernel, tm=tm, n_ctx=n_ctx, final=final),
        out_shape=jax.ShapeDtypeStruct((bsz, t, d), F32),
        grid=(bsz, t // tm, dff // tk),
        in_specs=[tok, vec, pl.BlockSpec((1, 2, 6, d), lambda b, i, k: (b, 0, 0, 0)),
                  pl.BlockSpec((d, tk), lambda b, i, k: (0, k)),
                  pl.BlockSpec((tk, d), lambda b, i, k: (k, 0)), vec],
        out_specs=tok,
        scratch_shapes=[pltpu.VMEM((tm, d), BF16), pltpu.VMEM((tm, d), F32)],
        compiler_params=_cparams(("parallel", "parallel", "arbitrary")),
        name="mlp",
    )(x, g.reshape(1, d), mods, w1, w2, final_g.reshape(1, d))


def _lowrank_weights(mu_h, w1, a1, g1, v1, mu_v):
    d = mu_h.shape[-1]
    blocks = [(w1[0], mu_h[0]), (w1[1], mu_h[0]), (a1[0], mu_h[1]), (a1[1], mu_h[1]), (g1, mu_h[2])]
    if v1 is not None:
        blocks.append((v1, mu_v))
    p = jnp.concatenate([w * (1.0 - m)[:, None] for w, m in blocks], axis=1)
    q = jnp.concatenate([w * m[:, None] for w, m in blocks], axis=1)
    pad = lambda a: jnp.pad(a, ((0, 0), (0, LR_W - a.shape[1])))
    return jnp.concatenate([pad(p), pad(q)], axis=1)


def _second_stage_weights(w2, a2, g2, v2):
    d = w2.shape[-1]
    def place(w, off):
        return jnp.zeros((256, d), F32).at[off:off + w.shape[0]].set(w)
    v2p = place(v2, 160) if v2 is not None else jnp.zeros((256, d), F32)
    return jnp.stack([place(w2[0], 0), place(w2[1], 64), place(a2[0], 128), place(a2[1], 192),
                      place(g2, 0), v2p]).astype(BF16)


def kernel(x, c, ctx, c_ctx, w_ada, b_ada, norm1_g, norm2_g, w_in, rpb, conv_w, conv_b, rg_wa, rg_ba, rg_wx, rg_bx, rg_lam, rw_mu_rkv, rw_mu_h, rw_w0, rw_w1, rw_w2, rw_a0, rw_a1, rw_a2, rw_g1, rw_g2, rw_k_k, rw_k_a, rw_r_k, rw_ln_w, rw_ln_b, vres_v0, vres_v1, vres_v2, vres_mu, w_br_a, w_br_b, w_br_c, w_out, w_ff1, w_ff2, final_g):
    bsz, seq, d = x.shape
    n_ctx = ctx.shape[1]
    depth = w_in.shape[0]
    t = n_ctx + seq
    xs = jnp.concatenate([ctx, x], axis=1)
    cvec = jnp.zeros((8, d), F32).at[:bsz].set(c).at[bsz].set(c_ctx)
    tables = _rope_tables(t, n_ctx)
    vfirst = None
    for i in range(depth):
        last = i == depth - 1
        mod = _ada(cvec, w_ada[i], b_ada[i])
        mods = jnp.stack([jnp.broadcast_to(mod[bsz].reshape(1, 6, d), (bsz, 6, d)),
                          mod[:bsz].reshape(bsz, 6, d)], axis=1)
        if i == 0:
            lr_w = _lowrank_weights(rw_mu_h[i], rw_w1[i], rw_a1[i], rw_g1[i], None, None)
            w2cat = _second_stage_weights(rw_w2[i], rw_a2[i], rw_g2[i], None)
            v0 = jnp.zeros((d,), F32)
        else:
            lr_w = _lowrank_weights(rw_mu_h[i], rw_w1[i], rw_a1[i], rw_g1[i], vres_v1[i - 1], vres_mu[i - 1])
            w2cat = _second_stage_weights(rw_w2[i], rw_a2[i], rw_g2[i], vres_v2[i - 1])
            v0 = vres_v0[i - 1]
        w_all = jnp.concatenate([w_in[i], lr_w], axis=1).astype(BF16)
        z, qr, kr, vb = _inproj(xs, norm1_g[i], mods, w_all, tables, n_ctx)
        ynat = _natten(qr, kr, vb, _natten_table(rpb[i]), n_ctx)
        r, v, kk, g, bonus, lw, kd, bb = _rwkv_prep(
            z, vfirst, rw_mu_rkv[i], rw_k_k[i], rw_k_a[i], rw_r_k[i].reshape(-1), rw_w0[i], rw_a0[i], v0,
            w2cat, n_ctx)
        if i == 0:
            vfirst = v
        yrw, hlru = _seq_mixers(r, v, kk, lw, kd, bb, z, conv_w[i], conv_b[i],
                                _block_diag_groups(rg_wa[i], 4).astype(BF16), rg_ba[i],
                                _block_diag_groups(rg_wx[i], 4).astype(BF16), rg_bx[i], rg_lam[i], n_ctx)
        xs = _merge(xs, ynat, hlru, z, yrw, bonus, g, mods, rw_ln_w[i], rw_ln_b[i],
                    w_br_a[i].astype(BF16), w_br_b[i].astype(BF16), w_br_c[i].astype(BF16),
                    w_out[i].astype(BF16), n_ctx)
        xs = _mlp(xs, norm2_g[i], mods, w_ff1[i].astype(BF16), w_ff2[i].astype(BF16), final_g, n_ctx, last)
    return xs[:, n_ctx:]
```

```python
import functools
import math

import jax
import jax.numpy as jnp
import numpy as np
from jax import lax
from jax.experimental import pallas as pl
from jax.experimental.pallas import tpu as pltpu

F32 = jnp.float32
BF16 = jnp.bfloat16

D_MODEL = 1024
GRID_W = 64
N_HEADS = 16
HEAD_DIM = 64
WIN_H = 8
WIN_W = 16
ROPE_BASE = 10000.0
NB_B = 16
BS_B = 64
RG_C = 8.0
LNX_EPS = 64e-5
NORM_EPS = 1e-6
NEG_INF = -1e30
MASKED = -1e30
N_ATT = 3
G_XB, G_GB, G_R, G_K, G_V, G_MA, G_MB, G_MC, G_LR = range(9)
LR_W = 512
CHUNK = 64
VMEM_LIMIT = 58 * 1024 * 1024
PAIR = 2 * HEAD_DIM
SCAN_GROUP = 16


def _cparams(sem):
    return pltpu.CompilerParams(dimension_semantics=sem, vmem_limit_bytes=VMEM_LIMIT)


def _pick(n, cands):
    for c in cands:
        if n % c == 0:
            return c
    raise ValueError(f"no tile for {n}")


def _dot(a, b):
    return jnp.dot(a, b, preferred_element_type=F32)


def _dot_nt(a, b):
    return lax.dot_general(a, b, (((1,), (1,)), ((), ())), preferred_element_type=F32)


def _dot_tn(a, b):
    return lax.dot_general(a, b, (((0,), (0,)), ((), ())), preferred_element_type=F32)


def _sigmoid(x):
    return 0.5 * jnp.tanh(0.5 * x) + 0.5


def _row_select(mod_ref, idx, rows, n_ctx):
    return jnp.where(rows < n_ctx, mod_ref[0, 0, idx:idx + 1, :], mod_ref[0, 1, idx:idx + 1, :])


def _seg_sum64(x):
    h1 = x.astype(BF16)
    r1 = x - h1.astype(F32)
    h2 = r1.astype(BF16)
    h3 = (r1 - h2.astype(F32)).astype(BF16)
    gi = lax.broadcasted_iota(jnp.int32, (3 * PAIR, PAIR), 0) % PAIR
    gj = lax.broadcasted_iota(jnp.int32, (3 * PAIR, PAIR), 1)
    ones = jnp.where(gi // HEAD_DIM == gj // HEAD_DIM, 1.0, 0.0).astype(BF16)
    out = []
    for p in range(x.shape[-1] // PAIR):
        sl = slice(p * PAIR, (p + 1) * PAIR)
        out.append(_dot(jnp.concatenate([h1[:, sl], h2[:, sl], h3[:, sl]], axis=1), ones))
    return jnp.concatenate(out, axis=1)


def _seq_block(d, i, nt, nctx):
    rev = jnp.where(i < nctx, nctx - 1 - i, nt - 1 - (i - nctx))
    return jnp.where(d == 0, i, rev)


def _ada_kernel(a_ref, w_ref, b_ref, o_ref):
    a = a_ref[...]
    s = a * _sigmoid(a)
    o_ref[...] = _dot(s.astype(BF16), w_ref[...].astype(BF16)) + b_ref[...]


def _ada(cvec, w, b):
    m, k = cvec.shape
    n = w.shape[1]
    tn = _pick(n, (1536, 1024, 512, 128))
    return pl.pallas_call(
        _ada_kernel,
        out_shape=jax.ShapeDtypeStruct((m, n), F32),
        grid=(n // tn,),
        in_specs=[pl.BlockSpec((m, k), lambda j: (0, 0)),
                  pl.BlockSpec((k, tn), lambda j: (0, j)),
                  pl.BlockSpec((1, tn), lambda j: (0, j))],
        out_specs=pl.BlockSpec((m, tn), lambda j: (0, j)),
        compiler_params=_cparams(("arbitrary",)),
        name="ada",
    )(cvec, w, b.reshape(1, n))


def _inproj_kernel(x_ref, g_ref, mod_ref, w_ref, c_ref, s1_ref, s2_ref, z_ref, q_ref, k_ref, v_ref, hb_ref,
                   *, tm, n_ctx):
    i = pl.program_id(1)
    j = pl.program_id(2)

    @pl.when(j == 0)
    def _():
        x = x_ref[0]
        ms = jnp.mean(x * x, axis=-1, keepdims=True)
        y = x * lax.rsqrt(ms + NORM_EPS) * g_ref[...]
        rows = i * tm + lax.broadcasted_iota(jnp.int32, (tm, 1), 0)
        sh = _row_select(mod_ref, 0, rows, n_ctx)
        sc = _row_select(mod_ref, 1, rows, n_ctx)
        hb_ref[...] = (y * (1.0 + sc) + sh).astype(BF16)

    z_ref[0, 0] = _dot(hb_ref[...], w_ref[...])
    d = z_ref.shape[-1]
    quarter = HEAD_DIM // 4

    def rope():
        cos, s1, s2 = c_ref[...], s1_ref[...], s2_ref[...]
        out = []
        for p in range(d // PAIR):
            xp = z_ref[0, 0, :, p * PAIR:(p + 1) * PAIR]
            out.append(xp * cos + pltpu.roll(xp, PAIR - quarter, 1) * s1 + pltpu.roll(xp, quarter, 1) * s2)
        return jnp.concatenate(out, axis=1)

    @pl.when(j == 0)
    def _():
        q_ref[0] = (rope() * (HEAD_DIM ** -0.5)).astype(BF16)

    @pl.when(j == 1)
    def _():
        k_ref[0] = rope().astype(BF16)

    @pl.when(j == 2)
    def _():
        v_ref[0] = z_ref[0, 0].astype(BF16)


def _inproj(x, g, mods, w, tables, n_ctx):
    bsz, t, d = x.shape
    ngrp = w.shape[1] // d
    tm = _pick(t, (1408, 768, 256))
    tspec = pl.BlockSpec((tm, PAIR), lambda b, i, j: (i, 0))
    aspec = pl.BlockSpec((1, tm, d), lambda b, i, j: (b, i, 0))
    ashape = jax.ShapeDtypeStruct((bsz, t, d), BF16)
    return pl.pallas_call(
        functools.partial(_inproj_kernel, tm=tm, n_ctx=n_ctx),
        out_shape=(jax.ShapeDtypeStruct((ngrp - N_ATT, bsz, t, d), F32), ashape, ashape, ashape),
        grid=(bsz, t // tm, ngrp),
        in_specs=[pl.BlockSpec((1, tm, d), lambda b, i, j: (b, i, 0)),
                  pl.BlockSpec((1, d), lambda b, i, j: (0, 0)),
                  pl.BlockSpec((1, 2, 6, d), lambda b, i, j: (b, 0, 0, 0)),
                  pl.BlockSpec((d, d), lambda b, i, j: (0, j)),
                  tspec, tspec, tspec],
        out_specs=(pl.BlockSpec((1, 1, tm, d), lambda b, i, j: (jnp.maximum(j - N_ATT, 0), b, i, 0)),
                   aspec, aspec, aspec),
        scratch_shapes=[pltpu.VMEM((tm, d), BF16)],
        compiler_params=_cparams(("parallel", "parallel", "arbitrary")),
        name="inproj",
    )(x, g.reshape(1, d), mods, w, *tables)


def _rope_tables(t, n_ctx):
    nf = HEAD_DIM // 4
    inv_freq = ROPE_BASE ** (-jnp.arange(nf, dtype=F32) / nf)
    tok = jnp.arange(t) - n_ctx
    row = (tok // GRID_W).astype(F32)
    col = (tok % GRID_W).astype(F32)
    lane = np.arange(PAIR) % HEAD_DIM
    use_col = jnp.asarray(lane >= HEAD_DIM // 2)
    first = jnp.asarray((lane % (HEAD_DIM // 2)) < nf)
    pos = jnp.where(use_col[None, :], col[:, None], row[:, None])
    ang = pos * inv_freq[lane % nf][None, :]
    is_lat = (tok >= 0)[:, None]
    cos = jnp.where(is_lat, jnp.cos(ang), 1.0)
    sin = jnp.where(is_lat, jnp.sin(ang), 0.0)
    return cos, jnp.where(first[None, :], -sin, 0.0), jnp.where(first[None, :], 0.0, sin)


def _natten_kernel(q_ref, *refs):
    k_refs = refs[0:WIN_H]
    v_refs = refs[WIN_H:2 * WIN_H]
    kc_ref, vc_ref = refs[2 * WIN_H:2 * WIN_H + 2]
    tb_refs = refs[2 * WIN_H + 2:-1]
    o_ref = refs[-1]
    nq = q_ref.shape[1]
    nloc = WIN_H * GRID_W
    lane = lax.broadcasted_iota(jnp.int32, (nq, PAIR), 1)
    first = lane < HEAD_DIM
    zero = jnp.zeros((nq, PAIR), BF16)
    sls = [slice(p * PAIR, (p + 1) * PAIR) for p in range(N_HEADS // 2)]
    qs = [jnp.concatenate([jnp.where(first, q_ref[0, :, sl], zero), jnp.where(first, zero, q_ref[0, :, sl])], axis=0)
          for sl in sls]
    ks = [jnp.concatenate([r[0, :, sl] for r in k_refs] + [kc_ref[0, :, sl]], axis=0) for sl in sls]
    scores = [_dot_nt(q, k) for q, k in zip(qs, ks)]
    probs = []
    invs = []
    for p, s in enumerate(scores):
        tb = jnp.concatenate([t[0, p] for t in tb_refs], axis=1)
        sl_ = s[:, :nloc] + tb
        sc_ = s[:, nloc:]
        m = jnp.maximum(jnp.max(sl_, axis=-1, keepdims=True), jnp.max(sc_, axis=-1, keepdims=True))
        el_ = jnp.exp(sl_ - m)
        ec_ = jnp.exp(sc_ - m)
        invs.append(1.0 / (jnp.sum(el_, axis=-1, keepdims=True) + jnp.sum(ec_, axis=-1, keepdims=True)))
        probs.append(jnp.concatenate([el_.astype(BF16), ec_.astype(BF16)], axis=1))
    vs = [jnp.concatenate([r[0, :, sl] for r in v_refs] + [vc_ref[0, :, sl]], axis=0) for sl in sls]
    outs = [_dot(pr, v) * inv for pr, v, inv in zip(probs, vs, invs)]
    for sl, o in zip(sls, outs):
        o_ref[0, :, sl] = jnp.where(first, o[:nq], o[nq:]).astype(BF16)


def _natten_table(rpb):
    col = np.arange(GRID_W)
    c0 = np.clip(col - WIN_W // 2, 0, GRID_W - WIN_W)
    in_win = (col[None, :] >= c0[:, None]) & (col[None, :] < c0[:, None] + WIN_W)
    padw = GRID_W - WIN_W
    padded = jnp.pad(rpb.astype(F32), ((0, 0), (0, 0), (padw, padw)))
    bias = jnp.stack([padded[:, :, GRID_W - 1 - q:2 * GRID_W - 1 - q] for q in range(GRID_W)], axis=2)
    b = jnp.where(jnp.asarray(in_win)[None, None], bias, MASKED)
    b = jnp.transpose(b, (1, 0, 2, 3)).reshape(2 * WIN_H - 1, N_HEADS // 2, 2 * GRID_W, GRID_W)
    two = jnp.concatenate([b[:-1], b[1:]], axis=-1)
    return jnp.concatenate([two, jnp.full((1,) + two.shape[1:], MASKED, F32)], axis=0)


def _natten(qr, kr, vb, table, n_ctx):
    bsz, t, d = qr.shape
    nqc = n_ctx // GRID_W
    rows = (t - n_ctx) // GRID_W
    assert rows >= WIN_H
    masked_entry = table.shape[0] - 1

    def r0(s):
        return jnp.clip(s - nqc - WIN_H // 2, 0, rows - WIN_H)

    def entry(s, c):
        return jnp.where(s < nqc, masked_entry, 2 * c + WIN_H - 1 - (s - nqc - r0(s)))

    blk = (1, GRID_W, d)
    kv_specs = [pl.BlockSpec(blk, functools.partial(lambda b, s, i: (b, nqc + r0(s) + i, 0), i=i))
                for i in range(WIN_H)]
    ctx_spec = pl.BlockSpec((1, n_ctx, d), lambda b, s: (b, 0, 0))
    tb_specs = [pl.BlockSpec((1,) + table.shape[1:], functools.partial(lambda b, s, c: (entry(s, c), 0, 0, 0), c=c))
                for c in range(WIN_H // 2)]
    return pl.pallas_call(
        _natten_kernel,
        out_shape=jax.ShapeDtypeStruct((bsz, t, d), BF16),
        grid=(bsz, t // GRID_W),
        in_specs=[pl.BlockSpec(blk, lambda b, s: (b, s, 0))] + kv_specs + kv_specs + [ctx_spec, ctx_spec] + tb_specs,
        out_specs=pl.BlockSpec(blk, lambda b, s: (b, s, 0)),
        compiler_params=_cparams(("parallel", "arbitrary")),
        name="natten",
    )(qr, *([kr] * WIN_H), *([vb] * WIN_H), kr, vb, *([table] * (WIN_H // 2)))


def _block_diag_groups(w, per):
    two, nb, bs, _ = w.shape
    w = w.reshape(two, nb // per, per, bs, bs)
    eye = jnp.eye(per, dtype=w.dtype)
    return jnp.einsum('dgaij,ab->dgaibj', w, eye).reshape(two, nb // per, per * bs, per * bs)


def _rwkv_prep_kernel(*refs, tb, nt, nctx, has_vres):
    (zr, zr_p, zr_n, zk, zk_p, zk_n, zv, zv_p, zv_n, zl, zl_p, zl_n) = refs[:12]
    refs = refs[12:]
    if has_vres:
        vf_ref, refs = refs[0], refs[1:]
    (mu_ref, kk_ref, ka_ref, rk_ref, w0_ref, a0_ref, v0_ref, w2_ref,
     r_o, v_o, kk_o, g_o, bonus_o, lw_o, kd_o, bb_o) = refs
    blk = pl.program_id(1)
    seq_first = (blk == 0) | (blk == nctx)
    seq_last = (blk == nctx - 1) | (blk == nt - 1)
    row = lax.broadcasted_iota(jnp.int32, (tb, 1), 0)

    def nbr_mean(main, prev_row, next_row):
        prev_row = jnp.where(seq_first, 0.0, prev_row)
        next_row = jnp.where(seq_last, 0.0, next_row)
        up = jnp.where(row == 0, prev_row, pltpu.roll(main, 1, 0))
        dn = jnp.where(row == tb - 1, next_row, pltpu.roll(main, tb - 1, 0))
        return 0.5 * (up + dn)

    def shifted(m_ref, p_ref, n_ref, mu):
        main = m_ref[0, 0]
        return main + mu * (nbr_mean(main, p_ref[0, 0, 7:8, :], n_ref[0, 0, 0:1, :]) - main)

    mu = mu_ref[...]
    r = shifted(zr, zr_p, zr_n, mu[0:1])
    k = shifted(zk, zk_p, zk_n, mu[1:2])
    v = shifted(zv, zv_p, zv_n, mu[2:3])

    lr = zl[0, 0, :, 0:LR_W] + nbr_mean(zl[0, 0, :, LR_W:2 * LR_W], zl_p[0, 0, 7:8, LR_W:2 * LR_W],
                                        zl_n[0, 0, 0:1, LR_W:2 * LR_W])
    col = lax.broadcasted_iota(jnp.int32, (1, LR_W), 1)
    act = jnp.where(col < 128, jnp.tanh(lr), jnp.where((col >= 256) & (col < 416), _sigmoid(lr), lr))
    actb = act.astype(BF16)
    s0 = actb[:, 0:256]
    s1 = actb[:, 256:512]
    g_o[0] = _dot(s1, w2_ref[4]).astype(BF16)
    if has_vres:
        v = v + (vf_ref[0].astype(F32) - v) * _sigmoid(v0_ref[...] + _dot(s1, w2_ref[5]))
    r_o[0] = r.astype(BF16)
    v_o[0] = v.astype(BF16)
    kkr = k * kk_ref[...]
    kk = kkr / jnp.maximum(jnp.sqrt(_seg_sum64(kkr * kkr)), 1e-12)
    kk_o[0] = kk.astype(BF16)
    ksum = jnp.zeros_like(k)
    for dd in range(2):
        zw = w0_ref[dd:dd + 1, :] + _dot(s0, w2_ref[dd])
        lw_o[dd, 0] = -math.exp(-0.5) * _sigmoid(zw)
        a = _sigmoid(a0_ref[dd:dd + 1, :] + _dot(s0, w2_ref[2 + dd]))
        kd = k * (1.0 + (a - 1.0) * ka_ref[...])
        kd_o[dd, 0] = kd.astype(BF16)
        bb_o[dd, 0] = (kk * a).astype(BF16)
        ksum = ksum + kd
    bonus_o[0] = (_seg_sum64(r * ksum * rk_ref[...]) * v).astype(BF16)


def _rwkv_prep(z, vfirst, mu_rkv, k_k, k_a, r_k, w0, a0, v0, w2cat, n_ctx):
    _, bsz, t, d = z.shape
    tb = 256
    nt = t // tb
    nctx = n_ctx // tb
    hb = tb // 8
    has_vres = vfirst is not None

    def triple(g):
        return [pl.BlockSpec((1, 1, tb, d), lambda b, i: (g, b, i, 0)),
                pl.BlockSpec((1, 1, 8, d), lambda b, i: (g, b, jnp.maximum(i * hb - 1, 0), 0)),
                pl.BlockSpec((1, 1, 8, d), lambda b, i: (g, b, jnp.minimum((i + 1) * hb, t // 8 - 1), 0))]

    tok = pl.BlockSpec((1, tb, d), lambda b, i: (b, i, 0))
    tok2 = pl.BlockSpec((2, 1, tb, d), lambda b, i: (0, b, i, 0))
    const = lambda shape: pl.BlockSpec(shape, lambda b, i: (0,) * len(shape))
    in_specs = triple(G_R) + triple(G_K) + triple(G_V) + triple(G_LR)
    args = [z] * 12
    if has_vres:
        in_specs.append(tok)
        args.append(vfirst)
    in_specs += [const((3, d)), const((1, d)), const((1, d)), const((1, d)), const((2, d)), const((2, d)),
                 const((1, d)), const(w2cat.shape)]
    args += [mu_rkv, k_k.reshape(1, d), k_a.reshape(1, d), r_k.reshape(1, d), w0, a0, v0.reshape(1, d), w2cat]
    one = jax.ShapeDtypeStruct((bsz, t, d), BF16)
    two = jax.ShapeDtypeStruct((2, bsz, t, d), BF16)
    return pl.pallas_call(
        functools.partial(_rwkv_prep_kernel, tb=tb, nt=nt, nctx=nctx, has_vres=has_vres),
        out_shape=(one, one, one, one, one, jax.ShapeDtypeStruct((2, bsz, t, d), F32), two, two),
        grid=(bsz, nt),
        in_specs=in_specs,
        out_specs=(tok, tok, tok, tok, tok, tok2, tok2, tok2),
        compiler_params=_cparams(("parallel", "parallel")),
        name="rwkv_prep",
    )(*args)


def _seq_kernel(r_ref, v_ref, kk_ref, lw_ref, kd_ref, bb_ref,
                u_ref, up_ref, un_ref, cw_ref, cb_ref, wa_ref, ba_ref, wx_ref, bx_ref, lam_ref,
                y_ref, hl_ref, s_ref, a_s, b_s, h_s, *, nc, nctx):
    d = pl.program_id(0)
    i = pl.program_id(1)
    bsz, n = r_ref.shape[0], r_ref.shape[1]
    rev = d == 1
    blk = _seq_block(d, i, nc, nctx)

    @pl.when(i == 0)
    def _():
        s_ref[...] = jnp.zeros_like(s_ref)
        h_s[...] = jnp.zeros_like(h_s)

    seq_first = (blk == 0) | (blk == nctx)
    seq_last = (blk == nctx - 1) | (blk == nc - 1)
    row = lax.broadcasted_iota(jnp.int32, (n, 1), 0)
    cw = cw_ref[...]
    nl = -lam_ref[0]
    softplus = jnp.maximum(nl, 0.0) + jnp.log(1.0 + jnp.exp(-jnp.abs(nl)))
    gw = wa_ref.shape[-1]
    for b in range(bsz):
        x = u_ref[0, b]
        prev = jnp.where(seq_first, 0.0, up_ref[0, b])
        nxt = jnp.where(seq_last, 0.0, un_ref[0, b])
        xm1 = jnp.where(row == 0, prev[7:8], pltpu.roll(x, 1, 0))
        xm2 = jnp.where(row == 0, prev[6:7], jnp.where(row == 1, prev[7:8], pltpu.roll(x, 2, 0)))
        xp1 = jnp.where(row == n - 1, nxt[0:1], pltpu.roll(x, n - 1, 0))
        u = cw[0:1] * xm2 + cw[1:2] * xm1 + cw[2:3] * x + cw[3:4] * xp1 + cb_ref[...]
        ub = u.astype(BF16)
        ngrp = u.shape[-1] // gw
        gr = jnp.concatenate([_dot(ub[:, g * gw:(g + 1) * gw], wa_ref[0, g]) for g in range(ngrp)], axis=1)
        gi = jnp.concatenate([_dot(ub[:, g * gw:(g + 1) * gw], wx_ref[0, g]) for g in range(ngrp)], axis=1)
        log_a = -RG_C * _sigmoid(gr + ba_ref[0]) * softplus
        a_s[b] = jnp.exp(log_a)
        b_s[b] = jnp.sqrt(1.0 - jnp.exp(2.0 * log_a)) * (_sigmoid(gi + bx_ref[0]) * u)

    hs = [h_s[b] for b in range(bsz)]
    for t in range(n):
        tt = jnp.where(rev, n - 1 - t, t)
        for b in range(bsz):
            hs[b] = a_s[b, pl.ds(tt, 1), :] * hs[b] + b_s[b, pl.ds(tt, 1), :]
            b_s[b, pl.ds(tt, 1), :] = hs[b]
    for b in range(bsz):
        h_s[b] = hs[b]
        hl_ref[0, b] = b_s[b].astype(BF16)

    ti = lax.broadcasted_iota(jnp.int32, (n, n), 0)
    tj = lax.broadcasted_iota(jnp.int32, (n, n), 1)
    order = jnp.where(rev, tj - ti, ti - tj)
    tri = jnp.where(order >= 0, 1.0, 0.0).astype(BF16)
    wide = []
    gls = []
    for b in range(bsz):
        lw = lw_ref[0, b]
        h1 = lw.astype(BF16)
        r1 = lw - h1.astype(F32)
        h2 = r1.astype(BF16)
        h3 = (r1 - h2.astype(F32)).astype(BF16)
        c = _dot(tri, h1) + _dot(tri, h2) + _dot(tri, h3)
        c_last = jnp.where(rev, c[0:1], c[n - 1:n])
        kd = kd_ref[0, b].astype(F32)
        bb = bb_ref[0, b].astype(F32)
        enc = jnp.exp(-c)
        el = jnp.exp(c_last - c)
        wide.append([t.astype(BF16) for t in (-kk_ref[b].astype(F32) * jnp.exp(c - lw),
                                              r_ref[b].astype(F32) * jnp.exp(c), bb * enc, kd * enc,
                                              v_ref[b], bb * el, kd * el)])
        gls.append(jnp.exp(c_last))

    lane = lax.broadcasted_iota(jnp.int32, (n, PAIR), 1)
    first = lane < HEAD_DIM
    si = lax.broadcasted_iota(jnp.int32, (2 * n, 2 * n), 0) & (n - 1)
    sj = lax.broadcasted_iota(jnp.int32, (2 * n, 2 * n), 1) & (n - 1)
    row2 = lax.broadcasted_iota(jnp.int32, (2 * n, 2 * n), 0)
    col2 = lax.broadcasted_iota(jnp.int32, (2 * n, 2 * n), 1)
    qi = jnp.where(rev, n - 1 - si, si)
    qj = jnp.where(rev, n - 1 - sj, sj)
    strict = qi > qj
    incl = qi >= qj
    eye = jnp.where(row2 == col2, 1.0, 0.0)
    levels = []
    blk = 1
    while blk < n:
        bi = qi // blk
        bj = qj // blk
        levels.append((bi // 2 == bj // 2) & (bi % 2 == 1) & (bj % 2 == 0))
        blk *= 2

    zero = jnp.zeros((n, PAIR), BF16)

    def stack(x, sl):
        return jnp.concatenate([jnp.where(first, x[:, sl], zero), jnp.where(first, zero, x[:, sl])], axis=0)

    zero2 = jnp.zeros((2 * n, 2 * n), BF16)
    chains = [(b, p) for b in range(bsz) for p in range(N_HEADS // 2)]
    for g0 in range(0, len(chains), SCAN_GROUP):
        grp = chains[g0:g0 + SCAN_GROUP]
        sls = [slice(p * PAIR, (p + 1) * PAIR) for _, p in grp]
        opnd = [[stack(t, sl) for t in wide[b]] for (b, _), sl in zip(grp, sls)]
        s2 = [s_ref[b, p] for b, p in grp]
        s2b = [s.astype(BF16) for s in s2]
        aa = [_dot_nt(jnp.concatenate([am, rm], axis=0), jnp.concatenate([bm, km], axis=0))
              for am, rm, bm, km, _, _, _ in opnd]
        a_ab = [jnp.where(strict, a[:2 * n, :2 * n], 0.0) for a in aa]
        a_abb = [a.astype(BF16) for a in a_ab]
        a_ak = [jnp.where(strict, a[:2 * n, 2 * n:], 0.0).astype(BF16) for a in aa]
        a_r = [jnp.concatenate([jnp.where(incl, a[2 * n:, :2 * n], 0.0), jnp.where(incl, a[2 * n:, 2 * n:], 0.0)],
                               axis=1).astype(BF16) for a in aa]
        x = [_dot_nt(o[0], sb) + _dot(ak, o[4]) for o, sb, ak in zip(opnd, s2b, a_ak)]
        tinv = [eye + jnp.where(levels[0], a, 0.0) for a in a_ab]
        for m in levels[1:]:
            tb = [t.astype(BF16) for t in tinv]
            half = [_dot(t, jnp.where(m, a, zero2)) for t, a in zip(tb, a_abb)]
            tinv = [t + _dot(h.astype(BF16), q) for t, h, q in zip(tinv, half, tb)]
        uv = [jnp.concatenate([_dot(t.astype(BF16), xx.astype(BF16)).astype(BF16), o[4]], axis=0)
              for t, xx, o in zip(tinv, x, opnd)]
        for j, (b, p) in enumerate(grp):
            rm, blm, klm = opnd[j][1], opnd[j][5], opnd[j][6]
            o = _dot_nt(rm, s2b[j]) + _dot(a_r[j], uv[j])
            y_ref[0, b, :, sls[j]] = (o[:n] + o[n:]).astype(BF16)
            s_ref[b, p] = s2[j] * gls[b][:, sls[j]] + _dot_tn(uv[j], jnp.concatenate([blm, klm], axis=0))


def _seq_mixers(r, v, kk, lw, kd, bb, z, conv_w, conv_b, wa, ba, wx, bx, lam, n_ctx):
    bsz, t, d = r.shape
    nc = t // CHUNK
    nctx = n_ctx // CHUNK
    hb = CHUNK // 8
    gw = wa.shape[-1]

    def blk_of(dd, i):
        return _seq_block(dd, i, nc, nctx)

    tok = pl.BlockSpec((bsz, CHUNK, d), lambda dd, i: (0, blk_of(dd, i), 0))
    tok2 = pl.BlockSpec((1, bsz, CHUNK, d), lambda dd, i: (dd, 0, blk_of(dd, i), 0))
    vec = lambda: pl.BlockSpec((1, 1, d), lambda dd, i: (dd, 0, 0))
    wspec = lambda: pl.BlockSpec((1, d // gw, gw, gw), lambda dd, i: (dd, 0, 0, 0))
    out = jax.ShapeDtypeStruct((2, bsz, t, d), BF16)
    return pl.pallas_call(
        functools.partial(_seq_kernel, nc=nc, nctx=nctx),
        out_shape=(out, out),
        grid=(2, nc),
        in_specs=[tok, tok, tok, tok2, tok2, tok2,
                  pl.BlockSpec((1, bsz, CHUNK, d), lambda dd, i: (G_XB, 0, blk_of(dd, i), 0)),
                  pl.BlockSpec((1, bsz, 8, d), lambda dd, i: (G_XB, 0, jnp.maximum(blk_of(dd, i) * hb - 1, 0), 0)),
                  pl.BlockSpec((1, bsz, 8, d),
                               lambda dd, i: (G_XB, 0, jnp.minimum((blk_of(dd, i) + 1) * hb, t // 8 - 1), 0)),
                  pl.BlockSpec((4, d), lambda dd, i: (0, 0)),
                  pl.BlockSpec((1, d), lambda dd, i: (0, 0)),
                  wspec(), vec(), wspec(), vec(), vec()],
        out_specs=(tok2, tok2),
        scratch_shapes=[pltpu.VMEM((bsz, N_HEADS // 2, PAIR, PAIR), F32),
                        pltpu.VMEM((bsz, CHUNK, d), F32), pltpu.VMEM((bsz, CHUNK, d), F32),
                        pltpu.VMEM((bsz, 1, d), F32)],
        compiler_params=_cparams(("arbitrary", "arbitrary")),
        name="seq_mixers",
    )(r, v, kk, lw, kd, bb, z, z, z, conv_w, conv_b.reshape(1, d), wa, ba.reshape(2, 1, d), wx, bx.reshape(2, 1, d),
      lam.reshape(2, 1, d))


def _gelu_tanh(x):
    return 0.5 * x * (1.0 + jnp.tanh(math.sqrt(2.0 / math.pi) * (x + 0.044715 * (x * x * x))))


def _merge_kernel(x_ref, ya_ref, hl_ref, ug_ref, yw_ref, bonus_ref, g_ref, ma_ref, mb_ref, mc_ref,
                  mod_ref, lnw_ref, lnb_ref, wa_ref, wb_ref, wc_ref, wo_ref, o_ref, *, tm, n_ctx):
    i = pl.program_id(1)
    yb = ((hl_ref[0, 0].astype(F32) + hl_ref[1, 0].astype(F32)) * _gelu_tanh(ug_ref[0, 0])).astype(BF16)
    y = yw_ref[0, 0].astype(F32) + yw_ref[1, 0].astype(F32)
    inv = 1.0 / HEAD_DIM
    mean = _seg_sum64(y) * inv
    yc = y - mean
    var = _seg_sum64(yc * yc) * inv
    yn = yc * lax.rsqrt(var + LNX_EPS) * lnw_ref[...] + lnb_ref[...]
    oc = ((yn + bonus_ref[0].astype(F32)) * g_ref[0].astype(F32)).astype(BF16)
    mixed = (_sigmoid(ma_ref[0, 0]) * _dot(ya_ref[0], wa_ref[...])
             + _sigmoid(mb_ref[0, 0]) * _dot(yb, wb_ref[...])
             + _sigmoid(mc_ref[0, 0]) * _dot(oc, wc_ref[...]))
    yo = _dot(mixed.astype(BF16), wo_ref[...])
    rows = i * tm + lax.broadcasted_iota(jnp.int32, (tm, 1), 0)
    o_ref[0] = x_ref[0] + _row_select(mod_ref, 2, rows, n_ctx) * yo


def _merge(x, ynat, hlru, z, yrw, bonus, g, mods, ln_w, ln_b, wa, wb, wc, wo, n_ctx):
    bsz, t, d = x.shape
    tm = 256
    tok = pl.BlockSpec((1, tm, d), lambda b, i: (b, i, 0))
    tok2 = pl.BlockSpec((2, 1, tm, d), lambda b, i: (0, b, i, 0))
    zspec = lambda gidx: pl.BlockSpec((1, 1, tm, d), lambda b, i: (gidx, b, i, 0))
    vec = pl.BlockSpec((1, d), lambda b, i: (0, 0))
    wspec = pl.BlockSpec((d, d), lambda b, i: (0, 0))
    return pl.pallas_call(
        functools.partial(_merge_kernel, tm=tm, n_ctx=n_ctx),
        out_shape=jax.ShapeDtypeStruct((bsz, t, d), F32),
        grid=(bsz, t // tm),
        in_specs=[tok, tok, tok2, zspec(G_GB), tok2, tok, tok, zspec(G_MA), zspec(G_MB), zspec(G_MC),
                  pl.BlockSpec((1, 2, 6, d), lambda b, i: (b, 0, 0, 0)), vec, vec,
                  wspec, wspec, wspec, wspec],
        out_specs=tok,
        compiler_params=_cparams(("parallel", "parallel")),
        name="merge",
    )(x, ynat, hlru, z, yrw, bonus, g, z, z, z, mods, ln_w.reshape(1, d), ln_b.reshape(1, d), wa, wb, wc, wo)


def _mlp_kernel(x_ref, g_ref, mod_ref, w1_ref, w2_ref, fg_ref, o_ref, hb_ref, acc_ref, *, tm, n_ctx, final):
    i = pl.program_id(1)
    k = pl.program_id(2)
    rows = i * tm + lax.broadcasted_iota(jnp.int32, (tm, 1), 0)

    @pl.when(k == 0)
    def _():
        x = x_ref[0]
        ms = jnp.mean(x * x, axis=-1, keepdims=True)
        y = x * lax.rsqrt(ms + NORM_EPS) * g_ref[...]
        sh = _row_select(mod_ref, 3, rows, n_ctx)
        sc = _row_select(mod_ref, 4, rows, n_ctx)
        hb_ref[...] = (y * (1.0 + sc) + sh).astype(BF16)
        acc_ref[...] = jnp.zeros_like(acc_ref)

    a = jnp.maximum(_dot(hb_ref[...], w1_ref[...]), 0.0)
    acc_ref[...] += _dot((a * a).astype(BF16), w2_ref[...])

    @pl.when(k == pl.num_programs(2) - 1)
    def _():
        xn = x_ref[0] + _row_select(mod_ref, 5, rows, n_ctx) * acc_ref[...]
        if final:
            ms = jnp.mean(xn * xn, axis=-1, keepdims=True)
            xn = xn * lax.rsqrt(ms + NORM_EPS) * fg_ref[...]
        o_ref[0] = xn


def _mlp(x, g, mods, w1, w2, final_g, n_ctx, final):
    bsz, t, d = x.shape
    dff = w1.shape[1]
    tk = 1024
    tm = _pick(t, (1408, 768, 256))
    vec = pl.BlockSpec((1, d), lambda b, i, k: (0, 0))
    tok = pl.BlockSpec((1, tm, d), lambda b, i, k: (b, i, 0))
    return pl.pallas_call(
        functools.partial(_mlp_kernel, tm=tm, n_ctx=n_ctx, final=final),
        out_shape=jax.ShapeDtypeStruct((bsz, t, d), F32),
        grid=(bsz, t // tm, dff // tk),
        in_specs=[tok, vec, pl.BlockSpec((1, 2, 6, d), lambda b, i, k: (b, 0, 0, 0)),
                  pl.BlockSpec((d, tk), lambda b, i, k: (0, k)),
                  pl.BlockSpec((tk, d), lambda b, i, k: (k, 0)), vec],
        out_specs=tok,
        scratch_shapes=[pltpu.VMEM((tm, d), BF16), pltpu.VMEM((tm, d), F32)],
        compiler_params=_cparams(("parallel", "parallel", "arbitrary")),
        name="mlp",
    )(x, g.reshape(1, d), mods, w1, w2, final_g.reshape(1, d))


def _lowrank_weights(mu_h, w1, a1, g1, v1, mu_v):
    d = mu_h.shape[-1]
    blocks = [(w1[0], mu_h[0]), (w1[1], mu_h[0]), (a1[0], mu_h[1]), (a1[1], mu_h[1]), (g1, mu_h[2])]
    if v1 is not None:
        blocks.append((v1, mu_v))
    p = jnp.concatenate([w * (1.0 - m)[:, None] for w, m in blocks], axis=1)
    q = jnp.concatenate([w * m[:, None] for w, m in blocks], axis=1)
    pad = lambda a: jnp.pad(a, ((0, 0), (0, LR_W - a.shape[1])))
    return jnp.concatenate([pad(p), pad(q)], axis=1)


def _second_stage_weights(w2, a2, g2, v2):
    d = w2.shape[-1]
    def place(w, off):
        return jnp.zeros((256, d), F32).at[off:off + w.shape[0]].set(w)
    v2p = place(v2, 160) if v2 is not None else jnp.zeros((256, d), F32)
    return jnp.stack([place(w2[0], 0), place(w2[1], 64), place(a2[0], 128), place(a2[1], 192),
                      place(g2, 0), v2p]).astype(BF16)


def kernel(x, c, ctx, c_ctx, w_ada, b_ada, norm1_g, norm2_g, w_in, rpb, conv_w, conv_b, rg_wa, rg_ba, rg_wx, rg_bx, rg_lam, rw_mu_rkv, rw_mu_h, rw_w0, rw_w1, rw_w2, rw_a0, rw_a1, rw_a2, rw_g1, rw_g2, rw_k_k, rw_k_a, rw_r_k, rw_ln_w, rw_ln_b, vres_v0, vres_v1, vres_v2, vres_mu, w_br_a, w_br_b, w_br_c, w_out, w_ff1, w_ff2, final_g):
    bsz, seq, d = x.shape
    n_ctx = ctx.shape[1]
    depth = w_in.shape[0]
    t = n_ctx + seq
    xs = jnp.concatenate([ctx, x], axis=1)
    cvec = jnp.zeros((8, d), F32).at[:bsz].set(c).at[bsz].set(c_ctx)
    tables = _rope_tables(t, n_ctx)
    vfirst = None
    for i in range(depth):
        last = i == depth - 1
        mod = _ada(cvec, w_ada[i], b_ada[i])
        mods = jnp.stack([jnp.broadcast_to(mod[bsz].reshape(1, 6, d), (bsz, 6, d)),
                          mod[:bsz].reshape(bsz, 6, d)], axis=1)
        if i == 0:
            lr_w = _lowrank_weights(rw_mu_h[i], rw_w1[i], rw_a1[i], rw_g1[i], None, None)
            w2cat = _second_stage_weights(rw_w2[i], rw_a2[i], rw_g2[i], None)
            v0 = jnp.zeros((d,), F32)
        else:
            lr_w = _lowrank_weights(rw_mu_h[i], rw_w1[i], rw_a1[i], rw_g1[i], vres_v1[i - 1], vres_mu[i - 1])
            w2cat = _second_stage_weights(rw_w2[i], rw_a2[i], rw_g2[i], vres_v2[i - 1])
            v0 = vres_v0[i - 1]
        w_all = jnp.concatenate([w_in[i], lr_w], axis=1).astype(BF16)
        z, qr, kr, vb = _inproj(xs, norm1_g[i], mods, w_all, tables, n_ctx)
        ynat = _natten(qr, kr, vb, _natten_table(rpb[i]), n_ctx)
        r, v, kk, g, bonus, lw, kd, bb = _rwkv_prep(
            z, vfirst, rw_mu_rkv[i], rw_k_k[i], rw_k_a[i], rw_r_k[i].reshape(-1), rw_w0[i], rw_a0[i], v0,
            w2cat, n_ctx)
        if i == 0:
            vfirst = v
        yrw, hlru = _seq_mixers(r, v, kk, lw, kd, bb, z, conv_w[i], conv_b[i],
                                _block_diag_groups(rg_wa[i], 4).astype(BF16), rg_ba[i],
                                _block_diag_groups(rg_wx[i], 4).astype(BF16), rg_bx[i], rg_lam[i], n_ctx)
        xs = _merge(xs, ynat, hlru, z, yrw, bonus, g, mods, rw_ln_w[i], rw_ln_b[i],
                    w_br_a[i].astype(BF16), w_br_b[i].astype(BF16), w_br_c[i].astype(BF16),
                    w_out[i].astype(BF16), n_ctx)
        xs = _mlp(xs, norm2_g[i], mods, w_ff1[i].astype(BF16), w_ff2[i].astype(BF16), final_g, n_ctx, last)
    return xs[:, n_ctx:]
```

```python
import functools
import math

import jax
import jax.numpy as jnp
import numpy as np
from jax import lax
from jax.experimental import pallas as pl
from jax.experimental.pallas import tpu as pltpu

F32 = jnp.float32
BF16 = jnp.bfloat16

D_MODEL = 1024
GRID_W = 64
N_HEADS = 16
HEAD_DIM = 64
WIN_H = 8
WIN_W = 16
ROPE_BASE = 10000.0
NB_B = 16
BS_B = 64
RG_C = 8.0
LNX_EPS = 64e-5
NORM_EPS = 1e-6
NEG_INF = -1e30
MASKED = -1e30
N_ATT = 3
G_XB, G_GB, G_R, G_K, G_V, G_MA, G_MB, G_MC, G_LR = range(9)
LR_W = 512
CHUNK = 64
VMEM_LIMIT = 58 * 1024 * 1024
PAIR = 2 * HEAD_DIM
SCAN_GROUP = 16


def _cparams(sem):
    return pltpu.CompilerParams(dimension_semantics=sem, vmem_limit_bytes=VMEM_LIMIT)


def _pick(n, cands):
    for c in cands:
        if n % c == 0:
            return c
    raise ValueError(f"no tile for {n}")


def _dot(a, b):
    return jnp.dot(a, b, preferred_element_type=F32)


def _dot_nt(a, b):
    return lax.dot_general(a, b, (((1,), (1,)), ((), ())), preferred_element_type=F32)


def _dot_tn(a, b):
    return lax.dot_general(a, b, (((0,), (0,)), ((), ())), preferred_element_type=F32)


def _sigmoid(x):
    return 0.5 * jnp.tanh(0.5 * x) + 0.5


def _row_select(mod_ref, idx, rows, n_ctx):
    return jnp.where(rows < n_ctx, mod_ref[0, 0, idx:idx + 1, :], mod_ref[0, 1, idx:idx + 1, :])


def _seg_sum64(x):
    h1 = x.astype(BF16)
    r1 = x - h1.astype(F32)
    h2 = r1.astype(BF16)
    h3 = (r1 - h2.astype(F32)).astype(BF16)
    gi = lax.broadcasted_iota(jnp.int32, (3 * PAIR, PAIR), 0) % PAIR
    gj = lax.broadcasted_iota(jnp.int32, (3 * PAIR, PAIR), 1)
    ones = jnp.where(gi // HEAD_DIM == gj // HEAD_DIM, 1.0, 0.0).astype(BF16)
    out = []
    for p in range(x.shape[-1] // PAIR):
        sl = slice(p * PAIR, (p + 1) * PAIR)
        out.append(_dot(jnp.concatenate([h1[:, sl], h2[:, sl], h3[:, sl]], axis=1), ones))
    return jnp.concatenate(out, axis=1)


def _seq_block(d, i, nt, nctx):
    rev = jnp.where(i < nctx, nctx - 1 - i, nt - 1 - (i - nctx))
    return jnp.where(d == 0, i, rev)


def _ada_kernel(a_ref, w_ref, b_ref, o_ref):
    a = a_ref[...]
    s = a * _sigmoid(a)
    o_ref[...] = _dot(s.astype(BF16), w_ref[...].astype(BF16)) + b_ref[...]


def _ada(cvec, w, b):
    m, k = cvec.shape
    n = w.shape[1]
    tn = _pick(n, (1536, 1024, 512, 128))
    return pl.pallas_call(
        _ada_kernel,
        out_shape=jax.ShapeDtypeStruct((m, n), F32),
        grid=(n // tn,),
        in_specs=[pl.BlockSpec((m, k), lambda j: (0, 0)),
                  pl.BlockSpec((k, tn), lambda j: (0, j)),
                  pl.BlockSpec((1, tn), lambda j: (0, j))],
        out_specs=pl.BlockSpec((m, tn), lambda j: (0, j)),
        compiler_params=_cparams(("arbitrary",)),
        name="ada",
    )(cvec, w, b.reshape(1, n))


def _inproj_kernel(x_ref, g_ref, mod_ref, w_ref, c_ref, s1_ref, s2_ref, z_ref, q_ref, k_ref, v_ref, hb_ref,
                   *, tm, n_ctx):
    i = pl.program_id(1)
    j = pl.program_id(2)

    @pl.when(j == 0)
    def _():
        x = x_ref[0]
        ms = jnp.mean(x * x, axis=-1, keepdims=True)
        y = x * lax.rsqrt(ms + NORM_EPS) * g_ref[...]
        rows = i * tm + lax.broadcasted_iota(jnp.int32, (tm, 1), 0)
        sh = _row_select(mod_ref, 0, rows, n_ctx)
        sc = _row_select(mod_ref, 1, rows, n_ctx)
        hb_ref[...] = (y * (1.0 + sc) + sh).astype(BF16)

    z_ref[0, 0] = _dot(hb_ref[...], w_ref[...])
    d = z_ref.shape[-1]
    quarter = HEAD_DIM // 4

    def rope():
        cos, s1, s2 = c_ref[...], s1_ref[...], s2_ref[...]
        out = []
        for p in range(d // PAIR):
            xp = z_ref[0, 0, :, p * PAIR:(p + 1) * PAIR]
            out.append(xp * cos + pltpu.roll(xp, PAIR - quarter, 1) * s1 + pltpu.roll(xp, quarter, 1) * s2)
        return jnp.concatenate(out, axis=1)

    @pl.when(j == 0)
    def _():
        q_ref[0] = (rope() * (HEAD_DIM ** -0.5)).astype(BF16)

    @pl.when(j == 1)
    def _():
        k_ref[0] = rope().astype(BF16)

    @pl.when(j == 2)
    def _():
        v_ref[0] = z_ref[0, 0].astype(BF16)


def _inproj(x, g, mods, w, tables, n_ctx):
    bsz, t, d = x.shape
    ngrp = w.shape[1] // d
    tm = _pick(t, (1408, 768, 256))
    tspec = pl.BlockSpec((tm, PAIR), lambda b, i, j: (i, 0))
    aspec = pl.BlockSpec((1, tm, d), lambda b, i, j: (b, i, 0))
    ashape = jax.ShapeDtypeStruct((bsz, t, d), BF16)
    return pl.pallas_call(
        functools.partial(_inproj_kernel, tm=tm, n_ctx=n_ctx),
        out_shape=(jax.ShapeDtypeStruct((ngrp - N_ATT, bsz, t, d), F32), ashape, ashape, ashape),
        grid=(bsz, t // tm, ngrp),
        in_specs=[pl.BlockSpec((1, tm, d), lambda b, i, j: (b, i, 0)),
                  pl.BlockSpec((1, d), lambda b, i, j: (0, 0)),
                  pl.BlockSpec((1, 2, 6, d), lambda b, i, j: (b, 0, 0, 0)),
                  pl.BlockSpec((d, d), lambda b, i, j: (0, j)),
                  tspec, tspec, tspec],
        out_specs=(pl.BlockSpec((1, 1, tm, d), lambda b, i, j: (jnp.maximum(j - N_ATT, 0), b, i, 0)),
                   aspec, aspec, aspec),
        scratch_shapes=[pltpu.VMEM((tm, d), BF16)],
        compiler_params=_cparams(("parallel", "parallel", "arbitrary")),
        name="inproj",
    )(x, g.reshape(1, d), mods, w, *tables)


def _rope_tables(t, n_ctx):
    nf = HEAD_DIM // 4
    inv_freq = ROPE_BASE ** (-jnp.arange(nf, dtype=F32) / nf)
    tok = jnp.arange(t) - n_ctx
    row = (tok // GRID_W).astype(F32)
    col = (tok % GRID_W).astype(F32)
    lane = np.arange(PAIR) % HEAD_DIM
    use_col = jnp.asarray(lane >= HEAD_DIM // 2)
    first = jnp.asarray((lane % (HEAD_DIM // 2)) < nf)
    pos = jnp.where(use_col[None, :], col[:, None], row[:, None])
    ang = pos * inv_freq[lane % nf][None, :]
    is_lat = (tok >= 0)[:, None]
    cos = jnp.where(is_lat, jnp.cos(ang), 1.0)
    sin = jnp.where(is_lat, jnp.sin(ang), 0.0)
    return cos, jnp.where(first[None, :], -sin, 0.0), jnp.where(first[None, :], 0.0, sin)


def _natten_kernel(q_ref, *refs):
    k_refs = refs[0:WIN_H]
    v_refs = refs[WIN_H:2 * WIN_H]
    kc_ref, vc_ref = refs[2 * WIN_H:2 * WIN_H + 2]
    tb_refs = refs[2 * WIN_H + 2:-1]
    o_ref = refs[-1]
    bsz, nq = q_ref.shape[0], q_ref.shape[1]
    nloc = WIN_H * GRID_W
    lane = lax.broadcasted_iota(jnp.int32, (nq, PAIR), 1)
    first = lane < HEAD_DIM
    zero = jnp.zeros((nq, PAIR), BF16)
    chains = [(b, p, slice(p * PAIR, (p + 1) * PAIR)) for b in range(bsz) for p in range(N_HEADS // 2)]
    qs = [jnp.concatenate([jnp.where(first, q_ref[b, :, sl], zero), jnp.where(first, zero, q_ref[b, :, sl])], axis=0)
          for b, _, sl in chains]
    ks = [jnp.concatenate([r[b, :, sl] for r in k_refs] + [kc_ref[b, :, sl]], axis=0) for b, _, sl in chains]
    scores = [_dot_nt(q, k) for q, k in zip(qs, ks)]
    probs = []
    invs = []
    for (_, p, _), s in zip(chains, scores):
        tb = jnp.concatenate([t[0, p] for t in tb_refs], axis=1)
        sl_ = s[:, :nloc] + tb
        sc_ = s[:, nloc:]
        m = jnp.maximum(jnp.max(sl_, axis=-1, keepdims=True), jnp.max(sc_, axis=-1, keepdims=True))
        el_ = jnp.exp(sl_ - m)
        ec_ = jnp.exp(sc_ - m)
        invs.append(1.0 / (jnp.sum(el_, axis=-1, keepdims=True) + jnp.sum(ec_, axis=-1, keepdims=True)))
        probs.append(jnp.concatenate([el_.astype(BF16), ec_.astype(BF16)], axis=1))
    vs = [jnp.concatenate([r[b, :, sl] for r in v_refs] + [vc_ref[b, :, sl]], axis=0) for b, _, sl in chains]
    outs = [_dot(pr, v) * inv for pr, v, inv in zip(probs, vs, invs)]
    for (b, _, sl), o in zip(chains, outs):
        o_ref[b, :, sl] = jnp.where(first, o[:nq], o[nq:]).astype(BF16)


def _natten_table(rpb):
    col = np.arange(GRID_W)
    c0 = np.clip(col - WIN_W // 2, 0, GRID_W - WIN_W)
    in_win = (col[None, :] >= c0[:, None]) & (col[None, :] < c0[:, None] + WIN_W)
    padw = GRID_W - WIN_W
    padded = jnp.pad(rpb.astype(F32), ((0, 0), (0, 0), (padw, padw)))
    bias = jnp.stack([padded[:, :, GRID_W - 1 - q:2 * GRID_W - 1 - q] for q in range(GRID_W)], axis=2)
    b = jnp.where(jnp.asarray(in_win)[None, None], bias, MASKED)
    b = jnp.transpose(b, (1, 0, 2, 3)).reshape(2 * WIN_H - 1, N_HEADS // 2, 2 * GRID_W, GRID_W)
    two = jnp.concatenate([b[:-1], b[1:]], axis=-1)
    return jnp.concatenate([two, jnp.full((1,) + two.shape[1:], MASKED, F32)], axis=0)


def _natten(qr, kr, vb, table, n_ctx):
    bsz, t, d = qr.shape
    nqc = n_ctx // GRID_W
    rows = (t - n_ctx) // GRID_W
    assert rows >= WIN_H
    masked_entry = table.shape[0] - 1

    def r0(s):
        return jnp.clip(s - nqc - WIN_H // 2, 0, rows - WIN_H)

    def entry(s, c):
        return jnp.where(s < nqc, masked_entry, 2 * c + WIN_H - 1 - (s - nqc - r0(s)))

    blk = (bsz, GRID_W, d)
    kv_specs = [pl.BlockSpec(blk, functools.partial(lambda s, i: (0, nqc + r0(s) + i, 0), i=i))
                for i in range(WIN_H)]
    ctx_spec = pl.BlockSpec((bsz, n_ctx, d), lambda s: (0, 0, 0))
    tb_specs = [pl.BlockSpec((1,) + table.shape[1:], functools.partial(lambda s, c: (entry(s, c), 0, 0, 0), c=c))
                for c in range(WIN_H // 2)]
    return pl.pallas_call(
        _natten_kernel,
        out_shape=jax.ShapeDtypeStruct((bsz, t, d), BF16),
        grid=(t // GRID_W,),
        in_specs=[pl.BlockSpec(blk, lambda s: (0, s, 0))] + kv_specs + kv_specs + [ctx_spec, ctx_spec] + tb_specs,
        out_specs=pl.BlockSpec(blk, lambda s: (0, s, 0)),
        compiler_params=_cparams(("arbitrary",)),
        name="natten",
    )(qr, *([kr] * WIN_H), *([vb] * WIN_H), kr, vb, *([table] * (WIN_H // 2)))


def _block_diag_groups(w, per):
    two, nb, bs, _ = w.shape
    w = w.reshape(two, nb // per, per, bs, bs)
    eye = jnp.eye(per, dtype=w.dtype)
    return jnp.einsum('dgaij,ab->dgaibj', w, eye).reshape(two, nb // per, per * bs, per * bs)


def _rwkv_prep_kernel(*refs, tb, nt, nctx, has_vres):
    (zr, zr_p, zr_n, zk, zk_p, zk_n, zv, zv_p, zv_n, zl, zl_p, zl_n) = refs[:12]
    refs = refs[12:]
    if has_vres:
        vf_ref, refs = refs[0], refs[1:]
    (mu_ref, kk_ref, ka_ref, rk_ref, w0_ref, a0_ref, v0_ref, w2_ref,
     r_o, v_o, kk_o, g_o, bonus_o, lw_o, kd_o, bb_o) = refs
    blk = pl.program_id(1)
    seq_first = (blk == 0) | (blk == nctx)
    seq_last = (blk == nctx - 1) | (blk == nt - 1)
    row = lax.broadcasted_iota(jnp.int32, (tb, 1), 0)

    def nbr_mean(main, prev_row, next_row):
        prev_row = jnp.where(seq_first, 0.0, prev_row)
        next_row = jnp.where(seq_last, 0.0, next_row)
        up = jnp.where(row == 0, prev_row, pltpu.roll(main, 1, 0))
        dn = jnp.where(row == tb - 1, next_row, pltpu.roll(main, tb - 1, 0))
        return 0.5 * (up + dn)

    def shifted(m_ref, p_ref, n_ref, mu):
        main = m_ref[0, 0]
        return main + mu * (nbr_mean(main, p_ref[0, 0, 7:8, :], n_ref[0, 0, 0:1, :]) - main)

    mu = mu_ref[...]
    r = shifted(zr, zr_p, zr_n, mu[0:1])
    k = shifted(zk, zk_p, zk_n, mu[1:2])
    v = shifted(zv, zv_p, zv_n, mu[2:3])

    lr = zl[0, 0, :, 0:LR_W] + nbr_mean(zl[0, 0, :, LR_W:2 * LR_W], zl_p[0, 0, 7:8, LR_W:2 * LR_W],
                                        zl_n[0, 0, 0:1, LR_W:2 * LR_W])
    col = lax.broadcasted_iota(jnp.int32, (1, LR_W), 1)
    act = jnp.where(col < 128, jnp.tanh(lr), jnp.where((col >= 256) & (col < 416), _sigmoid(lr), lr))
    actb = act.astype(BF16)
    s0 = actb[:, 0:256]
    s1 = actb[:, 256:512]
    g_o[0] = _dot(s1, w2_ref[4]).astype(BF16)
    if has_vres:
        v = v + (vf_ref[0].astype(F32) - v) * _sigmoid(v0_ref[...] + _dot(s1, w2_ref[5]))
    r_o[0] = r.astype(BF16)
    v_o[0] = v.astype(BF16)
    kkr = k * kk_ref[...]
    kk = kkr / jnp.maximum(jnp.sqrt(_seg_sum64(kkr * kkr)), 1e-12)
    kk_o[0] = kk.astype(BF16)
    ksum = jnp.zeros_like(k)
    for dd in range(2):
        zw = w0_ref[dd:dd + 1, :] + _dot(s0, w2_ref[dd])
        lw_o[dd, 0] = -math.exp(-0.5) * _sigmoid(zw)
        a = _sigmoid(a0_ref[dd:dd + 1, :] + _dot(s0, w2_ref[2 + dd]))
        kd = k * (1.0 + (a - 1.0) * ka_ref[...])
        kd_o[dd, 0] = kd.astype(BF16)
        bb_o[dd, 0] = (kk * a).astype(BF16)
        ksum = ksum + kd
    bonus_o[0] = (_seg_sum64(r * ksum * rk_ref[...]) * v).astype(BF16)


def _rwkv_prep(z, vfirst, mu_rkv, k_k, k_a, r_k, w0, a0, v0, w2cat, n_ctx):
    _, bsz, t, d = z.shape
    tb = 256
    nt = t // tb
    nctx = n_ctx // tb
    hb = tb // 8
    has_vres = vfirst is not None

    def triple(g):
        return [pl.BlockSpec((1, 1, tb, d), lambda b, i: (g, b, i, 0)),
                pl.BlockSpec((1, 1, 8, d), lambda b, i: (g, b, jnp.maximum(i * hb - 1, 0), 0)),
                pl.BlockSpec((1, 1, 8, d), lambda b, i: (g, b, jnp.minimum((i + 1) * hb, t // 8 - 1), 0))]

    tok = pl.BlockSpec((1, tb, d), lambda b, i: (b, i, 0))
    tok2 = pl.BlockSpec((2, 1, tb, d), lambda b, i: (0, b, i, 0))
    const = lambda shape: pl.BlockSpec(shape, lambda b, i: (0,) * len(shape))
    in_specs = triple(G_R) + triple(G_K) + triple(G_V) + triple(G_LR)
    args = [z] * 12
    if has_vres:
        in_specs.append(tok)
        args.append(vfirst)
    in_specs += [const((3, d)), const((1, d)), const((1, d)), const((1, d)), const((2, d)), const((2, d)),
                 const((1, d)), const(w2cat.shape)]
    args += [mu_rkv, k_k.reshape(1, d), k_a.reshape(1, d), r_k.reshape(1, d), w0, a0, v0.reshape(1, d), w2cat]
    one = jax.ShapeDtypeStruct((bsz, t, d), BF16)
    two = jax.ShapeDtypeStruct((2, bsz, t, d), BF16)
    return pl.pallas_call(
        functools.partial(_rwkv_prep_kernel, tb=tb, nt=nt, nctx=nctx, has_vres=has_vres),
        out_shape=(one, one, one, one, one, jax.ShapeDtypeStruct((2, bsz, t, d), F32), two, two),
        grid=(bsz, nt),
        in_specs=in_specs,
        out_specs=(tok, tok, tok, tok, tok, tok2, tok2, tok2),
        compiler_params=_cparams(("parallel", "parallel")),
        name="rwkv_prep",
    )(*args)


def _seq_kernel(r_ref, v_ref, kk_ref, lw_ref, kd_ref, bb_ref,
                u_ref, up_ref, un_ref, cw_ref, cb_ref, wa_ref, ba_ref, wx_ref, bx_ref, lam_ref,
                y_ref, hl_ref, s_ref, a_s, b_s, h_s, *, nc, nctx):
    d = pl.program_id(0)
    i = pl.program_id(1)
    bsz, n = r_ref.shape[0], r_ref.shape[1]
    rev = d == 1
    blk = _seq_block(d, i, nc, nctx)

    @pl.when(i == 0)
    def _():
        s_ref[...] = jnp.zeros_like(s_ref)
        h_s[...] = jnp.zeros_like(h_s)

    seq_first = (blk == 0) | (blk == nctx)
    seq_last = (blk == nctx - 1) | (blk == nc - 1)
    row = lax.broadcasted_iota(jnp.int32, (n, 1), 0)
    cw = cw_ref[...]
    nl = -lam_ref[0]
    softplus = jnp.maximum(nl, 0.0) + jnp.log(1.0 + jnp.exp(-jnp.abs(nl)))
    gw = wa_ref.shape[-1]
    for b in range(bsz):
        x = u_ref[0, b]
        prev = jnp.where(seq_first, 0.0, up_ref[0, b])
        nxt = jnp.where(seq_last, 0.0, un_ref[0, b])
        xm1 = jnp.where(row == 0, prev[7:8], pltpu.roll(x, 1, 0))
        xm2 = jnp.where(row == 0, prev[6:7], jnp.where(row == 1, prev[7:8], pltpu.roll(x, 2, 0)))
        xp1 = jnp.where(row == n - 1, nxt[0:1], pltpu.roll(x, n - 1, 0))
        u = cw[0:1] * xm2 + cw[1:2] * xm1 + cw[2:3] * x + cw[3:4] * xp1 + cb_ref[...]
        ub = u.astype(BF16)
        ngrp = u.shape[-1] // gw
        gr = jnp.concatenate([_dot(ub[:, g * gw:(g + 1) * gw], wa_ref[0, g]) for g in range(ngrp)], axis=1)
        gi = jnp.concatenate([_dot(ub[:, g * gw:(g + 1) * gw], wx_ref[0, g]) for g in range(ngrp)], axis=1)
        log_a = -RG_C * _sigmoid(gr + ba_ref[0]) * softplus
        a_s[b] = jnp.exp(log_a)
        b_s[b] = jnp.sqrt(1.0 - jnp.exp(2.0 * log_a)) * (_sigmoid(gi + bx_ref[0]) * u)

    hs = [h_s[b] for b in range(bsz)]
    for t in range(n):
        tt = jnp.where(rev, n - 1 - t, t)
        for b in range(bsz):
            hs[b] = a_s[b, pl.ds(tt, 1), :] * hs[b] + b_s[b, pl.ds(tt, 1), :]
            b_s[b, pl.ds(tt, 1), :] = hs[b]
    for b in range(bsz):
        h_s[b] = hs[b]
        hl_ref[0, b] = b_s[b].astype(BF16)

    ti = lax.broadcasted_iota(jnp.int32, (n, n), 0)
    tj = lax.broadcasted_iota(jnp.int32, (n, n), 1)
    order = jnp.where(rev, tj - ti, ti - tj)
    tri = jnp.where(order >= 0, 1.0, 0.0).astype(BF16)
    wide = []
    gls = []
    for b in range(bsz):
        lw = lw_ref[0, b]
        h1 = lw.astype(BF16)
        r1 = lw - h1.astype(F32)
        h2 = r1.astype(BF16)
        h3 = (r1 - h2.astype(F32)).astype(BF16)
        c = _dot(tri, h1) + _dot(tri, h2) + _dot(tri, h3)
        c_last = jnp.where(rev, c[0:1], c[n - 1:n])
        kd = kd_ref[0, b].astype(F32)
        bb = bb_ref[0, b].astype(F32)
        enc = jnp.exp(-c)
        el = jnp.exp(c_last - c)
        wide.append([t.astype(BF16) for t in (-kk_ref[b].astype(F32) * jnp.exp(c - lw),
                                              r_ref[b].astype(F32) * jnp.exp(c), bb * enc, kd * enc,
                                              v_ref[b], bb * el, kd * el)])
        gls.append(jnp.exp(c_last))

    lane = lax.broadcasted_iota(jnp.int32, (n, PAIR), 1)
    first = lane < HEAD_DIM
    si = lax.broadcasted_iota(jnp.int32, (2 * n, 2 * n), 0) & (n - 1)
    sj = lax.broadcasted_iota(jnp.int32, (2 * n, 2 * n), 1) & (n - 1)
    row2 = lax.broadcasted_iota(jnp.int32, (2 * n, 2 * n), 0)
    col2 = lax.broadcasted_iota(jnp.int32, (2 * n, 2 * n), 1)
    qi = jnp.where(rev, n - 1 - si, si)
    qj = jnp.where(rev, n - 1 - sj, sj)
    strict = qi > qj
    incl = qi >= qj
    eye = jnp.where(row2 == col2, 1.0, 0.0)
    levels = []
    blk = 1
    while blk < n:
        bi = qi // blk
        bj = qj // blk
        levels.append((bi // 2 == bj // 2) & (bi % 2 == 1) & (bj % 2 == 0))
        blk *= 2

    zero = jnp.zeros((n, PAIR), BF16)

    def stack(x, sl):
        return jnp.concatenate([jnp.where(first, x[:, sl], zero), jnp.where(first, zero, x[:, sl])], axis=0)

    zero2 = jnp.zeros((2 * n, 2 * n), BF16)
    chains = [(b, p) for b in range(bsz) for p in range(N_HEADS // 2)]
    for g0 in range(0, len(chains), SCAN_GROUP):
        grp = chains[g0:g0 + SCAN_GROUP]
        sls = [slice(p * PAIR, (p + 1) * PAIR) for _, p in grp]
        opnd = [[stack(t, sl) for t in wide[b]] for (b, _), sl in zip(grp, sls)]
        s2 = [s_ref[b, p] for b, p in grp]
        s2b = [s.astype(BF16) for s in s2]
        aa = [_dot_nt(jnp.concatenate([am, rm], axis=0), jnp.concatenate([bm, km], axis=0))
              for am, rm, bm, km, _, _, _ in opnd]
        a_ab = [jnp.where(strict, a[:2 * n, :2 * n], 0.0) for a in aa]
        a_abb = [a.astype(BF16) for a in a_ab]
        a_ak = [jnp.where(strict, a[:2 * n, 2 * n:], 0.0).astype(BF16) for a in aa]
        a_r = [jnp.concatenate([jnp.where(incl, a[2 * n:, :2 * n], 0.0), jnp.where(incl, a[2 * n:, 2 * n:], 0.0)],
                               axis=1).astype(BF16) for a in aa]
        x = [_dot_nt(o[0], sb) + _dot(ak, o[4]) for o, sb, ak in zip(opnd, s2b, a_ak)]
        tinv = [eye + jnp.where(levels[0], a, 0.0) for a in a_ab]
        for m in levels[1:]:
            tb = [t.astype(BF16) for t in tinv]
            half = [_dot(t, jnp.where(m, a, zero2)) for t, a in zip(tb, a_abb)]
            tinv = [t + _dot(h.astype(BF16), q) for t, h, q in zip(tinv, half, tb)]
        uv = [jnp.concatenate([_dot(t.astype(BF16), xx.astype(BF16)).astype(BF16), o[4]], axis=0)
              for t, xx, o in zip(tinv, x, opnd)]
        for j, (b, p) in enumerate(grp):
            rm, blm, klm = opnd[j][1], opnd[j][5], opnd[j][6]
            o = _dot_nt(rm, s2b[j]) + _dot(a_r[j], uv[j])
            y_ref[0, b, :, sls[j]] = (o[:n] + o[n:]).astype(BF16)
            s_ref[b, p] = s2[j] * gls[b][:, sls[j]] + _dot_tn(uv[j], jnp.concatenate([blm, klm], axis=0))


def _seq_mixers(r, v, kk, lw, kd, bb, z, conv_w, conv_b, wa, ba, wx, bx, lam, n_ctx):
    bsz, t, d = r.shape
    nc = t // CHUNK
    nctx = n_ctx // CHUNK
    hb = CHUNK // 8
    gw = wa.shape[-1]

    def blk_of(dd, i):
        return _seq_block(dd, i, nc, nctx)

    tok = pl.BlockSpec((bsz, CHUNK, d), lambda dd, i: (0, blk_of(dd, i), 0))
    tok2 = pl.BlockSpec((1, bsz, CHUNK, d), lambda dd, i: (dd, 0, blk_of(dd, i), 0))
    vec = lambda: pl.BlockSpec((1, 1, d), lambda dd, i: (dd, 0, 0))
    wspec = lambda: pl.BlockSpec((1, d // gw, gw, gw), lambda dd, i: (dd, 0, 0, 0))
    out = jax.ShapeDtypeStruct((2, bsz, t, d), BF16)
    return pl.pallas_call(
        functools.partial(_seq_kernel, nc=nc, nctx=nctx),
        out_shape=(out, out),
        grid=(2, nc),
        in_specs=[tok, tok, tok, tok2, tok2, tok2,
                  pl.BlockSpec((1, bsz, CHUNK, d), lambda dd, i: (G_XB, 0, blk_of(dd, i), 0)),
                  pl.BlockSpec((1, bsz, 8, d), lambda dd, i: (G_XB, 0, jnp.maximum(blk_of(dd, i) * hb - 1, 0), 0)),
                  pl.BlockSpec((1, bsz, 8, d),
                               lambda dd, i: (G_XB, 0, jnp.minimum((blk_of(dd, i) + 1) * hb, t // 8 - 1), 0)),
                  pl.BlockSpec((4, d), lambda dd, i: (0, 0)),
                  pl.BlockSpec((1, d), lambda dd, i: (0, 0)),
                  wspec(), vec(), wspec(), vec(), vec()],
        out_specs=(tok2, tok2),
        scratch_shapes=[pltpu.VMEM((bsz, N_HEADS // 2, PAIR, PAIR), F32),
                        pltpu.VMEM((bsz, CHUNK, d), F32), pltpu.VMEM((bsz, CHUNK, d), F32),
                        pltpu.VMEM((bsz, 1, d), F32)],
        compiler_params=_cparams(("arbitrary", "arbitrary")),
        name="seq_mixers",
    )(r, v, kk, lw, kd, bb, z, z, z, conv_w, conv_b.reshape(1, d), wa, ba.reshape(2, 1, d), wx, bx.reshape(2, 1, d),
      lam.reshape(2, 1, d))


def _gelu_tanh(x):
    return 0.5 * x * (1.0 + jnp.tanh(math.sqrt(2.0 / math.pi) * (x + 0.044715 * (x * x * x))))


def _merge_kernel(x_ref, ya_ref, hl_ref, ug_ref, yw_ref, bonus_ref, g_ref, ma_ref, mb_ref, mc_ref,
                  mod_ref, lnw_ref, lnb_ref, wa_ref, wb_ref, wc_ref, wo_ref, o_ref, *, tm, n_ctx):
    i = pl.program_id(1)
    yb = ((hl_ref[0, 0].astype(F32) + hl_ref[1, 0].astype(F32)) * _gelu_tanh(ug_ref[0, 0])).astype(BF16)
    y = yw_ref[0, 0].astype(F32) + yw_ref[1, 0].astype(F32)
    inv = 1.0 / HEAD_DIM
    mean = _seg_sum64(y) * inv
    yc = y - mean
    var = _seg_sum64(yc * yc) * inv
    yn = yc * lax.rsqrt(var + LNX_EPS) * lnw_ref[...] + lnb_ref[...]
    oc = ((yn + bonus_ref[0].astype(F32)) * g_ref[0].astype(F32)).astype(BF16)
    mixed = (_sigmoid(ma_ref[0, 0]) * _dot(ya_ref[0], wa_ref[...])
             + _sigmoid(mb_ref[0, 0]) * _dot(yb, wb_ref[...])
             + _sigmoid(mc_ref[0, 0]) * _dot(oc, wc_ref[...]))
    yo = _dot(mixed.astype(BF16), wo_ref[...])
    rows = i * tm + lax.broadcasted_iota(jnp.int32, (tm, 1), 0)
    o_ref[0] = x_ref[0] + _row_select(mod_ref, 2, rows, n_ctx) * yo


def _merge(x, ynat, hlru, z, yrw, bonus, g, mods, ln_w, ln_b, wa, wb, wc, wo, n_ctx):
    bsz, t, d = x.shape
    tm = 256
    tok = pl.BlockSpec((1, tm, d), lambda b, i: (b, i, 0))
    tok2 = pl.BlockSpec((2, 1, tm, d), lambda b, i: (0, b, i, 0))
    zspec = lambda gidx: pl.BlockSpec((1, 1, tm, d), lambda b, i: (gidx, b, i, 0))
    vec = pl.BlockSpec((1, d), lambda b, i: (0, 0))
    wspec = pl.BlockSpec((d, d), lambda b, i: (0, 0))
    return pl.pallas_call(
        functools.partial(_merge_kernel, tm=tm, n_ctx=n_ctx),
        out_shape=jax.ShapeDtypeStruct((bsz, t, d), F32),
        grid=(bsz, t // tm),
        in_specs=[tok, tok, tok2, zspec(G_GB), tok2, tok, tok, zspec(G_MA), zspec(G_MB), zspec(G_MC),
                  pl.BlockSpec((1, 2, 6, d), lambda b, i: (b, 0, 0, 0)), vec, vec,
                  wspec, wspec, wspec, wspec],
        out_specs=tok,
        compiler_params=_cparams(("parallel", "parallel")),
        name="merge",
    )(x, ynat, hlru, z, yrw, bonus, g, z, z, z, mods, ln_w.reshape(1, d), ln_b.reshape(1, d), wa, wb, wc, wo)


def _mlp_kernel(x_ref, g_ref, mod_ref, w1_ref, w2_ref, fg_ref, o_ref, hb_ref, acc_ref, *, tm, n_ctx, final):
    i = pl.program_id(1)
    k = pl.program_id(2)
    rows = i * tm + lax.broadcasted_iota(jnp.int32, (tm, 1), 0)

    @pl.when(k == 0)
    def _():
        x = x_ref[0]
        ms = jnp.mean(x * x, axis=-1, keepdims=True)
        y = x * lax.rsqrt(ms + NORM_EPS) * g_ref[...]
        sh = _row_select(mod_ref, 3, rows, n_ctx)
        sc = _row_select(mod_ref, 4, rows, n_ctx)
        hb_ref[...] = (y * (1.0 + sc) + sh).astype(BF16)
        acc_ref[...] = jnp.zeros_like(acc_ref)

    a = jnp.maximum(_dot(hb_ref[...], w1_ref[...]), 0.0)
    acc_ref[...] += _dot((a * a).astype(BF16), w2_ref[...])

    @pl.when(k == pl.num_programs(2) - 1)
    def _():
        xn = x_ref[0] + _row_select(mod_ref, 5, rows, n_ctx) * acc_ref[...]
        if final:
            ms = jnp.mean(xn * xn, axis=-1, keepdims=True)
            xn = xn * lax.rsqrt(ms + NORM_EPS) * fg_ref[...]
        o_ref[0] = xn


def _mlp(x, g, mods, w1, w2, final_g, n_ctx, final):
    bsz, t, d = x.shape
    dff = w1.shape[1]
    tk = 1024
    tm = _pick(t, (1408, 768, 256))
    vec = pl.BlockSpec((1, d), lambda b, i, k: (0, 0))
    tok = pl.BlockSpec((1, tm, d), lambda b, i, k: (b, i, 0))
    return pl.pallas_call(
        functools.partial(_mlp_kernel, tm=tm, n_ctx=n_ctx, final=final),
        out_shape=jax.ShapeDtypeStruct((bsz, t, d), F32),
        grid=(bsz, t // tm, dff // tk),
        in_specs=[tok, vec, pl.BlockSpec((1, 2, 6, d), lambda b, i, k: (b, 0, 0, 0)),
                  pl.BlockSpec((d, tk), lambda b, i, k: (0, k)),
                  pl.BlockSpec((tk, d), lambda b, i, k: (k, 0)), vec],
        out_specs=tok,
        scratch_shapes=[pltpu.VMEM((tm, d), BF16), pltpu.VMEM((tm, d), F32)],
        compiler_params=_cparams(("parallel", "parallel", "arbitrary")),
        name="mlp",
    )(x, g.reshape(1, d), mods, w1, w2, final_g.reshape(1, d))


def _lowrank_weights(mu_h, w1, a1, g1, v1, mu_v):
    d = mu_h.shape[-1]
    blocks = [(w1[0], mu_h[0]), (w1[1], mu_h[0]), (a1[0], mu_h[1]), (a1[1], mu_h[1]), (g1, mu_h[2])]
    if v1 is not None:
        blocks.append((v1, mu_v))
    p = jnp.concatenate([w * (1.0 - m)[:, None] for w, m in blocks], axis=1)
    q = jnp.concatenate([w * m[:, None] for w, m in blocks], axis=1)
    pad = lambda a: jnp.pad(a, ((0, 0), (0, LR_W - a.shape[1])))
    return jnp.concatenate([pad(p), pad(q)], axis=1)


def _second_stage_weights(w2, a2, g2, v2):
    d = w2.shape[-1]
    def place(w, off):
        return jnp.zeros((256, d), F32).at[off:off + w.shape[0]].set(w)
    v2p = place(v2, 160) if v2 is not None else jnp.zeros((256, d), F32)
    return jnp.stack([place(w2[0], 0), place(w2[1], 64), place(a2[0], 128), place(a2[1], 192),
                      place(g2, 0), v2p]).astype(BF16)


def kernel(x, c, ctx, c_ctx, w_ada, b_ada, norm1_g, norm2_g, w_in, rpb, conv_w, conv_b, rg_wa, rg_ba, rg_wx, rg_bx, rg_lam, rw_mu_rkv, rw_mu_h, rw_w0, rw_w1, rw_w2, rw_a0, rw_a1, rw_a2, rw_g1, rw_g2, rw_k_k, rw_k_a, rw_r_k, rw_ln_w, rw_ln_b, vres_v0, vres_v1, vres_v2, vres_mu, w_br_a, w_br_b, w_br_c, w_out, w_ff1, w_ff2, final_g):
    bsz, seq, d = x.shape
    n_ctx = ctx.shape[1]
    depth = w_in.shape[0]
    t = n_ctx + seq
    xs = jnp.concatenate([ctx, x], axis=1)
    cvec = jnp.zeros((8, d), F32).at[:bsz].set(c).at[bsz].set(c_ctx)
    tables = _rope_tables(t, n_ctx)
    vfirst = None
    for i in range(depth):
        last = i == depth - 1
        mod = _ada(cvec, w_ada[i], b_ada[i])
        mods = jnp.stack([jnp.broadcast_to(mod[bsz].reshape(1, 6, d), (bsz, 6, d)),
                          mod[:bsz].reshape(bsz, 6, d)], axis=1)
        if i == 0:
            lr_w = _lowrank_weights(rw_mu_h[i], rw_w1[i], rw_a1[i], rw_g1[i], None, None)
            w2cat = _second_stage_weights(rw_w2[i], rw_a2[i], rw_g2[i], None)
            v0 = jnp.zeros((d,), F32)
        else:
            lr_w = _lowrank_weights(rw_mu_h[i], rw_w1[i], rw_a1[i], rw_g1[i], vres_v1[i - 1], vres_mu[i - 1])
            w2cat = _second_stage_weights(rw_w2[i], rw_a2[i], rw_g2[i], vres_v2[i - 1])
            v0 = vres_v0[i - 1]
        w_all = jnp.concatenate([w_in[i], lr_w], axis=1).astype(BF16)
        z, qr, kr, vb = _inproj(xs, norm1_g[i], mods, w_all, tables, n_ctx)
        ynat = _natten(qr, kr, vb, _natten_table(rpb[i]), n_ctx)
        r, v, kk, g, bonus, lw, kd, bb = _rwkv_prep(
            z, vfirst, rw_mu_rkv[i], rw_k_k[i], rw_k_a[i], rw_r_k[i].reshape(-1), rw_w0[i], rw_a0[i], v0,
            w2cat, n_ctx)
        if i == 0:
            vfirst = v
        yrw, hlru = _seq_mixers(r, v, kk, lw, kd, bb, z, conv_w[i], conv_b[i],
                                _block_diag_groups(rg_wa[i], 4).astype(BF16), rg_ba[i],
                                _block_diag_groups(rg_wx[i], 4).astype(BF16), rg_bx[i], rg_lam[i], n_ctx)
        xs = _merge(xs, ynat, hlru, z, yrw, bonus, g, mods, rw_ln_w[i], rw_ln_b[i],
                    w_br_a[i].astype(BF16), w_br_b[i].astype(BF16), w_br_c[i].astype(BF16),
                    w_out[i].astype(BF16), n_ctx)
        xs = _mlp(xs, norm2_g[i], mods, w_ff1[i].astype(BF16), w_ff2[i].astype(BF16), final_g, n_ctx, last)
    return xs[:, n_ctx:]
```

```python
import functools
import math

import jax
import jax.numpy as jnp
import numpy as np
from jax import lax
from jax.experimental import pallas as pl
from jax.experimental.pallas import tpu as pltpu

F32 = jnp.float32
BF16 = jnp.bfloat16

GRID_W = 64
N_HEADS = 16
HEAD_DIM = 64
WIN_H = 8
WIN_W = 16
ROPE_BASE = 10000.0
RG_C = 8.0
LNX_EPS = 64e-5
NORM_EPS = 1e-6
MASKED = -1e30
N_ATT = 3
G_XB, G_GB, G_R, G_K, G_V, G_MA, G_MB, G_MC, G_LR = range(9)
LR_W = 512
LR_SLAB = 256
R_DECAY, R_ICLR, R_GATE, R_VRES = 64, 64, 160, 32
LR_TANH_END = 2 * R_DECAY
LR_GATE_START = LR_TANH_END + 2 * R_ICLR
LR_GATE_END = LR_GATE_START + R_GATE
assert LR_GATE_START == LR_SLAB and LR_GATE_END + R_VRES <= LR_W
CHUNK = 64
VMEM_LIMIT = 58 * 1024 * 1024
PAIR = 2 * HEAD_DIM
SCAN_GROUP = 16


def _cparams(sem):
    return pltpu.CompilerParams(dimension_semantics=sem, vmem_limit_bytes=VMEM_LIMIT)


def _pick(n, cands):
    for c in cands:
        if n % c == 0:
            return c
    raise ValueError(f"no tile for {n}")


def _dot(a, b):
    return jnp.dot(a, b, preferred_element_type=F32)


def _dot_nt(a, b):
    return lax.dot_general(a, b, (((1,), (1,)), ((), ())), preferred_element_type=F32)


def _dot_tn(a, b):
    return lax.dot_general(a, b, (((0,), (0,)), ((), ())), preferred_element_type=F32)


def _sigmoid(x):
    return 0.5 * jnp.tanh(0.5 * x) + 0.5


def _row_select(mod_ref, idx, rows, n_ctx):
    return jnp.where(rows < n_ctx, mod_ref[0, 0, idx:idx + 1, :], mod_ref[0, 1, idx:idx + 1, :])


def _seg_sum64(x):
    h1 = x.astype(BF16)
    r1 = x - h1.astype(F32)
    h2 = r1.astype(BF16)
    h3 = (r1 - h2.astype(F32)).astype(BF16)
    gi = lax.broadcasted_iota(jnp.int32, (3 * PAIR, PAIR), 0) % PAIR
    gj = lax.broadcasted_iota(jnp.int32, (3 * PAIR, PAIR), 1)
    ones = jnp.where(gi // HEAD_DIM == gj // HEAD_DIM, 1.0, 0.0).astype(BF16)
    out = []
    for p in range(x.shape[-1] // PAIR):
        sl = slice(p * PAIR, (p + 1) * PAIR)
        out.append(_dot(jnp.concatenate([h1[:, sl], h2[:, sl], h3[:, sl]], axis=1), ones))
    return jnp.concatenate(out, axis=1)


def _seq_block(d, i, nt, nctx):
    rev = jnp.where(i < nctx, nctx - 1 - i, nt - 1 - (i - nctx))
    return jnp.where(d == 0, i, rev)


def _ada_kernel(a_ref, w_ref, b_ref, o_ref):
    a = a_ref[...]
    s = a * _sigmoid(a)
    o_ref[...] = _dot(s.astype(BF16), w_ref[...].astype(BF16)) + b_ref[...]


def _ada(cvec, w, b):
    m, k = cvec.shape
    n = w.shape[1]
    tn = _pick(n, (1536, 1024, 512, 128))
    return pl.pallas_call(
        _ada_kernel,
        out_shape=jax.ShapeDtypeStruct((m, n), F32),
        grid=(n // tn,),
        in_specs=[pl.BlockSpec((m, k), lambda j: (0, 0)),
                  pl.BlockSpec((k, tn), lambda j: (0, j)),
                  pl.BlockSpec((1, tn), lambda j: (0, j))],
        out_specs=pl.BlockSpec((m, tn), lambda j: (0, j)),
        compiler_params=_cparams(("arbitrary",)),
        name="ada",
    )(cvec, w, b.reshape(1, n))


def _inproj_kernel(x_ref, g_ref, mod_ref, w_ref, c_ref, s1_ref, s2_ref, z_ref, q_ref, k_ref, v_ref, hb_ref,
                   *, tm, n_ctx):
    i = pl.program_id(1)
    j = pl.program_id(2)

    @pl.when(j == 0)
    def _():
        x = x_ref[0]
        ms = jnp.mean(x * x, axis=-1, keepdims=True)
        y = x * lax.rsqrt(ms + NORM_EPS) * g_ref[...]
        rows = i * tm + lax.broadcasted_iota(jnp.int32, (tm, 1), 0)
        sh = _row_select(mod_ref, 0, rows, n_ctx)
        sc = _row_select(mod_ref, 1, rows, n_ctx)
        hb_ref[...] = (y * (1.0 + sc) + sh).astype(BF16)

    z_ref[0, 0] = _dot(hb_ref[...], w_ref[...])
    d = z_ref.shape[-1]
    quarter = HEAD_DIM // 4

    def rope():
        cos, s1, s2 = c_ref[...], s1_ref[...], s2_ref[...]
        out = []
        for p in range(d // PAIR):
            xp = z_ref[0, 0, :, p * PAIR:(p + 1) * PAIR]
            out.append(xp * cos + pltpu.roll(xp, PAIR - quarter, 1) * s1 + pltpu.roll(xp, quarter, 1) * s2)
        return jnp.concatenate(out, axis=1)

    @pl.when(j == 0)
    def _():
        q_ref[0] = (rope() * (HEAD_DIM ** -0.5)).astype(BF16)

    @pl.when(j == 1)
    def _():
        k_ref[0] = rope().astype(BF16)

    @pl.when(j == 2)
    def _():
        v_ref[0] = z_ref[0, 0].astype(BF16)


def _inproj(x, g, mods, w, tables, n_ctx):
    bsz, t, d = x.shape
    ngrp = w.shape[1] // d
    tm = _pick(t, (1408, 768, 256))
    tspec = pl.BlockSpec((tm, PAIR), lambda b, i, j: (i, 0))
    aspec = pl.BlockSpec((1, tm, d), lambda b, i, j: (b, i, 0))
    ashape = jax.ShapeDtypeStruct((bsz, t, d), BF16)
    return pl.pallas_call(
        functools.partial(_inproj_kernel, tm=tm, n_ctx=n_ctx),
        out_shape=(jax.ShapeDtypeStruct((ngrp - N_ATT, bsz, t, d), F32), ashape, ashape, ashape),
        grid=(bsz, t // tm, ngrp),
        in_specs=[pl.BlockSpec((1, tm, d), lambda b, i, j: (b, i, 0)),
                  pl.BlockSpec((1, d), lambda b, i, j: (0, 0)),
                  pl.BlockSpec((1, 2, 6, d), lambda b, i, j: (b, 0, 0, 0)),
                  pl.BlockSpec((d, d), lambda b, i, j: (0, j)),
                  tspec, tspec, tspec],
        out_specs=(pl.BlockSpec((1, 1, tm, d), lambda b, i, j: (jnp.maximum(j - N_ATT, 0), b, i, 0)),
                   aspec, aspec, aspec),
        scratch_shapes=[pltpu.VMEM((tm, d), BF16)],
        compiler_params=_cparams(("parallel", "parallel", "arbitrary")),
        name="inproj",
    )(x, g.reshape(1, d), mods, w, *tables)


def _rope_tables(t, n_ctx):
    nf = HEAD_DIM // 4
    inv_freq = ROPE_BASE ** (-jnp.arange(nf, dtype=F32) / nf)
    tok = jnp.arange(t) - n_ctx
    row = (tok // GRID_W).astype(F32)
    col = (tok % GRID_W).astype(F32)
    lane = np.arange(PAIR) % HEAD_DIM
    use_col = jnp.asarray(lane >= HEAD_DIM // 2)
    first = jnp.asarray((lane % (HEAD_DIM // 2)) < nf)
    pos = jnp.where(use_col[None, :], col[:, None], row[:, None])
    ang = pos * inv_freq[lane % nf][None, :]
    is_lat = (tok >= 0)[:, None]
    cos = jnp.where(is_lat, jnp.cos(ang), 1.0)
    sin = jnp.where(is_lat, jnp.sin(ang), 0.0)
    return cos, jnp.where(first[None, :], -sin, 0.0), jnp.where(first[None, :], 0.0, sin)


def _natten_kernel(q_ref, *refs):
    k_refs = refs[0:WIN_H]
    v_refs = refs[WIN_H:2 * WIN_H]
    kc_ref, vc_ref = refs[2 * WIN_H:2 * WIN_H + 2]
    tb_refs = refs[2 * WIN_H + 2:-1]
    o_ref = refs[-1]
    bsz, nq = q_ref.shape[0], q_ref.shape[1]
    nloc = WIN_H * GRID_W
    lane = lax.broadcasted_iota(jnp.int32, (nq, PAIR), 1)
    first = lane < HEAD_DIM
    zero = jnp.zeros((nq, PAIR), BF16)
    chains = [(b, p, slice(p * PAIR, (p + 1) * PAIR)) for b in range(bsz) for p in range(N_HEADS // 2)]
    qs = [jnp.concatenate([jnp.where(first, q_ref[b, :, sl], zero), jnp.where(first, zero, q_ref[b, :, sl])], axis=0)
          for b, _, sl in chains]
    ks = [jnp.concatenate([r[b, :, sl] for r in k_refs] + [kc_ref[b, :, sl]], axis=0) for b, _, sl in chains]
    scores = [_dot_nt(q, k) for q, k in zip(qs, ks)]
    probs = []
    invs = []
    for (_, p, _), s in zip(chains, scores):
        tb = jnp.concatenate([t[0, p] for t in tb_refs], axis=1)
        sl_ = s[:, :nloc] + tb
        sc_ = s[:, nloc:]
        m = jnp.maximum(jnp.max(sl_, axis=-1, keepdims=True), jnp.max(sc_, axis=-1, keepdims=True))
        el_ = jnp.exp(sl_ - m)
        ec_ = jnp.exp(sc_ - m)
        invs.append(1.0 / (jnp.sum(el_, axis=-1, keepdims=True) + jnp.sum(ec_, axis=-1, keepdims=True)))
        probs.append(jnp.concatenate([el_.astype(BF16), ec_.astype(BF16)], axis=1))
    vs = [jnp.concatenate([r[b, :, sl] for r in v_refs] + [vc_ref[b, :, sl]], axis=0) for b, _, sl in chains]
    outs = [_dot(pr, v) * inv for pr, v, inv in zip(probs, vs, invs)]
    for (b, _, sl), o in zip(chains, outs):
        o_ref[b, :, sl] = jnp.where(first, o[:nq], o[nq:]).astype(BF16)


def _natten_table(rpb):
    col = np.arange(GRID_W)
    c0 = np.clip(col - WIN_W // 2, 0, GRID_W - WIN_W)
    in_win = (col[None, :] >= c0[:, None]) & (col[None, :] < c0[:, None] + WIN_W)
    padw = GRID_W - WIN_W
    padded = jnp.pad(rpb.astype(F32), ((0, 0), (0, 0), (padw, padw)))
    bias = jnp.stack([padded[:, :, GRID_W - 1 - q:2 * GRID_W - 1 - q] for q in range(GRID_W)], axis=2)
    b = jnp.where(jnp.asarray(in_win)[None, None], bias, MASKED)
    b = jnp.transpose(b, (1, 0, 2, 3)).reshape(2 * WIN_H - 1, N_HEADS // 2, 2 * GRID_W, GRID_W)
    two = jnp.concatenate([b[:-1], b[1:]], axis=-1)
    return jnp.concatenate([two, jnp.full((1,) + two.shape[1:], MASKED, F32)], axis=0)


def _natten(qr, kr, vb, table, n_ctx):
    bsz, t, d = qr.shape
    nqc = n_ctx // GRID_W
    rows = (t - n_ctx) // GRID_W
    assert rows >= WIN_H
    masked_entry = table.shape[0] - 1

    def r0(s):
        return jnp.clip(s - nqc - WIN_H // 2, 0, rows - WIN_H)

    def entry(s, c):
        return jnp.where(s < nqc, masked_entry, 2 * c + WIN_H - 1 - (s - nqc - r0(s)))

    blk = (bsz, GRID_W, d)
    kv_specs = [pl.BlockSpec(blk, functools.partial(lambda s, i: (0, nqc + r0(s) + i, 0), i=i))
                for i in range(WIN_H)]
    ctx_spec = pl.BlockSpec((bsz, n_ctx, d), lambda s: (0, 0, 0))
    tb_specs = [pl.BlockSpec((1,) + table.shape[1:], functools.partial(lambda s, c: (entry(s, c), 0, 0, 0), c=c))
                for c in range(WIN_H // 2)]
    return pl.pallas_call(
        _natten_kernel,
        out_shape=jax.ShapeDtypeStruct((bsz, t, d), BF16),
        grid=(t // GRID_W,),
        in_specs=[pl.BlockSpec(blk, lambda s: (0, s, 0))] + kv_specs + kv_specs + [ctx_spec, ctx_spec] + tb_specs,
        out_specs=pl.BlockSpec(blk, lambda s: (0, s, 0)),
        compiler_params=_cparams(("arbitrary",)),
        name="natten",
    )(qr, *([kr] * WIN_H), *([vb] * WIN_H), kr, vb, *([table] * (WIN_H // 2)))


def _block_diag_groups(w, per):
    two, nb, bs, _ = w.shape
    w = w.reshape(two, nb // per, per, bs, bs)
    eye = jnp.eye(per, dtype=w.dtype)
    return jnp.einsum('dgaij,ab->dgaibj', w, eye).reshape(two, nb // per, per * bs, per * bs)


def _rwkv_prep_kernel(*refs, tb, nt, nctx, has_vres):
    (zr, zr_p, zr_n, zk, zk_p, zk_n, zv, zv_p, zv_n, zl, zl_p, zl_n) = refs[:12]
    refs = refs[12:]
    if has_vres:
        vf_ref, refs = refs[0], refs[1:]
    (mu_ref, kk_ref, ka_ref, rk_ref, w0_ref, a0_ref, v0_ref, w2_ref,
     r_o, v_o, kk_o, g_o, bonus_o, lw_o, kd_o, bb_o) = refs
    blk = pl.program_id(1)
    seq_first = (blk == 0) | (blk == nctx)
    seq_last = (blk == nctx - 1) | (blk == nt - 1)
    row = lax.broadcasted_iota(jnp.int32, (tb, 1), 0)

    def nbr_mean(main, prev_row, next_row):
        prev_row = jnp.where(seq_first, 0.0, prev_row)
        next_row = jnp.where(seq_last, 0.0, next_row)
        up = jnp.where(row == 0, prev_row, pltpu.roll(main, 1, 0))
        dn = jnp.where(row == tb - 1, next_row, pltpu.roll(main, tb - 1, 0))
        return 0.5 * (up + dn)

    def shifted(m_ref, p_ref, n_ref, mu):
        main = m_ref[0, 0]
        return main + mu * (nbr_mean(main, p_ref[0, 0, 7:8, :], n_ref[0, 0, 0:1, :]) - main)

    mu = mu_ref[...]
    r = shifted(zr, zr_p, zr_n, mu[0:1])
    k = shifted(zk, zk_p, zk_n, mu[1:2])
    v = shifted(zv, zv_p, zv_n, mu[2:3])

    lr = zl[0, 0, :, 0:LR_W] + nbr_mean(zl[0, 0, :, LR_W:2 * LR_W], zl_p[0, 0, 7:8, LR_W:2 * LR_W],
                                        zl_n[0, 0, 0:1, LR_W:2 * LR_W])
    col = lax.broadcasted_iota(jnp.int32, (1, LR_W), 1)
    act = jnp.where(col < LR_TANH_END, jnp.tanh(lr),
                    jnp.where((col >= LR_GATE_START) & (col < LR_GATE_END), _sigmoid(lr), lr))
    actb = act.astype(BF16)
    s0 = actb[:, 0:LR_SLAB]
    s1 = actb[:, LR_SLAB:2 * LR_SLAB]
    g_o[0] = _dot(s1, w2_ref[4]).astype(BF16)
    if has_vres:
        v = v + (vf_ref[0].astype(F32) - v) * _sigmoid(v0_ref[...] + _dot(s1, w2_ref[5]))
    r_o[0] = r.astype(BF16)
    v_o[0] = v.astype(BF16)
    kkr = k * kk_ref[...]
    kk = kkr / jnp.maximum(jnp.sqrt(_seg_sum64(kkr * kkr)), 1e-12)
    kk_o[0] = kk.astype(BF16)
    ksum = jnp.zeros_like(k)
    for dd in range(2):
        zw = w0_ref[dd:dd + 1, :] + _dot(s0, w2_ref[dd])
        lw_o[dd, 0] = -math.exp(-0.5) * _sigmoid(zw)
        a = _sigmoid(a0_ref[dd:dd + 1, :] + _dot(s0, w2_ref[2 + dd]))
        kd = k * (1.0 + (a - 1.0) * ka_ref[...])
        kd_o[dd, 0] = kd.astype(BF16)
        bb_o[dd, 0] = (kk * a).astype(BF16)
        ksum = ksum + kd
    bonus_o[0] = (_seg_sum64(r * ksum * rk_ref[...]) * v).astype(BF16)


def _rwkv_prep(z, vfirst, mu_rkv, k_k, k_a, r_k, w0, a0, v0, w2cat, n_ctx):
    _, bsz, t, d = z.shape
    tb = 256
    nt = t // tb
    nctx = n_ctx // tb
    hb = tb // 8
    has_vres = vfirst is not None

    def triple(g):
        return [pl.BlockSpec((1, 1, tb, d), lambda b, i: (g, b, i, 0)),
                pl.BlockSpec((1, 1, 8, d), lambda b, i: (g, b, jnp.maximum(i * hb - 1, 0), 0)),
                pl.BlockSpec((1, 1, 8, d), lambda b, i: (g, b, jnp.minimum((i + 1) * hb, t // 8 - 1), 0))]

    tok = pl.BlockSpec((1, tb, d), lambda b, i: (b, i, 0))
    tok2 = pl.BlockSpec((2, 1, tb, d), lambda b, i: (0, b, i, 0))
    const = lambda shape: pl.BlockSpec(shape, lambda b, i: (0,) * len(shape))
    in_specs = triple(G_R) + triple(G_K) + triple(G_V) + triple(G_LR)
    args = [z] * 12
    if has_vres:
        in_specs.append(tok)
        args.append(vfirst)
    in_specs += [const((3, d)), const((1, d)), const((1, d)), const((1, d)), const((2, d)), const((2, d)),
                 const((1, d)), const(w2cat.shape)]
    args += [mu_rkv, k_k.reshape(1, d), k_a.reshape(1, d), r_k.reshape(1, d), w0, a0, v0.reshape(1, d), w2cat]
    one = jax.ShapeDtypeStruct((bsz, t, d), BF16)
    two = jax.ShapeDtypeStruct((2, bsz, t, d), BF16)
    return pl.pallas_call(
        functools.partial(_rwkv_prep_kernel, tb=tb, nt=nt, nctx=nctx, has_vres=has_vres),
        out_shape=(one, one, one, one, one, jax.ShapeDtypeStruct((2, bsz, t, d), F32), two, two),
        grid=(bsz, nt),
        in_specs=in_specs,
        out_specs=(tok, tok, tok, tok, tok, tok2, tok2, tok2),
        compiler_params=_cparams(("parallel", "parallel")),
        name="rwkv_prep",
    )(*args)


def _seq_kernel(r_ref, v_ref, kk_ref, lw_ref, kd_ref, bb_ref,
                u_ref, up_ref, un_ref, cw_ref, cb_ref, wa_ref, ba_ref, wx_ref, bx_ref, lam_ref,
                y_ref, hl_ref, s_ref, a_s, b_s, h_s, *, nc, nctx):
    d = pl.program_id(0)
    i = pl.program_id(1)
    bsz, n = r_ref.shape[0], r_ref.shape[1]
    rev = d == 1
    blk = _seq_block(d, i, nc, nctx)

    @pl.when(i == 0)
    def _():
        s_ref[...] = jnp.zeros_like(s_ref)
        h_s[...] = jnp.zeros_like(h_s)

    seq_first = (blk == 0) | (blk == nctx)
    seq_last = (blk == nctx - 1) | (blk == nc - 1)
    row = lax.broadcasted_iota(jnp.int32, (n, 1), 0)
    cw = cw_ref[...]
    nl = -lam_ref[0]
    softplus = jnp.maximum(nl, 0.0) + jnp.log(1.0 + jnp.exp(-jnp.abs(nl)))
    gw = wa_ref.shape[-1]
    for b in range(bsz):
        x = u_ref[0, b]
        prev = jnp.where(seq_first, 0.0, up_ref[0, b])
        nxt = jnp.where(seq_last, 0.0, un_ref[0, b])
        xm1 = jnp.where(row == 0, prev[7:8], pltpu.roll(x, 1, 0))
        xm2 = jnp.where(row == 0, prev[6:7], jnp.where(row == 1, prev[7:8], pltpu.roll(x, 2, 0)))
        xp1 = jnp.where(row == n - 1, nxt[0:1], pltpu.roll(x, n - 1, 0))
        u = cw[0:1] * xm2 + cw[1:2] * xm1 + cw[2:3] * x + cw[3:4] * xp1 + cb_ref[...]
        ub = u.astype(BF16)
        ngrp = u.shape[-1] // gw
        gr = jnp.concatenate([_dot(ub[:, g * gw:(g + 1) * gw], wa_ref[0, g]) for g in range(ngrp)], axis=1)
        gi = jnp.concatenate([_dot(ub[:, g * gw:(g + 1) * gw], wx_ref[0, g]) for g in range(ngrp)], axis=1)
        log_a = -RG_C * _sigmoid(gr + ba_ref[0]) * softplus
        a_s[b] = jnp.exp(log_a)
        b_s[b] = jnp.sqrt(1.0 - jnp.exp(2.0 * log_a)) * (_sigmoid(gi + bx_ref[0]) * u)

    hs = [h_s[b] for b in range(bsz)]
    for t in range(n):
        tt = jnp.where(rev, n - 1 - t, t)
        for b in range(bsz):
            hs[b] = a_s[b, pl.ds(tt, 1), :] * hs[b] + b_s[b, pl.ds(tt, 1), :]
            b_s[b, pl.ds(tt, 1), :] = hs[b]
    for b in range(bsz):
        h_s[b] = hs[b]
        hl_ref[0, b] = b_s[b].astype(BF16)

    ti = lax.broadcasted_iota(jnp.int32, (n, n), 0)
    tj = lax.broadcasted_iota(jnp.int32, (n, n), 1)
    order = jnp.where(rev, tj - ti, ti - tj)
    tri = jnp.where(order >= 0, 1.0, 0.0).astype(BF16)
    wide = []
    gls = []
    for b in range(bsz):
        lw = lw_ref[0, b]
        h1 = lw.astype(BF16)
        r1 = lw - h1.astype(F32)
        h2 = r1.astype(BF16)
        h3 = (r1 - h2.astype(F32)).astype(BF16)
        c = _dot(tri, h1) + _dot(tri, h2) + _dot(tri, h3)
        c_last = jnp.where(rev, c[0:1], c[n - 1:n])
        kd = kd_ref[0, b].astype(F32)
        bb = bb_ref[0, b].astype(F32)
        enc = jnp.exp(-c)
        el = jnp.exp(c_last - c)
        wide.append([t.astype(BF16) for t in (-kk_ref[b].astype(F32) * jnp.exp(c - lw),
                                              r_ref[b].astype(F32) * jnp.exp(c), bb * enc, kd * enc,
                                              v_ref[b], bb * el, kd * el)])
        gls.append(jnp.exp(c_last))

    lane = lax.broadcasted_iota(jnp.int32, (n, PAIR), 1)
    first = lane < HEAD_DIM
    si = lax.broadcasted_iota(jnp.int32, (2 * n, 2 * n), 0) & (n - 1)
    sj = lax.broadcasted_iota(jnp.int32, (2 * n, 2 * n), 1) & (n - 1)
    row2 = lax.broadcasted_iota(jnp.int32, (2 * n, 2 * n), 0)
    col2 = lax.broadcasted_iota(jnp.int32, (2 * n, 2 * n), 1)
    qi = jnp.where(rev, n - 1 - si, si)
    qj = jnp.where(rev, n - 1 - sj, sj)
    strict = qi > qj
    incl = qi >= qj
    eye = jnp.where(row2 == col2, 1.0, 0.0)
    levels = []
    blk = 1
    while blk < n:
        bi = qi // blk
        bj = qj // blk
        levels.append((bi // 2 == bj // 2) & (bi % 2 == 1) & (bj % 2 == 0))
        blk *= 2

    zero = jnp.zeros((n, PAIR), BF16)

    def stack(x, sl):
        return jnp.concatenate([jnp.where(first, x[:, sl], zero), jnp.where(first, zero, x[:, sl])], axis=0)

    zero2 = jnp.zeros((2 * n, 2 * n), BF16)
    chains = [(b, p) for b in range(bsz) for p in range(N_HEADS // 2)]
    for g0 in range(0, len(chains), SCAN_GROUP):
        grp = chains[g0:g0 + SCAN_GROUP]
        sls = [slice(p * PAIR, (p + 1) * PAIR) for _, p in grp]
        opnd = [[stack(t, sl) for t in wide[b]] for (b, _), sl in zip(grp, sls)]
        s2 = [s_ref[b, p] for b, p in grp]
        s2b = [s.astype(BF16) for s in s2]
        aa = [_dot_nt(jnp.concatenate([am, rm], axis=0), jnp.concatenate([bm, km], axis=0))
              for am, rm, bm, km, _, _, _ in opnd]
        a_ab = [jnp.where(strict, a[:2 * n, :2 * n], 0.0) for a in aa]
        a_abb = [a.astype(BF16) for a in a_ab]
        a_ak = [jnp.where(strict, a[:2 * n, 2 * n:], 0.0).astype(BF16) for a in aa]
        a_r = [jnp.concatenate([jnp.where(incl, a[2 * n:, :2 * n], 0.0), jnp.where(incl, a[2 * n:, 2 * n:], 0.0)],
                               axis=1).astype(BF16) for a in aa]
        x = [_dot_nt(o[0], sb) + _dot(ak, o[4]) for o, sb, ak in zip(opnd, s2b, a_ak)]
        tinv = [eye + jnp.where(levels[0], a, 0.0) for a in a_ab]
        for m in levels[1:]:
            tb = [t.astype(BF16) for t in tinv]
            half = [_dot(t, jnp.where(m, a, zero2)) for t, a in zip(tb, a_abb)]
            tinv = [t + _dot(h.astype(BF16), q) for t, h, q in zip(tinv, half, tb)]
        uv = [jnp.concatenate([_dot(t.astype(BF16), xx.astype(BF16)).astype(BF16), o[4]], axis=0)
              for t, xx, o in zip(tinv, x, opnd)]
        for j, (b, p) in enumerate(grp):
            rm, blm, klm = opnd[j][1], opnd[j][5], opnd[j][6]
            o = _dot_nt(rm, s2b[j]) + _dot(a_r[j], uv[j])
            y_ref[0, b, :, sls[j]] = (o[:n] + o[n:]).astype(BF16)
            s_ref[b, p] = s2[j] * gls[b][:, sls[j]] + _dot_tn(uv[j], jnp.concatenate([blm, klm], axis=0))


def _seq_mixers(r, v, kk, lw, kd, bb, z, conv_w, conv_b, wa, ba, wx, bx, lam, n_ctx):
    bsz, t, d = r.shape
    nc = t // CHUNK
    nctx = n_ctx // CHUNK
    hb = CHUNK // 8
    gw = wa.shape[-1]

    def blk_of(dd, i):
        return _seq_block(dd, i, nc, nctx)

    tok = pl.BlockSpec((bsz, CHUNK, d), lambda dd, i: (0, blk_of(dd, i), 0))
    tok2 = pl.BlockSpec((1, bsz, CHUNK, d), lambda dd, i: (dd, 0, blk_of(dd, i), 0))
    vec = lambda: pl.BlockSpec((1, 1, d), lambda dd, i: (dd, 0, 0))
    wspec = lambda: pl.BlockSpec((1, d // gw, gw, gw), lambda dd, i: (dd, 0, 0, 0))
    out = jax.ShapeDtypeStruct((2, bsz, t, d), BF16)
    return pl.pallas_call(
        functools.partial(_seq_kernel, nc=nc, nctx=nctx),
        out_shape=(out, out),
        grid=(2, nc),
        in_specs=[tok, tok, tok, tok2, tok2, tok2,
                  pl.BlockSpec((1, bsz, CHUNK, d), lambda dd, i: (G_XB, 0, blk_of(dd, i), 0)),
                  pl.BlockSpec((1, bsz, 8, d), lambda dd, i: (G_XB, 0, jnp.maximum(blk_of(dd, i) * hb - 1, 0), 0)),
                  pl.BlockSpec((1, bsz, 8, d),
                               lambda dd, i: (G_XB, 0, jnp.minimum((blk_of(dd, i) + 1) * hb, t // 8 - 1), 0)),
                  pl.BlockSpec((4, d), lambda dd, i: (0, 0)),
                  pl.BlockSpec((1, d), lambda dd, i: (0, 0)),
                  wspec(), vec(), wspec(), vec(), vec()],
        out_specs=(tok2, tok2),
        scratch_shapes=[pltpu.VMEM((bsz, N_HEADS // 2, PAIR, PAIR), F32),
                        pltpu.VMEM((bsz, CHUNK, d), F32), pltpu.VMEM((bsz, CHUNK, d), F32),
                        pltpu.VMEM((bsz, 1, d), F32)],
        compiler_params=_cparams(("arbitrary", "arbitrary")),
        name="seq_mixers",
    )(r, v, kk, lw, kd, bb, z, z, z, conv_w, conv_b.reshape(1, d), wa, ba.reshape(2, 1, d), wx, bx.reshape(2, 1, d),
      lam.reshape(2, 1, d))


def _gelu_tanh(x):
    return 0.5 * x * (1.0 + jnp.tanh(math.sqrt(2.0 / math.pi) * (x + 0.044715 * (x * x * x))))


def _merge_kernel(x_ref, ya_ref, hl_ref, ug_ref, yw_ref, bonus_ref, g_ref, ma_ref, mb_ref, mc_ref,
                  mod_ref, lnw_ref, lnb_ref, wa_ref, wb_ref, wc_ref, wo_ref, o_ref, *, tm, n_ctx):
    i = pl.program_id(1)
    yb = ((hl_ref[0, 0].astype(F32) + hl_ref[1, 0].astype(F32)) * _gelu_tanh(ug_ref[0, 0])).astype(BF16)
    y = yw_ref[0, 0].astype(F32) + yw_ref[1, 0].astype(F32)
    inv = 1.0 / HEAD_DIM
    mean = _seg_sum64(y) * inv
    yc = y - mean
    var = _seg_sum64(yc * yc) * inv
    yn = yc * lax.rsqrt(var + LNX_EPS) * lnw_ref[...] + lnb_ref[...]
    oc = ((yn + bonus_ref[0].astype(F32)) * g_ref[0].astype(F32)).astype(BF16)
    mixed = (_sigmoid(ma_ref[0, 0]) * _dot(ya_ref[0], wa_ref[...])
             + _sigmoid(mb_ref[0, 0]) * _dot(yb, wb_ref[...])
             + _sigmoid(mc_ref[0, 0]) * _dot(oc, wc_ref[...]))
    yo = _dot(mixed.astype(BF16), wo_ref[...])
    rows = i * tm + lax.broadcasted_iota(jnp.int32, (tm, 1), 0)
    o_ref[0] = x_ref[0] + _row_select(mod_ref, 2, rows, n_ctx) * yo


def _merge(x, ynat, hlru, z, yrw, bonus, g, mods, ln_w, ln_b, wa, wb, wc, wo, n_ctx):
    bsz, t, d = x.shape
    tm = _pick(t, (384, 256))
    tok = pl.BlockSpec((1, tm, d), lambda b, i: (b, i, 0))
    tok2 = pl.BlockSpec((2, 1, tm, d), lambda b, i: (0, b, i, 0))
    zspec = lambda gidx: pl.BlockSpec((1, 1, tm, d), lambda b, i: (gidx, b, i, 0))
    vec = pl.BlockSpec((1, d), lambda b, i: (0, 0))
    wspec = pl.BlockSpec((d, d), lambda b, i: (0, 0))
    return pl.pallas_call(
        functools.partial(_merge_kernel, tm=tm, n_ctx=n_ctx),
        out_shape=jax.ShapeDtypeStruct((bsz, t, d), F32),
        grid=(bsz, t // tm),
        in_specs=[tok, tok, tok2, zspec(G_GB), tok2, tok, tok, zspec(G_MA), zspec(G_MB), zspec(G_MC),
                  pl.BlockSpec((1, 2, 6, d), lambda b, i: (b, 0, 0, 0)), vec, vec,
                  wspec, wspec, wspec, wspec],
        out_specs=tok,
        compiler_params=_cparams(("parallel", "parallel")),
        name="merge",
    )(x, ynat, hlru, z, yrw, bonus, g, z, z, z, mods, ln_w.reshape(1, d), ln_b.reshape(1, d), wa, wb, wc, wo)


def _mlp_kernel(x_ref, g_ref, mod_ref, w1_ref, w2_ref, fg_ref, o_ref, hb_ref, acc_ref, *, tm, n_ctx, final):
    i = pl.program_id(1)
    k = pl.program_id(2)
    rows = i * tm + lax.broadcasted_iota(jnp.int32, (tm, 1), 0)

    @pl.when(k == 0)
    def _():
        x = x_ref[0]
        ms = jnp.mean(x * x, axis=-1, keepdims=True)
        y = x * lax.rsqrt(ms + NORM_EPS) * g_ref[...]
        sh = _row_select(mod_ref, 3, rows, n_ctx)
        sc = _row_select(mod_ref, 4, rows, n_ctx)
        hb_ref[...] = (y * (1.0 + sc) + sh).astype(BF16)
        acc_ref[...] = jnp.zeros_like(acc_ref)

    a = jnp.maximum(_dot(hb_ref[...], w1_ref[...]), 0.0)
    acc_ref[...] += _dot((a * a).astype(BF16), w2_ref[...])

    @pl.when(k == pl.num_programs(2) - 1)
    def _():
        xn = x_ref[0] + _row_select(mod_ref, 5, rows, n_ctx) * acc_ref[...]
        if final:
            ms = jnp.mean(xn * xn, axis=-1, keepdims=True)
            xn = xn * lax.rsqrt(ms + NORM_EPS) * fg_ref[...]
        o_ref[0] = xn


def _mlp(x, g, mods, w1, w2, final_g, n_ctx, final):
    bsz, t, d = x.shape
    dff = w1.shape[1]
    tk = 1024
    tm = _pick(t, (1408, 768, 256))
    vec = pl.BlockSpec((1, d), lambda b, i, k: (0, 0))
    tok = pl.BlockSpec((1, tm, d), lambda b, i, k: (b, i, 0))
    return pl.pallas_call(
        functools.partial(_mlp_kernel, tm=tm, n_ctx=n_ctx, final=final),
        out_shape=jax.ShapeDtypeStruct((bsz, t, d), F32),
        grid=(bsz, t // tm, dff // tk),
        in_specs=[tok, vec, pl.BlockSpec((1, 2, 6, d), lambda b, i, k: (b, 0, 0, 0)),
                  pl.BlockSpec((d, tk), lambda b, i, k: (0, k)),
                  pl.BlockSpec((tk, d), lambda b, i, k: (k, 0)), vec],
        out_specs=tok,
        scratch_shapes=[pltpu.VMEM((tm, d), BF16), pltpu.VMEM((tm, d), F32)],
        compiler_params=_cparams(("parallel", "parallel", "arbitrary")),
        name="mlp",
    )(x, g.reshape(1, d), mods, w1, w2, final_g.reshape(1, d))


def _lowrank_weights(mu_h, w1, a1, g1, v1, mu_v):
    d = mu_h.shape[-1]
    blocks = [(w1[0], mu_h[0]), (w1[1], mu_h[0]), (a1[0], mu_h[1]), (a1[1], mu_h[1]), (g1, mu_h[2])]
    if v1 is not None:
        blocks.append((v1, mu_v))
    p = jnp.concatenate([w * (1.0 - m)[:, None] for w, m in blocks], axis=1)
    q = jnp.concatenate([w * m[:, None] for w, m in blocks], axis=1)
    pad = lambda a: jnp.pad(a, ((0, 0), (0, LR_W - a.shape[1])))
    return jnp.concatenate([pad(p), pad(q)], axis=1)


def _second_stage_weights(w2, a2, g2, v2):
    d = w2.shape[-1]

    def place(w, off):
        return jnp.zeros((LR_SLAB, d), F32).at[off:off + w.shape[0]].set(w)

    v2p = place(v2, LR_GATE_END - LR_SLAB) if v2 is not None else jnp.zeros((LR_SLAB, d), F32)
    return jnp.stack([place(w2[0], 0), place(w2[1], R_DECAY), place(a2[0], LR_TANH_END),
                      place(a2[1], LR_TANH_END + R_ICLR), place(g2, LR_GATE_START - LR_SLAB), v2p]).astype(BF16)


def kernel(x, c, ctx, c_ctx, w_ada, b_ada, norm1_g, norm2_g, w_in, rpb, conv_w, conv_b, rg_wa, rg_ba, rg_wx, rg_bx, rg_lam, rw_mu_rkv, rw_mu_h, rw_w0, rw_w1, rw_w2, rw_a0, rw_a1, rw_a2, rw_g1, rw_g2, rw_k_k, rw_k_a, rw_r_k, rw_ln_w, rw_ln_b, vres_v0, vres_v1, vres_v2, vres_mu, w_br_a, w_br_b, w_br_c, w_out, w_ff1, w_ff2, final_g):
    bsz, seq, d = x.shape
    n_ctx = ctx.shape[1]
    depth = w_in.shape[0]
    t = n_ctx + seq
    xs = jnp.concatenate([ctx, x], axis=1)
    cvec = jnp.zeros((8, d), F32).at[:bsz].set(c).at[bsz].set(c_ctx)
    tables = _rope_tables(t, n_ctx)
    vfirst = None
    w_in_b, w_a_b, w_b_b, w_c_b, w_o_b, w_f1_b, w_f2_b = (
        w.astype(BF16) for w in (w_in, w_br_a, w_br_b, w_br_c, w_out, w_ff1, w_ff2))
    for i in range(depth):
        last = i == depth - 1
        mod = _ada(cvec, w_ada[i], b_ada[i])
        mods = jnp.stack([jnp.broadcast_to(mod[bsz].reshape(1, 6, d), (bsz, 6, d)),
                          mod[:bsz].reshape(bsz, 6, d)], axis=1)
        if i == 0:
            lr_w = _lowrank_weights(rw_mu_h[i], rw_w1[i], rw_a1[i], rw_g1[i], None, None)
            w2cat = _second_stage_weights(rw_w2[i], rw_a2[i], rw_g2[i], None)
            v0 = jnp.zeros((d,), F32)
        else:
            lr_w = _lowrank_weights(rw_mu_h[i], rw_w1[i], rw_a1[i], rw_g1[i], vres_v1[i - 1], vres_mu[i - 1])
            w2cat = _second_stage_weights(rw_w2[i], rw_a2[i], rw_g2[i], vres_v2[i - 1])
            v0 = vres_v0[i - 1]
        w_all = jnp.concatenate([w_in_b[i], lr_w.astype(BF16)], axis=1)
        z, qr, kr, vb = _inproj(xs, norm1_g[i], mods, w_all, tables, n_ctx)
        ynat = _natten(qr, kr, vb, _natten_table(rpb[i]), n_ctx)
        r, v, kk, g, bonus, lw, kd, bb = _rwkv_prep(
            z, vfirst, rw_mu_rkv[i], rw_k_k[i], rw_k_a[i], rw_r_k[i].reshape(-1), rw_w0[i], rw_a0[i], v0,
            w2cat, n_ctx)
        if i == 0:
            vfirst = v
        yrw, hlru = _seq_mixers(r, v, kk, lw, kd, bb, z, conv_w[i], conv_b[i],
                                _block_diag_groups(rg_wa[i], 4).astype(BF16), rg_ba[i],
                                _block_diag_groups(rg_wx[i], 4).astype(BF16), rg_bx[i], rg_lam[i], n_ctx)
        xs = _merge(xs, ynat, hlru, z, yrw, bonus, g, mods, rw_ln_w[i], rw_ln_b[i],
                    w_a_b[i], w_b_b[i], w_c_b[i], w_o_b[i], n_ctx)
        xs = _mlp(xs, norm2_g[i], mods, w_f1_b[i], w_f2_b[i], final_g, n_ctx, last)
    return xs[:, n_ctx:]
```
